```python
import jax
import jax.numpy as jnp
from jax import lax
import numpy as np

D_MODEL = 1024
BATCH = 8
SEQ = 2048
DEPTH = 4

GRID_W = 64
CTX_LEN = 256
EPS = 1e-6
N_MOD = 6

A_HEADS = 4
A_HEAD_DIM = 128
A_CHUNK = 128
A_WIDTH = A_HEADS * A_HEAD_DIM

B_HEADS = 4
B_DK = 64
B_DV = 128
B_QK = B_HEADS * B_DK
B_V = B_HEADS * B_DV
B_GATE_RANK = 16
B_GATE_TEMP = 16.0
B_CHUNK = 64
B_COLS = 2 * B_QK + 2 * B_V + 2 * B_GATE_RANK

EVEN_IN = 2 * A_WIDTH + B_COLS
EVEN_OUT = A_WIDTH + B_V

C_HEADS = 16
C_NOPE = 64
C_ROPE = 32
C_VDIM = 64
C_Q_RANK = 256
C_KV_RANK = 256
ODD_SPLITS = (C_Q_RANK, C_KV_RANK, C_ROPE)
ODD_IN = C_Q_RANK + C_KV_RANK + C_ROPE
C_OUT = C_HEADS * C_VDIM
C_SCALE = (C_NOPE + C_ROPE) ** -0.5
Q_BLOCK = 128
ROPE_BASE = 10000.0

FF_HIDDEN = 2816
N_EXPERTS = 8
TOP_K = 2
MOE_HIDDEN = 3584

N_EVEN = (DEPTH + 1) // 2
N_ODD = DEPTH // 2

kernel_name = 'hybrid_dit_gmlp_gla_mla_moe'


def split_cols(p, sizes):
    return jnp.split(p, np.cumsum(sizes)[:-1].tolist(), axis=-1)


def rms_norm(x, g):
    xf = x.astype(jnp.float32)
    y = xf * lax.rsqrt(jnp.mean(xf * xf, axis=-1, keepdims=True) + EPS)
    return (y * g.astype(jnp.float32)).astype(x.dtype)


def layer_norm(x, g, b):
    xf = x.astype(jnp.float32)
    mu = jnp.mean(xf, axis=-1, keepdims=True)
    var = jnp.mean(jnp.square(xf - mu), axis=-1, keepdims=True)
    y = (xf - mu) * lax.rsqrt(var + EPS)
    return (y * g.astype(jnp.float32) + b.astype(jnp.float32)).astype(x.dtype)


def modulate(h, shift, scale):
    return h * (1 + scale) + shift


def axial_rope_angles(grid_rows):
    half = C_ROPE // 2
    inv_freq = ROPE_BASE ** (-jnp.arange(0, half, 2, dtype=jnp.float32) / half)
    rows = jnp.repeat(jnp.arange(grid_rows, dtype=jnp.float32), GRID_W)
    cols = jnp.tile(jnp.arange(GRID_W, dtype=jnp.float32), grid_rows)
    return rows[:, None] * inv_freq, cols[:, None] * inv_freq


def rope_rotate(x, ang):
    x1, x2 = jnp.split(x, 2, axis=-1)
    cos = jnp.cos(ang).astype(x.dtype)
    sin = jnp.sin(ang).astype(x.dtype)
    return jnp.concatenate([x1 * cos - x2 * sin, x2 * cos + x1 * sin], axis=-1)


def axial_rope(x, ang_r, ang_c):
    shape = (x.shape[1],) + (1,) * (x.ndim - 3) + (ang_r.shape[-1],)
    xr, xc = jnp.split(x, 2, axis=-1)
    return jnp.concatenate([rope_rotate(xr, ang_r.reshape(shape)),
                            rope_rotate(xc, ang_c.reshape(shape))], axis=-1)


def chunk_mlp(zu, zv, ln_g, ln_b, ws, bs):
    bsz, L, _ = zu.shape
    u = jax.nn.gelu(zu)
    v = layer_norm(jax.nn.gelu(zv), ln_g, ln_b)
    v = v.reshape(bsz, L // A_CHUNK, A_CHUNK, A_HEADS, A_HEAD_DIM)
    mixed = jnp.einsum('hpq,bnqhd->bnphd', ws, v) + bs.T[None, None, :, :, None]
    return u * mixed.reshape(bsz, L, A_WIDTH)


def gla_scan(q, k, v, g, s0, need_out):
    bsz, L, H, _ = q.shape
    n = L // B_CHUNK

    def chunks(t):
        return t.reshape(bsz, n, B_CHUNK, H, t.shape[-1])

    q, k, v, g = chunks(q), chunks(k), chunks(v), chunks(g)
    b = jnp.cumsum(g, axis=2)
    b_last = b[:, :, -1:]
    chunk_states = jnp.einsum('bnjhk,bnjhv->bnhkv', k * jnp.exp(b_last - b), v)
    decay = jnp.exp(b_last[:, :, 0])

    def step(s, inp):
        dec, add = inp
        return s * dec[..., None] + add, (s if need_out else None)

    s_final, s_prev = lax.scan(step, s0, (jnp.moveaxis(decay, 1, 0), jnp.moveaxis(chunk_states, 1, 0)))
    if not need_out:
        return None, s_final
    q_dec = q * jnp.exp(b)
    k_inv = k * jnp.exp(-b)
    scores = jnp.einsum('bnihk,bnjhk->bnhij', q_dec, k_inv)
    scores = jnp.where(jnp.tril(jnp.ones((B_CHUNK, B_CHUNK), dtype=bool)), scores, 0.0)
    o_intra = jnp.einsum('bnhij,bnjhv->bnihv', scores, v)
    o_inter = jnp.einsum('bnihk,bnhkv->bnihv', q_dec, jnp.moveaxis(s_prev, 0, 1))
    return (o_intra + o_inter).reshape(bsz, L, H, v.shape[-1]), s_final


def gla_mixer(pc, pl, gate_w, gate_b, norm_g, need_ctx):
    def prep(p):
        bsz, L, _ = p.shape
        q, k, v, r, gf = split_cols(p, (B_QK, B_QK, B_V, B_V, 2 * B_GATE_RANK))
        gf = gf.reshape(bsz, L, 2, B_GATE_RANK)
        logit = jnp.einsum('bldr,drk->bldk', gf, gate_w) + gate_b
        g = jax.nn.log_sigmoid(logit.astype(jnp.float32)) / B_GATE_TEMP
        g = g.reshape(bsz, L, 2, B_HEADS, B_DK)
        q = q.astype(jnp.float32).reshape(bsz, L, B_HEADS, B_DK) * (B_DK ** -0.5)
        k = k.astype(jnp.float32).reshape(bsz, L, B_HEADS, B_DK)
        v = v.astype(jnp.float32).reshape(bsz, L, B_HEADS, B_DV)
        return q, k, v, r, g[:, :, 0], g[:, :, 1]

    def flip(t):
        return jnp.flip(t, axis=1)

    qc, kc, vc, rc, gc_f, gc_b = prep(pc)
    ql, kl, vl, rl, gl_f, gl_b = prep(pl)
    s0 = jnp.zeros((pl.shape[0], B_HEADS, B_DK, B_DV), jnp.float32)
    oc_f, sc_f = gla_scan(qc, kc, vc, gc_f, s0, need_ctx)
    oc_b, sc_b = gla_scan(flip(qc), flip(kc), flip(vc), flip(gc_b), s0, need_ctx)
    ol_f, _ = gla_scan(ql, kl, vl, gl_f, sc_f, True)
    ol_b, _ = gla_scan(flip(ql), flip(kl), flip(vl), flip(gl_b), sc_b, True)

    def finish(o, r):
        bsz, L = o.shape[:2]
        o = rms_norm(o, norm_g).reshape(bsz, L, B_V).astype(r.dtype)
        return jax.nn.silu(r) * o

    y_l = finish(ol_f + flip(ol_b), rl)
    y_c = finish(oc_f + flip(oc_b), rc) if need_ctx else None
    return y_c, y_l


def even_mixer(nc, nl, w_in, a_ln_g, a_ln_b, a_ws, a_bs, b_gate_w, b_gate_b, b_norm_g, w_out, need_ctx):
    ua_c, va_c, pb_c = split_cols(nc @ w_in, (A_WIDTH, A_WIDTH, B_COLS))
    ua_l, va_l, pb_l = split_cols(nl @ w_in, (A_WIDTH, A_WIDTH, B_COLS))
    yb_c, yb_l = gla_mixer(pb_c, pb_l, b_gate_w, b_gate_b, b_norm_g, need_ctx)
    ya_l = chunk_mlp(ua_l, va_l, a_ln_g, a_ln_b, a_ws, a_bs)
    y_l = jnp.concatenate([ya_l, yb_l], axis=-1) @ w_out
    if not need_ctx:
        return None, y_l
    ya_c = chunk_mlp(ua_c, va_c, a_ln_g, a_ln_b, a_ws, a_bs)
    y_c = jnp.concatenate([ya_c, yb_c], axis=-1) @ w_out
    return y_c, y_l


def mla_project(n, w_in, q_norm_g, w_uq, kv_norm_g, w_ukv):
    bsz, L, _ = n.shape
    cq, ckv, k_rope = split_cols(n @ w_in, ODD_SPLITS)
    q = (rms_norm(cq, q_norm_g) @ w_uq).reshape(bsz, L, C_HEADS, C_NOPE + C_ROPE)
    kv = (rms_norm(ckv, kv_norm_g) @ w_ukv).reshape(bsz, L, C_HEADS, C_NOPE + C_VDIM)
    return q[..., :C_NOPE], q[..., C_NOPE:], kv[..., :C_NOPE], k_rope, kv[..., C_NOPE:]


def mla_attend(q_nope, q_rope, k_nope, k_rope, v):
    s = (jnp.einsum('bqhd,bkhd->bhqk', q_nope, k_nope)
         + jnp.einsum('bqhr,bkr->bhqk', q_rope, k_rope))
    p = jax.nn.softmax(s.astype(jnp.float32) * C_SCALE, axis=-1).astype(v.dtype)
    return jnp.einsum('bhqk,bkhd->bqhd', p, v)


def mla_mixer(nc, nl, w_in, q_norm_g, w_uq, kv_norm_g, w_ukv, w_out, ang_r, ang_c, need_ctx):
    bsz, L, _ = nl.shape
    qn_c, qr_c, kn_c, kr_c, v_c = mla_project(nc, w_in, q_norm_g, w_uq, kv_norm_g, w_ukv)
    qn_l, qr_l, kn_l, kr_l, v_l = mla_project(nl, w_in, q_norm_g, w_uq, kv_norm_g, w_ukv)
    qr_l = axial_rope(qr_l, ang_r, ang_c)
    kr_l = axial_rope(kr_l, ang_r, ang_c)
    k_nope = jnp.concatenate([kn_c, kn_l], axis=1)
    k_rope = jnp.concatenate([kr_c, kr_l], axis=1)
    v = jnp.concatenate([v_c, v_l], axis=1)
    n_blocks = L // Q_BLOCK

    def to_blocks(t):
        return jnp.moveaxis(t.reshape((bsz, n_blocks, Q_BLOCK) + t.shape[2:]), 1, 0)

    o_l = lax.map(lambda qb: mla_attend(qb[0], qb[1], k_nope, k_rope, v),
                  (to_blocks(qn_l), to_blocks(qr_l)))
    y_l = jnp.moveaxis(o_l, 0, 1).reshape(bsz, L, C_OUT) @ w_out
    if not need_ctx:
        return None, y_l
    o_c = mla_attend(qn_c, qr_c, kn_c, kr_c, v_c)
    y_c = o_c.reshape(bsz, nc.shape[1], C_OUT) @ w_out
    return y_c, y_l


def swiglu(h, w_gate, w_up, w_down):
    return (jax.nn.silu(h @ w_gate) * (h @ w_up)) @ w_down


def moe_swiglu(h, router, w_gate, w_up, w_down):
    logits = (h @ router).astype(jnp.float32)
    top_val, top_idx = lax.top_k(logits, TOP_K)
    top_w = jax.nn.softmax(top_val, axis=-1)
    gates = jnp.einsum('...k,...ke->...e', top_w,
                       jax.nn.one_hot(top_idx, N_EXPERTS, dtype=jnp.float32)).astype(h.dtype)
    y = jnp.zeros_like(h)
    for e in range(N_EXPERTS):
        y = y + gates[..., e:e + 1] * swiglu(h, w_gate[e], w_up[e], w_down[e])
    return y


def setup_inputs(seed: int = 0) -> dict:
    key = jax.random.key(seed)
    keys = iter(jax.random.split(key, 32))
    D = D_MODEL

    def normal(shape, scale):
        return jax.random.normal(next(keys), shape, jnp.float32) * scale

    def gain(shape):
        return 1.0 + normal(shape, 0.02)

    return {
        'x': normal((BATCH, SEQ, D), 1.0),
        'c': normal((BATCH, D), 1.0),
        'ctx': normal((BATCH, CTX_LEN, D), 1.0),
        'c_ctx': normal((D,), 1.0),
        'mod_w': normal((DEPTH, D, N_MOD * D), 0.5 * D ** -0.5),
        'mod_b': normal((DEPTH, N_MOD * D), 0.02),
        'norm_mix_g': gain((DEPTH, D)),
        'norm_ffn_g': gain((DEPTH, D)),
        'final_g': gain((D,)),
        'ev_w_in': normal((N_EVEN, D, EVEN_IN), D ** -0.5),
        'a_ln_g': gain((N_EVEN, A_WIDTH)),
        'a_ln_b': normal((N_EVEN, A_WIDTH), 0.02),
        'a_ws': normal((N_EVEN, A_HEADS, A_CHUNK, A_CHUNK), A_CHUNK ** -0.5),
        'a_bs': gain((N_EVEN, A_HEADS, A_CHUNK)),
        'b_gate_w': normal((N_EVEN, 2, B_GATE_RANK, B_QK), B_GATE_RANK ** -0.5),
        'b_gate_b': normal((N_EVEN, 2, B_QK), 0.1),
        'b_norm_g': gain((N_EVEN, B_DV)),
        'ev_w_out': normal((N_EVEN, EVEN_OUT, D), EVEN_OUT ** -0.5),
        'od_w_in': normal((N_ODD, D, ODD_IN), D ** -0.5),
        'c_q_norm_g': gain((N_ODD, C_Q_RANK)),
        'c_w_uq': normal((N_ODD, C_Q_RANK, C_HEADS * (C_NOPE + C_ROPE)), C_Q_RANK ** -0.5),
        'c_kv_norm_g': gain((N_ODD, C_KV_RANK)),
        'c_w_ukv': normal((N_ODD, C_KV_RANK, C_HEADS * (C_NOPE + C_VDIM)), C_KV_RANK ** -0.5),
        'od_w_out': normal((N_ODD, C_OUT, D), C_OUT ** -0.5),
        'ff_w_gate': normal((N_EVEN, D, FF_HIDDEN), D ** -0.5),
        'ff_w_up': normal((N_EVEN, D, FF_HIDDEN), D ** -0.5),
        'ff_w_down': normal((N_EVEN, FF_HIDDEN, D), FF_HIDDEN ** -0.5),
        'moe_router': normal((N_ODD, D, N_EXPERTS), D ** -0.5),
        'moe_w_gate': normal((N_ODD, N_EXPERTS, D, MOE_HIDDEN), D ** -0.5),
        'moe_w_up': normal((N_ODD, N_EXPERTS, D, MOE_HIDDEN), D ** -0.5),
        'moe_w_down': normal((N_ODD, N_EXPERTS, MOE_HIDDEN, D), MOE_HIDDEN ** -0.5),
    }


def reference(x, c, ctx, c_ctx, mod_w, mod_b, norm_mix_g, norm_ffn_g, final_g,
              ev_w_in, a_ln_g, a_ln_b, a_ws, a_bs, b_gate_w, b_gate_b, b_norm_g, ev_w_out,
              od_w_in, c_q_norm_g, c_w_uq, c_kv_norm_g, c_w_ukv, od_w_out,
              ff_w_gate, ff_w_up, ff_w_down,
              moe_router, moe_w_gate, moe_w_up, moe_w_down):
    seq = x.shape[1]
    grid_rows = seq // GRID_W
    ang_r, ang_c = axial_rope_angles(grid_rows)
    cond_l = jax.nn.silu(c)[:, None, :]
    cond_c = jax.nn.silu(c_ctx)[None, None, :]
    h_l, h_c = x, ctx
    for layer in range(DEPTH):
        need_ctx = layer < DEPTH - 1
        i = layer // 2
        sh1_l, sc1_l, g1_l, sh2_l, sc2_l, g2_l = jnp.split(cond_l @ mod_w[layer] + mod_b[layer], N_MOD, axis=-1)
        sh1_c, sc1_c, g1_c, sh2_c, sc2_c, g2_c = jnp.split(cond_c @ mod_w[layer] + mod_b[layer], N_MOD, axis=-1)

        n_l = modulate(rms_norm(h_l, norm_mix_g[layer]), sh1_l, sc1_l)
        n_c = modulate(rms_norm(h_c, norm_mix_g[layer]), sh1_c, sc1_c)
        if layer % 2 == 0:
            y_c, y_l = even_mixer(n_c, n_l, ev_w_in[i], a_ln_g[i], a_ln_b[i], a_ws[i], a_bs[i],
                                  b_gate_w[i], b_gate_b[i], b_norm_g[i], ev_w_out[i], need_ctx)
        else:
            y_c, y_l = mla_mixer(n_c, n_l, od_w_in[i], c_q_norm_g[i], c_w_uq[i], c_kv_norm_g[i],
                                 c_w_ukv[i], od_w_out[i], ang_r, ang_c, need_ctx)
        h_l = h_l + g1_l * y_l

        m_l = modulate(rms_norm(h_l, norm_ffn_g[layer]), sh2_l, sc2_l)
        if need_ctx:
            h_c = h_c + g1_c * y_c
            m_c = modulate(rms_norm(h_c, norm_ffn_g[layer]), sh2_c, sc2_c)
            m = jnp.concatenate([m_c, m_l], axis=1)
        else:
            m = m_l
        if layer % 2 == 0:
            f = swiglu(m, ff_w_gate[i], ff_w_up[i], ff_w_down[i])
        else:
            f = moe_swiglu(m, moe_router[i], moe_w_gate[i], moe_w_up[i], moe_w_down[i])
        if need_ctx:
            n_ctx = h_c.shape[1]
            h_c = h_c + g2_c * f[:, :n_ctx]
            h_l = h_l + g2_l * f[:, n_ctx:]
        else:
            h_l = h_l + g2_l * f
    return rms_norm(h_l, final_g)
```

```python
import functools

import numpy as np
import jax
import jax.numpy as jnp
from jax import lax
from jax.experimental import pallas as pl
from jax.experimental.pallas import tpu as pltpu

D_MODEL = 1024
BATCH = 8
SEQ = 2048
DEPTH = 4
GRID_W = 64
CTX_LEN = 256
EPS = 1e-6
N_MOD = 6

A_HEADS = 4
A_HEAD_DIM = 128
A_CHUNK = 128
A_WIDTH = A_HEADS * A_HEAD_DIM

B_HEADS = 4
B_DK = 64
B_DV = 128
B_QK = B_HEADS * B_DK
B_V = B_HEADS * B_DV
B_GATE_RANK = 16
B_GATE_TEMP = 16.0
B_CHUNK = 64

C_HEADS = 16
C_NOPE = 64
C_ROPE = 32
C_VDIM = 64
C_Q_RANK = 256
C_KV_RANK = 256
C_SCALE = (C_NOPE + C_ROPE) ** -0.5
ROPE_BASE = 10000.0

FF_HIDDEN = 2816
N_EXPERTS = 8
MOE_HIDDEN = 3584

NC_ROWS = BATCH * CTX_LEN
NL_ROWS = BATCH * SEQ
N_ROWS = NC_ROWS + NL_ROWS
MOD_ROWS = BATCH + 1
MOD_PAD = 16

LANE = 128
HEAD_PAD = 128
C_QK_PAD = C_HEADS * HEAD_PAD

TM = 512
GLA_BLK = 256
ATT_TQ = 512
ATT_HG = 4
FF_TH = 256
MOE_TM = 1024
MOE_TH = 512
VMEM_LIMIT = 56 * 1024 * 1024

F32 = jnp.float32
BF16 = jnp.bfloat16


def _mm(a, b):
    return jnp.dot(a.astype(BF16), b.astype(BF16), preferred_element_type=F32)


def _mm_nt(a, b):
    return lax.dot_general(a.astype(BF16), b.astype(BF16), (((1,), (1,)), ((), ())),
                           preferred_element_type=F32)


def _mm_tn(a, b):
    return lax.dot_general(a.astype(BF16), b.astype(BF16), (((0,), (0,)), ((), ())),
                           preferred_element_type=F32)


def _sigmoid(x):
    return 1.0 / (1.0 + jnp.exp(-x))


def _silu(x):
    return x * _sigmoid(x)


def _gelu(x):
    return 0.5 * x * (1.0 + jnp.tanh(np.sqrt(2.0 / np.pi) * (x + 0.044715 * (x * x * x))))


def _rms(x, g):
    return x * lax.rsqrt(jnp.mean(x * x, axis=-1, keepdims=True) + EPS) * g


def _split_bf16(x):
    hi = x.astype(BF16)
    lo = (x - hi.astype(F32)).astype(BF16)
    return hi, lo


def _rows(tm, width, tile0=0, col=0):
    return pl.BlockSpec((tm, width), lambda i: (i + tile0, col))


def _const(shape):
    zeros = (0,) * len(shape)
    return pl.BlockSpec(shape, lambda *_: zeros, pipeline_mode=pl.Buffered(1))


def _mod_row(tile, tm):
    return jnp.where(tile < NC_ROWS // tm, BATCH, (tile * tm - NC_ROWS) // SEQ)


def _mod_spec(tm, tile0=0):
    return pl.BlockSpec((N_MOD, None, 1, D_MODEL), lambda i: (0, _mod_row(i + tile0, tm), 0, 0))


def _params(*sem):
    return pltpu.CompilerParams(dimension_semantics=sem, vmem_limit_bytes=VMEM_LIMIT)


def _mod_kernel(c_ref, w_ref, b_ref, o_ref):
    o_ref[...] = _mm(_silu(c_ref[...]), w_ref[...]) + b_ref[...]


def _modulation(cond, mod_w, mod_b):
    tn = 1536
    out = pl.pallas_call(
        _mod_kernel,
        out_shape=jax.ShapeDtypeStruct((DEPTH, MOD_PAD, N_MOD * D_MODEL), F32),
        grid=(DEPTH, N_MOD * D_MODEL // tn),
        in_specs=[
            pl.BlockSpec((MOD_PAD, D_MODEL), lambda l, j: (0, 0)),
            pl.BlockSpec((None, D_MODEL, tn), lambda l, j: (l, 0, j)),
            pl.BlockSpec((None, 1, tn), lambda l, j: (l, 0, j)),
        ],
        out_specs=pl.BlockSpec((None, MOD_PAD, tn), lambda l, j: (l, 0, j)),
        compiler_params=_params("arbitrary", "arbitrary"),
        name="modulation",
    )(cond, mod_w, mod_b.reshape(DEPTH, 1, N_MOD * D_MODEL))
    out = out[:, :MOD_ROWS].reshape(DEPTH, MOD_ROWS, N_MOD, 1, D_MODEL)
    return jnp.transpose(out, (0, 2, 1, 3, 4))


def _even_in_kernel(h_ref, mod_ref, ng_ref, wa_ref, wq_ref, wgf_ref, gw_ref, gb_ref,
                    lng_ref, lnb_ref, ws_ref, bs_ref, ya_ref, qkv_ref, r_ref, g_ref):
    tm = h_ref.shape[0]
    n = _rms(h_ref[...], ng_ref[...]) * (1.0 + mod_ref[1]) + mod_ref[0]
    nb = n.astype(BF16)

    ua_va = jnp.dot(nb, wa_ref[...], preferred_element_type=F32)
    u = _gelu(ua_va[:, :A_WIDTH])
    v = _gelu(ua_va[:, A_WIDTH:])
    mu = jnp.mean(v, axis=-1, keepdims=True)
    vc = v - mu
    var = jnp.mean(vc * vc, axis=-1, keepdims=True)
    v = (vc * lax.rsqrt(var + EPS) * lng_ref[...] + lnb_ref[...]).astype(BF16)
    for c in range(tm // A_CHUNK):
        rows = slice(c * A_CHUNK, (c + 1) * A_CHUNK)
        for h in range(A_HEADS):
            cols = slice(h * A_HEAD_DIM, (h + 1) * A_HEAD_DIM)
            mixed = jnp.dot(ws_ref[h], v[rows, cols], preferred_element_type=F32) + bs_ref[h]
            ya_ref[rows, cols] = (u[rows, cols] * mixed).astype(ya_ref.dtype)

    p = jnp.dot(nb, wq_ref[...], preferred_element_type=F32)
    qkv_ref[...] = p[:, :2 * B_QK + B_V]
    r_ref[...] = p[:, 2 * B_QK + B_V:]
    gf = jnp.dot(nb, wgf_ref[...], preferred_element_type=F32)
    logit = _mm(gf, gw_ref[...]) + gb_ref[...]
    log_sig = jnp.minimum(logit, 0.0) - jnp.log(1.0 + jnp.exp(-jnp.abs(logit)))
    g_ref[...] = log_sig / B_GATE_TEMP


def _even_in(h, mod, ng, wts):
    tm = TM
    n_tiles = N_ROWS // tm
    qkv_w = 2 * B_QK + B_V
    consts = [ng, wts["w_a"], wts["w_q"], wts["w_gf"], wts["gate_w"], wts["gate_b"],
              wts["ln_g"], wts["ln_b"], wts["ws"], wts["bs"]]
    return pl.pallas_call(
        _even_in_kernel,
        out_shape=(jax.ShapeDtypeStruct((N_ROWS, A_WIDTH), BF16),
                   jax.ShapeDtypeStruct((N_ROWS, qkv_w), F32),
                   jax.ShapeDtypeStruct((N_ROWS, B_V), F32),
                   jax.ShapeDtypeStruct((N_ROWS, 2 * B_QK), F32)),
        grid=(n_tiles,),
        in_specs=[_rows(tm, D_MODEL), _mod_spec(tm)] + [_const(a.shape) for a in consts],
        out_specs=(_rows(tm, A_WIDTH), _rows(tm, qkv_w), _rows(tm, B_V), _rows(tm, 2 * B_QK)),
        compiler_params=_params("arbitrary"),
        name="even_in",
    )(h, mod, *consts)


def _gla_direction(q, k, v, g, st_ref, reverse):
    n = GLA_BLK
    nch = n // B_CHUNK
    ri = lax.broadcasted_iota(jnp.int32, (n, n), 0)
    ci = lax.broadcasted_iota(jnp.int32, (n, n), 1)
    same = (ri // B_CHUNK) == (ci // B_CHUNK)
    tri = same & ((ci >= ri) if reverse else (ci <= ri))
    tri_b = jnp.where(tri, 1.0, 0.0).astype(BF16)
    same_b = jnp.where(same, 1.0, 0.0).astype(BF16)

    g_hi, g_lo = _split_bf16(g)
    dot = functools.partial(jnp.dot, preferred_element_type=F32)
    b = dot(tri_b, g_hi) + dot(tri_b, g_lo)
    b_tot = dot(same_b, g_hi) + dot(same_b, g_lo)
    q_dec = (q * (B_DK ** -0.5)) * jnp.exp(b)
    k_inv = k * jnp.exp(-b)
    k_dec = k * jnp.exp(b_tot - b)
    decay = jnp.exp(b_tot)

    lane = lax.broadcasted_iota(jnp.int32, (1, LANE), 1)
    even_lane = lane < B_DK

    outs = [None] * B_HEADS
    order = list(range(nch))[::-1] if reverse else list(range(nch))
    for p in range(B_HEADS // 2):
        lanes = slice(p * LANE, (p + 1) * LANE)
        qp = q_dec[:, lanes]
        q_heads = (jnp.where(even_lane, qp, 0.0).astype(BF16), jnp.where(even_lane, 0.0, qp).astype(BF16))
        k_inv_p = k_inv[:, lanes].astype(BF16)
        k_dec_p = k_dec[:, lanes].astype(BF16)
        intra = []
        for hh in range(2):
            h = 2 * p + hh
            scores = _mm_nt(q_heads[hh], k_inv_p)
            scores = jnp.where(tri, scores, 0.0)
            intra.append(_mm(scores, v[:, h * B_DV:(h + 1) * B_DV]))
        st = st_ref[p]
        inter = [[None] * nch for _ in range(2)]
        for c in order:
            rows = slice(c * B_CHUNK, (c + 1) * B_CHUNK)
            st_b = st.astype(BF16)
            for hh in range(2):
                inter[hh][c] = _mm_nt(q_heads[hh][rows], st_b)
            upd_e = _mm_tn(v[rows, (2 * p) * B_DV:(2 * p + 1) * B_DV], k_dec_p[rows])
            upd_o = _mm_tn(v[rows, (2 * p + 1) * B_DV:(2 * p + 2) * B_DV], k_dec_p[rows])
            st = st * decay[c * B_CHUNK:c * B_CHUNK + 1, lanes] + jnp.where(even_lane, upd_e, upd_o)
        st_ref[p] = st
        for hh in range(2):
            outs[2 * p + hh] = intra[hh] + jnp.concatenate(inter[hh], axis=0)
    return outs


def _gla_kernel(qf_ref, kf_ref, vf_ref, gf_ref, qb_ref, kb_ref, vb_ref, gb_ref,
                of_ref, ob_ref, stf_ref, stb_ref):
    @pl.when(pl.program_id(1) == 0)
    def _():
        stf_ref[...] = jnp.zeros_like(stf_ref)
        stb_ref[...] = jnp.zeros_like(stb_ref)

    outs = _gla_direction(qf_ref[...], kf_ref[...], vf_ref[...], gf_ref[...], stf_ref, False)
    for h in range(B_HEADS):
        of_ref[:, h * B_DV:(h + 1) * B_DV] = outs[h]
    outs = _gla_direction(qb_ref[...], kb_ref[...], vb_ref[...], gb_ref[...], stb_ref, True)
    for h in range(B_HEADS):
        ob_ref[:, h * B_DV:(h + 1) * B_DV] = outs[h]


def _gla(qkv, g):
    nb_ctx = CTX_LEN // GLA_BLK
    nb_lat = SEQ // GLA_BLK
    nblk = nb_ctx + nb_lat
    ctx_blocks = NC_ROWS // GLA_BLK

    def fwd_row(b, j):
        return jnp.where(j < nb_ctx, b * nb_ctx + j, ctx_blocks + b * nb_lat + (j - nb_ctx))

    def bwd_row(b, j):
        return jnp.where(j < nb_ctx, b * nb_ctx + (nb_ctx - 1 - j),
                         ctx_blocks + b * nb_lat + (nblk - 1 - j))

    def spec(width, col, row):
        return pl.BlockSpec((GLA_BLK, width), lambda b, j: (row(b, j), col))

    ins, in_specs = [], []
    for d, row in enumerate((fwd_row, bwd_row)):
        ins += [qkv, qkv, qkv, g]
        in_specs += [spec(B_QK, 0, row), spec(B_QK, 1, row), spec(B_V, 1, row), spec(B_QK, d, row)]
    return pl.pallas_call(
        _gla_kernel,
        out_shape=(jax.ShapeDtypeStruct((N_ROWS, B_V), F32),) * 2,
        grid=(BATCH, nblk),
        in_specs=in_specs,
        out_specs=(spec(B_V, 0, fwd_row), spec(B_V, 0, bwd_row)),
        scratch_shapes=[pltpu.VMEM((B_HEADS // 2, B_DV, LANE), F32)] * 2,
        compiler_params=_params("arbitrary", "arbitrary"),
        name="gla",
    )(*ins)


def _even_out_kernel(h_ref, mod_ref, of_ref, ob_ref, r_ref, ya_ref, bng_ref, woa_ref, wob_ref, out_ref):
    o = of_ref[...] + ob_ref[...]
    parts = [_rms(o[:, h * B_DV:(h + 1) * B_DV], bng_ref[...]) for h in range(B_HEADS)]
    yb = _silu(r_ref[...]) * jnp.concatenate(parts, axis=-1)
    y = jnp.dot(ya_ref[...], woa_ref[...], preferred_element_type=F32) + _mm(yb, wob_ref[...])
    out_ref[...] = h_ref[...] + mod_ref[2] * y


def _even_out(h, mod, o_f, o_b, r, ya, wts):
    tm = TM
    consts = [wts["b_norm_g"], wts["w_out_a"], wts["w_out_b"]]
    return pl.pallas_call(
        _even_out_kernel,
        out_shape=jax.ShapeDtypeStruct((N_ROWS, D_MODEL), F32),
        grid=(N_ROWS // tm,),
        in_specs=[_rows(tm, D_MODEL), _mod_spec(tm), _rows(tm, B_V), _rows(tm, B_V), _rows(tm, B_V),
                  _rows(tm, A_WIDTH)] + [_const(a.shape) for a in consts],
        out_specs=_rows(tm, D_MODEL),
        input_output_aliases={0: 0},
        compiler_params=_params("arbitrary"),
        name="even_out",
    )(h, mod, o_f, o_b, r, ya, *consts)


def _ffn_kernel(h_ref, mod_ref, ng_ref, wg_ref, wu_ref, wd_ref, out_ref):
    x = h_ref[...]
    m = (_rms(x, ng_ref[...]) * (1.0 + mod_ref[4]) + mod_ref[3]).astype(BF16)
    acc = jnp.zeros(x.shape, F32)
    for j in range(wg_ref.shape[0]):
        gate = jnp.dot(m, wg_ref[j], preferred_element_type=F32)
        up = jnp.dot(m, wu_ref[j], preferred_element_type=F32)
        act = (_silu(gate) * up).astype(BF16)
        acc = acc + jnp.dot(act, wd_ref[j], preferred_element_type=F32)
    out_ref[...] = x + mod_ref[5] * acc


def _ffn(h, mod, ng, wts):
    tm = TM
    consts = [ng, wts["ff_g"], wts["ff_u"], wts["ff_d"]]
    return pl.pallas_call(
        _ffn_kernel,
        out_shape=jax.ShapeDtypeStruct((N_ROWS, D_MODEL), F32),
        grid=(N_ROWS // tm,),
        in_specs=[_rows(tm, D_MODEL), _mod_spec(tm)] + [_const(a.shape) for a in consts],
        out_specs=_rows(tm, D_MODEL),
        input_output_aliases={0: 0},
        compiler_params=_params("arbitrary"),
        name="ffn",
    )(h, mod, *consts)


def _odd_in_kernel(h_ref, mod_ref, ng_ref, win_ref, qg_ref, kvg_ref, wq_ref, wqs_ref, wk_ref, wv_ref,
                   place_ref, cq_ref, sq_ref, ck_ref, sk_ref, q_ref, k_ref, v_ref):
    n = _rms(h_ref[...], ng_ref[...]) * (1.0 + mod_ref[1]) + mod_ref[0]
    p = jnp.dot(n.astype(BF16), win_ref[...], preferred_element_type=F32)
    cq = _rms(p[:, :C_Q_RANK], qg_ref[...]).astype(BF16)
    ckv = _rms(p[:, C_Q_RANK:C_Q_RANK + C_KV_RANK], kvg_ref[...]).astype(BF16)
    base = C_Q_RANK + C_KV_RANK
    kr = p[:, base:base + LANE] * ck_ref[...] + p[:, base + LANE:base + 2 * LANE] * sk_ref[...]

    q_plain = jnp.dot(cq, wq_ref[...], preferred_element_type=F32)
    q_swap = jnp.dot(cq, wqs_ref[...], preferred_element_type=F32)
    cq_t, sq_t = cq_ref[...], sq_ref[...]
    for h in range(C_HEADS):
        cols = slice(h * HEAD_PAD, (h + 1) * HEAD_PAD)
        q_ref[:, cols] = ((q_plain[:, cols] * cq_t + q_swap[:, cols] * sq_t) * C_SCALE).astype(q_ref.dtype)
    k_cat = jnp.dot(ckv, wk_ref[...], preferred_element_type=F32) + _mm(kr, place_ref[...])
    k_ref[...] = k_cat.astype(k_ref.dtype)
    v_ref[...] = jnp.dot(ckv, wv_ref[...], preferred_element_type=F32).astype(v_ref.dtype)


def _odd_in(h, mod, ng, wts, rope):
    tm = TM
    n_tiles = N_ROWS // tm
    consts = [ng, wts["w_in"], wts["q_norm_g"], wts["kv_norm_g"], wts["w_q"], wts["w_q_swap"],
              wts["w_k"], wts["w_v"], wts["place"]]
    ctx_tiles = NC_ROWS // tm
    lat_tiles = SEQ // tm

    def rope_spec():
        return pl.BlockSpec((tm, LANE), lambda i: (jnp.where(i < ctx_tiles, lat_tiles, (i - ctx_tiles) % lat_tiles), 0))

    return pl.pallas_call(
        _odd_in_kernel,
        out_shape=(jax.ShapeDtypeStruct((N_ROWS, C_QK_PAD), BF16),) * 3,
        grid=(n_tiles,),
        in_specs=[_rows(tm, D_MODEL), _mod_spec(tm)] + [_const(a.shape) for a in consts] + [rope_spec()] * 4,
        out_specs=(_rows(tm, C_QK_PAD),) * 3,
        compiler_params=_params("arbitrary"),
        name="odd_in",
    )(h, mod, *consts, *rope)


def _attn_kernel(*refs, n_pieces):
    q_ref = refs[0]
    k_refs = refs[1:1 + n_pieces]
    v_refs = refs[1 + n_pieces:1 + 2 * n_pieces]
    o_ref = refs[-1]
    for pair in range(ATT_HG // 2):
        acc = None
        for hh in range(2):
            cols = slice((2 * pair + hh) * HEAD_PAD, (2 * pair + hh + 1) * HEAD_PAD)
            q = q_ref[:, cols]
            s = [lax.dot_general(q, k[:, cols], (((1,), (1,)), ((), ())), preferred_element_type=F32)
                 for k in k_refs]
            m = functools.reduce(jnp.maximum, [jnp.max(x, axis=-1, keepdims=True) for x in s])
            e = [jnp.exp(x - m) for x in s]
            l = functools.reduce(jnp.add, [jnp.sum(x, axis=-1, keepdims=True) for x in e])
            o = functools.reduce(jnp.add, [jnp.dot(x.astype(BF16), v[:, cols], preferred_element_type=F32)
                                           for x, v in zip(e, v_refs)])
            o = o * (1.0 / l)
            acc = o if acc is None else acc + o
        o_ref[:, pair * LANE:(pair + 1) * LANE] = acc.astype(o_ref.dtype)


def _attention(q, k, v, latent, prev=None):
    width = ATT_HG * HEAD_PAD
    n_hg = C_HEADS // ATT_HG
    if latent:
        tq = ATT_TQ
        nq = SEQ // tq
        q_spec = pl.BlockSpec((tq, width), lambda b, g, t: ((NC_ROWS + b * SEQ) // tq + t, g))
        kv_specs = [pl.BlockSpec((CTX_LEN, width), lambda b, g, t: (b, g)),
                    pl.BlockSpec((SEQ, width), lambda b, g, t: (NC_ROWS // SEQ + b, g))]
    else:
        tq = CTX_LEN
        nq = 1
        q_spec = pl.BlockSpec((tq, width), lambda b, g, t: (b, g))
        kv_specs = [pl.BlockSpec((CTX_LEN, width), lambda b, g, t: (b, g))]
    n_pieces = len(kv_specs)
    o_spec = pl.BlockSpec((tq, ATT_HG * C_VDIM), q_spec.index_map)
    ins = [q] + [k] * n_pieces + [v] * n_pieces
    in_specs = [q_spec] + kv_specs + kv_specs
    aliases = {}
    if prev is not None:
        aliases = {len(ins): 0}
        ins.append(prev)
        in_specs.append(pl.BlockSpec(memory_space=pl.ANY))
    return pl.pallas_call(
        functools.partial(_attn_kernel, n_pieces=n_pieces),
        out_shape=jax.ShapeDtypeStruct((N_ROWS, C_HEADS * C_VDIM), BF16),
        grid=(BATCH, n_hg, nq),
        in_specs=in_specs,
        out_specs=o_spec,
        input_output_aliases=aliases,
        compiler_params=_params("arbitrary", "arbitrary", "arbitrary"),
        name="attn_latent" if latent else "attn_context",
    )(*ins)


def _odd_out_kernel(h_ref, mod_ref, o_ref, w_ref, out_ref):
    y = jnp.dot(o_ref[...], w_ref[...], preferred_element_type=F32)
    out_ref[...] = h_ref[...] + mod_ref[2] * y


def _odd_out(h, mod, o, w_out, tile0, n_tiles):
    tm = TM
    return pl.pallas_call(
        _odd_out_kernel,
        out_shape=jax.ShapeDtypeStruct((N_ROWS, D_MODEL), F32),
        grid=(n_tiles,),
        in_specs=[_rows(tm, D_MODEL, tile0), _mod_spec(tm, tile0), _rows(tm, C_HEADS * C_VDIM, tile0),
                  _const(w_out.shape)],
        out_specs=_rows(tm, D_MODEL, tile0),
        input_output_aliases={0: 0},
        compiler_params=_params("arbitrary"),
        name="odd_out",
    )(h, mod, o, w_out)


def _route(m, router):
    m_hi, m_lo = _split_bf16(m)
    r_hi, r_lo = _split_bf16(router)
    dot = functools.partial(jnp.dot, preferred_element_type=F32)
    logits = dot(m_hi, r_hi) + (dot(m_hi, r_lo) + dot(m_lo, r_hi))
    lane = lax.broadcasted_iota(jnp.int32, logits.shape, 1)
    neg = -jnp.inf
    l1 = jnp.where(lane < N_EXPERTS, logits, neg)
    m1 = jnp.max(l1, axis=-1, keepdims=True)
    i1 = jnp.min(jnp.where(l1 == m1, lane, LANE), axis=-1, keepdims=True)
    l2 = jnp.where(lane == i1, neg, l1)
    m2 = jnp.max(l2, axis=-1, keepdims=True)
    i2 = jnp.min(jnp.where(l2 == m2, lane, LANE), axis=-1, keepdims=True)
    e2 = jnp.exp(m2 - m1)
    w1 = 1.0 / (1.0 + e2)
    w2 = e2 / (1.0 + e2)
    return jnp.where(lane == i1, w1, 0.0) + jnp.where(lane == i2, w2, 0.0)


def _moe_kernel(h_ref, mod_ref, ng_ref, rt_ref, wg_ref, wu_ref, wd_ref, out_ref, m_s, gate_s, acc_s):
    e = pl.program_id(1)
    j = pl.program_id(2)

    @pl.when((e == 0) & (j == 0))
    def _():
        m = _rms(h_ref[...], ng_ref[...]) * (1.0 + mod_ref[4]) + mod_ref[3]
        m_s[...] = m.astype(BF16)
        gate_s[...] = _route(m, rt_ref[...])
        acc_s[...] = jnp.zeros_like(acc_s)

    m = m_s[...]
    gate = jnp.dot(m, wg_ref[...], preferred_element_type=F32)
    up = jnp.dot(m, wu_ref[...], preferred_element_type=F32)
    lane = lax.broadcasted_iota(jnp.int32, gate_s.shape, 1)
    w = jnp.sum(jnp.where(lane == e, gate_s[...], 0.0), axis=-1, keepdims=True)
    act = (_silu(gate) * up * w).astype(BF16)
    acc_s[...] += jnp.dot(act, wd_ref[...], preferred_element_type=F32)

    @pl.when((e == pl.num_programs(1) - 1) & (j == pl.num_programs(2) - 1))
    def _():
        out_ref[...] = h_ref[...] + mod_ref[5] * acc_s[...]


def _moe(h, mod, ng, wts, tile0, n_tiles):
    tm, th = MOE_TM, MOE_TH
    nj = MOE_HIDDEN // th
    row = lambda i, e, j: (i + tile0, 0)
    return pl.pallas_call(
        _moe_kernel,
        out_shape=jax.ShapeDtypeStruct((N_ROWS, D_MODEL), F32),
        grid=(n_tiles, N_EXPERTS, nj),
        in_specs=[
            pl.BlockSpec((tm, D_MODEL), row),
            pl.BlockSpec((N_MOD, None, 1, D_MODEL), lambda i, e, j: (0, _mod_row(i + tile0, tm), 0, 0)),
            _const(ng.shape),
            _const(wts["router"].shape),
            pl.BlockSpec((None, D_MODEL, th), lambda i, e, j: (e, 0, j)),
            pl.BlockSpec((None, D_MODEL, th), lambda i, e, j: (e, 0, j)),
            pl.BlockSpec((None, th, D_MODEL), lambda i, e, j: (e, j, 0)),
        ],
        out_specs=pl.BlockSpec((tm, D_MODEL), row),
        scratch_shapes=[pltpu.VMEM((tm, D_MODEL), BF16), pltpu.VMEM((tm, LANE), F32),
                        pltpu.VMEM((tm, D_MODEL), F32)],
        input_output_aliases={0: 0},
        compiler_params=_params("arbitrary", "arbitrary", "arbitrary"),
        name="moe",
    )(h, mod, ng, wts["router"], wts["moe_g"], wts["moe_u"], wts["moe_d"])


def _final_kernel(h_ref, g_ref, o_ref):
    o_ref[...] = _rms(h_ref[...], g_ref[...])


def _final_norm(h, g):
    tm = TM
    return pl.pallas_call(
        _final_kernel,
        out_shape=jax.ShapeDtypeStruct((NL_ROWS, D_MODEL), F32),
        grid=(NL_ROWS // tm,),
        in_specs=[_rows(tm, D_MODEL, NC_ROWS // tm), _const(g.shape)],
        out_specs=_rows(tm, D_MODEL),
        compiler_params=_params("arbitrary"),
        name="final_norm",
    )(h, g)


def _row_vec(x):
    return x.reshape(1, -1).astype(F32)


def _even_weights(i, ev_w_in, a_ln_g, a_ln_b, a_ws, a_bs, b_gate_w, b_gate_b, b_norm_g, ev_w_out,
                  ff_w_gate, ff_w_up, ff_w_down):
    w_in = ev_w_in[i]
    qkvr = 2 * B_QK + 2 * B_V
    gate_w = jnp.zeros((2 * B_GATE_RANK, 2 * B_QK), F32)
    gate_w = gate_w.at[:B_GATE_RANK, :B_QK].set(b_gate_w[i, 0]).at[B_GATE_RANK:, B_QK:].set(b_gate_w[i, 1])
    nch = FF_HIDDEN // FF_TH
    return {
        "w_a": w_in[:, :2 * A_WIDTH].astype(BF16),
        "w_q": w_in[:, 2 * A_WIDTH:2 * A_WIDTH + qkvr].astype(BF16),
        "w_gf": w_in[:, 2 * A_WIDTH + qkvr:].astype(BF16),
        "gate_w": gate_w.astype(BF16),
        "gate_b": b_gate_b[i].reshape(1, 2 * B_QK).astype(F32),
        "ln_g": _row_vec(a_ln_g[i]),
        "ln_b": _row_vec(a_ln_b[i]),
        "ws": a_ws[i].astype(BF16),
        "bs": jnp.broadcast_to(a_bs[i][:, :, None], (A_HEADS, A_CHUNK, A_HEAD_DIM)).astype(F32),
        "b_norm_g": _row_vec(b_norm_g[i]),
        "w_out_a": ev_w_out[i, :A_WIDTH].astype(BF16),
        "w_out_b": ev_w_out[i, A_WIDTH:].astype(BF16),
        "ff_g": jnp.transpose(ff_w_gate[i].reshape(D_MODEL, nch, FF_TH), (1, 0, 2)).astype(BF16),
        "ff_u": jnp.transpose(ff_w_up[i].reshape(D_MODEL, nch, FF_TH), (1, 0, 2)).astype(BF16),
        "ff_d": ff_w_down[i].reshape(nch, FF_TH, D_MODEL).astype(BF16),
    }


def _rope_swap_index():
    quarter = C_ROPE // 4
    idx = np.arange(C_ROPE)
    return np.concatenate([idx[quarter:2 * quarter], idx[:quarter], idx[3 * quarter:], idx[2 * quarter:3 * quarter]])


def _odd_weights(i, od_w_in, c_q_norm_g, c_w_uq, c_kv_norm_g, c_w_ukv, od_w_out,
                 moe_router, moe_w_gate, moe_w_up, moe_w_down):
    swap = _rope_swap_index()
    w_in = od_w_in[i]
    low = C_Q_RANK + C_KV_RANK
    k_rope_w = w_in[:, low:]
    pad = jnp.zeros((D_MODEL, LANE - C_ROPE), F32)
    w_in_x = jnp.concatenate([w_in[:, :low], k_rope_w, pad, k_rope_w[:, swap], pad], axis=1)

    uq = c_w_uq[i].reshape(C_Q_RANK, C_HEADS, C_NOPE + C_ROPE)
    zq = jnp.zeros((C_Q_RANK, C_HEADS, HEAD_PAD - C_NOPE - C_ROPE), F32)
    w_q = jnp.concatenate([uq, zq], axis=-1)
    w_q_swap = jnp.concatenate([jnp.zeros((C_Q_RANK, C_HEADS, C_NOPE), F32), uq[:, :, C_NOPE:][:, :, swap], zq], axis=-1)
    ukv = c_w_ukv[i].reshape(C_KV_RANK, C_HEADS, C_NOPE + C_VDIM)
    w_k = jnp.concatenate([ukv[:, :, :C_NOPE], jnp.zeros((C_KV_RANK, C_HEADS, HEAD_PAD - C_NOPE), F32)], axis=-1)
    zv = jnp.zeros((C_KV_RANK, C_HEADS // 2, C_VDIM), F32)
    uv = ukv[:, :, C_NOPE:].reshape(C_KV_RANK, C_HEADS // 2, 2, C_VDIM)
    w_v = jnp.stack([jnp.concatenate([uv[:, :, 0], zv], axis=-1),
                     jnp.concatenate([zv, uv[:, :, 1]], axis=-1)], axis=2)
    place = np.zeros((LANE, C_HEADS, HEAD_PAD), np.float32)
    for r in range(C_ROPE):
        place[r, :, C_NOPE + r] = 1.0
    router = jnp.concatenate([moe_router[i], jnp.zeros((D_MODEL, LANE - N_EXPERTS), F32)], axis=1)
    return {
        "w_in": w_in_x.astype(BF16),
        "q_norm_g": _row_vec(c_q_norm_g[i]),
        "kv_norm_g": _row_vec(c_kv_norm_g[i]),
        "w_q": w_q.reshape(C_Q_RANK, C_QK_PAD).astype(BF16),
        "w_q_swap": w_q_swap.reshape(C_Q_RANK, C_QK_PAD).astype(BF16),
        "w_k": w_k.reshape(C_KV_RANK, C_QK_PAD).astype(BF16),
        "w_v": w_v.reshape(C_KV_RANK, C_QK_PAD).astype(BF16),
        "place": jnp.asarray(place.reshape(LANE, C_QK_PAD), BF16),
        "w_out": od_w_out[i].astype(BF16),
        "router": router,
        "moe_g": moe_w_gate[i].astype(BF16),
        "moe_u": moe_w_up[i].astype(BF16),
        "moe_d": moe_w_down[i].astype(BF16),
    }


def _rope_tables(tm):
    quarter = C_ROPE // 4
    half = C_ROPE // 2
    inv_freq = ROPE_BASE ** (-jnp.arange(0, half, 2, dtype=F32) / half)
    pos = jnp.arange(SEQ, dtype=jnp.int32)
    ang_r = (pos // GRID_W).astype(F32)[:, None] * inv_freq
    ang_c = (pos % GRID_W).astype(F32)[:, None] * inv_freq
    cos = jnp.concatenate([jnp.cos(ang_r)] * 2 + [jnp.cos(ang_c)] * 2, axis=1)
    sin = jnp.concatenate([-jnp.sin(ang_r), jnp.sin(ang_r), -jnp.sin(ang_c), jnp.sin(ang_c)], axis=1)
    cos = jnp.concatenate([cos, jnp.ones((tm, C_ROPE), F32)], axis=0)
    sin = jnp.concatenate([sin, jnp.zeros((tm, C_ROPE), F32)], axis=0)
    rows = SEQ + tm
    ones = jnp.ones((rows, C_NOPE), F32)
    z = lambda w: jnp.zeros((rows, w), F32)
    cq = jnp.concatenate([ones, cos, z(HEAD_PAD - C_NOPE - C_ROPE)], axis=1)
    sq = jnp.concatenate([z(C_NOPE), sin, z(HEAD_PAD - C_NOPE - C_ROPE)], axis=1)
    ck = jnp.concatenate([cos, z(LANE - C_ROPE)], axis=1)
    sk = jnp.concatenate([sin, z(LANE - C_ROPE)], axis=1)
    return cq, sq, ck, sk


def kernel(x, c, ctx, c_ctx, mod_w, mod_b, norm_mix_g, norm_ffn_g, final_g, ev_w_in, a_ln_g, a_ln_b, a_ws, a_bs, b_gate_w, b_gate_b, b_norm_g, ev_w_out, od_w_in, c_q_norm_g, c_w_uq, c_kv_norm_g, c_w_ukv, od_w_out, ff_w_gate, ff_w_up, ff_w_down, moe_router, moe_w_gate, moe_w_up, moe_w_down):
    h = jnp.concatenate([ctx.reshape(NC_ROWS, D_MODEL), x.reshape(NL_ROWS, D_MODEL)], axis=0)
    cond = jnp.concatenate([c, c_ctx[None, :], jnp.zeros((MOD_PAD - MOD_ROWS, D_MODEL), F32)], axis=0)
    mod = _modulation(cond, mod_w, mod_b)
    rope = _rope_tables(TM)

    for layer in range(DEPTH):
        i = layer // 2
        need_ctx = layer < DEPTH - 1
        ng_mix = _row_vec(norm_mix_g[layer])
        ng_ffn = _row_vec(norm_ffn_g[layer])
        if layer % 2 == 0:
            wts = _even_weights(i, ev_w_in, a_ln_g, a_ln_b, a_ws, a_bs, b_gate_w, b_gate_b, b_norm_g,
                                ev_w_out, ff_w_gate, ff_w_up, ff_w_down)
            ya, qkv, r, g = _even_in(h, mod[layer], ng_mix, wts)
            o_f, o_b = _gla(qkv, g)
            h = _even_out(h, mod[layer], o_f, o_b, r, ya, wts)
            h = _ffn(h, mod[layer], ng_ffn, wts)
        else:
            wts = _odd_weights(i, od_w_in, c_q_norm_g, c_w_uq, c_kv_norm_g, c_w_ukv, od_w_out,
                               moe_router, moe_w_gate, moe_w_up, moe_w_down)
            q, k, v = _odd_in(h, mod[layer], ng_mix, wts, rope)
            o = _attention(q, k, v, latent=True)
            if need_ctx:
                o = _attention(q, k, v, latent=False, prev=o)
            first = 0 if need_ctx else NC_ROWS
            h = _odd_out(h, mod[layer], o, wts["w_out"], first // TM, (N_ROWS - first) // TM)
            h = _moe(h, mod[layer], ng_ffn, wts, first // MOE_TM, (N_ROWS - first) // MOE_TM)
    out = _final_norm(h, _row_vec(final_g))
    return out.reshape(BATCH, SEQ, D_MODEL)
```

```python
import functools

import numpy as np
import jax
import jax.numpy as jnp
from jax import lax
from jax.experimental import pallas as pl
from jax.experimental.pallas import tpu as pltpu

D_MODEL = 1024
BATCH = 8
SEQ = 2048
DEPTH = 4
GRID_W = 64
CTX_LEN = 256
EPS = 1e-6
N_MOD = 6

A_HEADS = 4
A_HEAD_DIM = 128
A_CHUNK = 128
A_WIDTH = A_HEADS * A_HEAD_DIM

B_HEADS = 4
B_DK = 64
B_DV = 128
B_QK = B_HEADS * B_DK
B_V = B_HEADS * B_DV
B_GATE_RANK = 16
B_GATE_TEMP = 16.0
B_CHUNK = 64

C_HEADS = 16
C_NOPE = 64
C_ROPE = 32
C_VDIM = 64
C_Q_RANK = 256
C_KV_RANK = 256
C_SCALE = (C_NOPE + C_ROPE) ** -0.5
ROPE_BASE = 10000.0

FF_HIDDEN = 2816
N_EXPERTS = 8
MOE_HIDDEN = 3584

NC_ROWS = BATCH * CTX_LEN
NL_ROWS = BATCH * SEQ
N_ROWS = NC_ROWS + NL_ROWS
MOD_ROWS = BATCH + 1
MOD_PAD = 16

LANE = 128
HEAD_PAD = 128
C_QK_PAD = C_HEADS * HEAD_PAD

TM = 512
GLA_BLK = 256
ATT_TQ = 512
ATT_HG = 4
FF_TH = 256
MOE_TC = 512
MOE_BLK = 256
MOE_XT = 512
MOE_TH = 512
VMEM_LIMIT = 56 * 1024 * 1024

F32 = jnp.float32
BF16 = jnp.bfloat16


def _mm(a, b):
    return jnp.dot(a.astype(BF16), b.astype(BF16), preferred_element_type=F32)


def _mm_nt(a, b):
    return lax.dot_general(a.astype(BF16), b.astype(BF16), (((1,), (1,)), ((), ())),
                           preferred_element_type=F32)


def _mm_tn(a, b):
    return lax.dot_general(a.astype(BF16), b.astype(BF16), (((0,), (0,)), ((), ())),
                           preferred_element_type=F32)


def _sigmoid(x):
    return 1.0 / (1.0 + jnp.exp(-x))


def _silu(x):
    return x * _sigmoid(x)


def _gelu(x):
    return 0.5 * x * (1.0 + jnp.tanh(np.sqrt(2.0 / np.pi) * (x + 0.044715 * (x * x * x))))


def _rms(x, g):
    return x * lax.rsqrt(jnp.mean(x * x, axis=-1, keepdims=True) + EPS) * g


def _split_bf16(x):
    hi = x.astype(BF16)
    lo = (x - hi.astype(F32)).astype(BF16)
    return hi, lo


def _rows(tm, width, tile0=0, col=0):
    return pl.BlockSpec((tm, width), lambda i: (i + tile0, col))


def _const(shape):
    zeros = (0,) * len(shape)
    return pl.BlockSpec(shape, lambda *_: zeros, pipeline_mode=pl.Buffered(1))


def _mod_row(tile, tm):
    return jnp.where(tile < NC_ROWS // tm, BATCH, (tile * tm - NC_ROWS) // SEQ)


def _mod_spec(tm, tile0=0):
    return pl.BlockSpec((N_MOD, None, 1, D_MODEL), lambda i: (0, _mod_row(i + tile0, tm), 0, 0))


def _params(*sem):
    return pltpu.CompilerParams(dimension_semantics=sem, vmem_limit_bytes=VMEM_LIMIT)


def _mod_kernel(c_ref, w_ref, b_ref, o_ref):
    o_ref[...] = _mm(_silu(c_ref[...]), w_ref[...]) + b_ref[...]


def _modulation(cond, mod_w, mod_b):
    tn = 1536
    out = pl.pallas_call(
        _mod_kernel,
        out_shape=jax.ShapeDtypeStruct((DEPTH, MOD_PAD, N_MOD * D_MODEL), F32),
        grid=(DEPTH, N_MOD * D_MODEL // tn),
        in_specs=[
            pl.BlockSpec((MOD_PAD, D_MODEL), lambda l, j: (0, 0)),
            pl.BlockSpec((None, D_MODEL, tn), lambda l, j: (l, 0, j)),
            pl.BlockSpec((None, 1, tn), lambda l, j: (l, 0, j)),
        ],
        out_specs=pl.BlockSpec((None, MOD_PAD, tn), lambda l, j: (l, 0, j)),
        compiler_params=_params("arbitrary", "arbitrary"),
        name="modulation",
    )(cond, mod_w, mod_b.reshape(DEPTH, 1, N_MOD * D_MODEL))
    out = out[:, :MOD_ROWS].reshape(DEPTH, MOD_ROWS, N_MOD, 1, D_MODEL)
    return jnp.transpose(out, (0, 2, 1, 3, 4))


def _even_in_kernel(h_ref, mod_ref, ng_ref, wa_ref, wq_ref, wgf_ref, gw_ref, gb_ref,
                    lng_ref, lnb_ref, ws_ref, bs_ref, ya_ref, qkv_ref, r_ref, g_ref):
    tm = h_ref.shape[0]
    n = _rms(h_ref[...], ng_ref[...]) * (1.0 + mod_ref[1]) + mod_ref[0]
    nb = n.astype(BF16)

    ua_va = jnp.dot(nb, wa_ref[...], preferred_element_type=F32)
    u = _gelu(ua_va[:, :A_WIDTH])
    v = _gelu(ua_va[:, A_WIDTH:])
    mu = jnp.mean(v, axis=-1, keepdims=True)
    vc = v - mu
    var = jnp.mean(vc * vc, axis=-1, keepdims=True)
    v = (vc * lax.rsqrt(var + EPS) * lng_ref[...] + lnb_ref[...]).astype(BF16)
    for c in range(tm // A_CHUNK):
        rows = slice(c * A_CHUNK, (c + 1) * A_CHUNK)
        for h in range(A_HEADS):
            cols = slice(h * A_HEAD_DIM, (h + 1) * A_HEAD_DIM)
            mixed = jnp.dot(ws_ref[h], v[rows, cols], preferred_element_type=F32) + bs_ref[h]
            ya_ref[rows, cols] = (u[rows, cols] * mixed).astype(ya_ref.dtype)

    p = jnp.dot(nb, wq_ref[...], preferred_element_type=F32)
    qkv_ref[...] = p[:, :2 * B_QK + B_V]
    r_ref[...] = p[:, 2 * B_QK + B_V:]
    gf = jnp.dot(nb, wgf_ref[...], preferred_element_type=F32)
    logit = _mm(gf, gw_ref[...]) + gb_ref[...]
    log_sig = jnp.minimum(logit, 0.0) - jnp.log(1.0 + jnp.exp(-jnp.abs(logit)))
    g_ref[...] = log_sig / B_GATE_TEMP


def _even_in(h, mod, ng, wts):
    tm = TM
    n_tiles = N_ROWS // tm
    qkv_w = 2 * B_QK + B_V
    consts = [ng, wts["w_a"], wts["w_q"], wts["w_gf"], wts["gate_w"], wts["gate_b"],
              wts["ln_g"], wts["ln_b"], wts["ws"], wts["bs"]]
    return pl.pallas_call(
        _even_in_kernel,
        out_shape=(jax.ShapeDtypeStruct((N_ROWS, A_WIDTH), BF16),
                   jax.ShapeDtypeStruct((N_ROWS, qkv_w), F32),
                   jax.ShapeDtypeStruct((N_ROWS, B_V), F32),
                   jax.ShapeDtypeStruct((N_ROWS, 2 * B_QK), F32)),
        grid=(n_tiles,),
        in_specs=[_rows(tm, D_MODEL), _mod_spec(tm)] + [_const(a.shape) for a in consts],
        out_specs=(_rows(tm, A_WIDTH), _rows(tm, qkv_w), _rows(tm, B_V), _rows(tm, 2 * B_QK)),
        compiler_params=_params("arbitrary"),
        name="even_in",
    )(h, mod, *consts)


def _gla_direction(q, k, v, g, st_ref, reverse):
    n = GLA_BLK
    nch = n // B_CHUNK
    ri = lax.broadcasted_iota(jnp.int32, (n, n), 0)
    ci = lax.broadcasted_iota(jnp.int32, (n, n), 1)
    same = (ri // B_CHUNK) == (ci // B_CHUNK)
    tri = same & ((ci >= ri) if reverse else (ci <= ri))
    tri_b = jnp.where(tri, 1.0, 0.0).astype(BF16)
    same_b = jnp.where(same, 1.0, 0.0).astype(BF16)

    g_hi, g_lo = _split_bf16(g)
    dot = functools.partial(jnp.dot, preferred_element_type=F32)
    b = dot(tri_b, g_hi) + dot(tri_b, g_lo)
    b_tot = dot(same_b, g_hi) + dot(same_b, g_lo)
    q_dec = (q * (B_DK ** -0.5)) * jnp.exp(b)
    k_inv = k * jnp.exp(-b)
    k_dec = k * jnp.exp(b_tot - b)
    decay = jnp.exp(b_tot)

    lane = lax.broadcasted_iota(jnp.int32, (1, LANE), 1)
    even_lane = lane < B_DK

    outs = [None] * B_HEADS
    order = list(range(nch))[::-1] if reverse else list(range(nch))
    for p in range(B_HEADS // 2):
        lanes = slice(p * LANE, (p + 1) * LANE)
        qp = q_dec[:, lanes]
        q_heads = (jnp.where(even_lane, qp, 0.0).astype(BF16), jnp.where(even_lane, 0.0, qp).astype(BF16))
        k_inv_p = k_inv[:, lanes].astype(BF16)
        k_dec_p = k_dec[:, lanes].astype(BF16)
        intra = []
        for hh in range(2):
            h = 2 * p + hh
            scores = _mm_nt(q_heads[hh], k_inv_p)
            scores = jnp.where(tri, scores, 0.0)
            intra.append(_mm(scores, v[:, h * B_DV:(h + 1) * B_DV]))
        st = st_ref[p]
        inter = [[None] * nch for _ in range(2)]
        for c in order:
            rows = slice(c * B_CHUNK, (c + 1) * B_CHUNK)
            st_b = st.astype(BF16)
            for hh in range(2):
                inter[hh][c] = _mm_nt(q_heads[hh][rows], st_b)
            upd_e = _mm_tn(v[rows, (2 * p) * B_DV:(2 * p + 1) * B_DV], k_dec_p[rows])
            upd_o = _mm_tn(v[rows, (2 * p + 1) * B_DV:(2 * p + 2) * B_DV], k_dec_p[rows])
            st = st * decay[c * B_CHUNK:c * B_CHUNK + 1, lanes] + jnp.where(even_lane, upd_e, upd_o)
        st_ref[p] = st
        for hh in range(2):
            outs[2 * p + hh] = intra[hh] + jnp.concatenate(inter[hh], axis=0)
    return outs


def _gla_kernel(qf_ref, kf_ref, vf_ref, gf_ref, qb_ref, kb_ref, vb_ref, gb_ref,
                of_ref, ob_ref, stf_ref, stb_ref):
    @pl.when(pl.program_id(1) == 0)
    def _():
        stf_ref[...] = jnp.zeros_like(stf_ref)
        stb_ref[...] = jnp.zeros_like(stb_ref)

    outs = _gla_direction(qf_ref[...], kf_ref[...], vf_ref[...], gf_ref[...], stf_ref, False)
    for h in range(B_HEADS):
        of_ref[:, h * B_DV:(h + 1) * B_DV] = outs[h]
    outs = _gla_direction(qb_ref[...], kb_ref[...], vb_ref[...], gb_ref[...], stb_ref, True)
    for h in range(B_HEADS):
        ob_ref[:, h * B_DV:(h + 1) * B_DV] = outs[h]


def _gla(qkv, g):
    nb_ctx = CTX_LEN // GLA_BLK
    nb_lat = SEQ // GLA_BLK
    nblk = nb_ctx + nb_lat
    ctx_blocks = NC_ROWS // GLA_BLK

    def fwd_row(b, j):
        return jnp.where(j < nb_ctx, b * nb_ctx + j, ctx_blocks + b * nb_lat + (j - nb_ctx))

    def bwd_row(b, j):
        return jnp.where(j < nb_ctx, b * nb_ctx + (nb_ctx - 1 - j),
                         ctx_blocks + b * nb_lat + (nblk - 1 - j))

    def spec(width, col, row):
        return pl.BlockSpec((GLA_BLK, width), lambda b, j: (row(b, j), col))

    ins, in_specs = [], []
    for d, row in enumerate((fwd_row, bwd_row)):
        ins += [qkv, qkv, qkv, g]
        in_specs += [spec(B_QK, 0, row), spec(B_QK, 1, row), spec(B_V, 1, row), spec(B_QK, d, row)]
    return pl.pallas_call(
        _gla_kernel,
        out_shape=(jax.ShapeDtypeStruct((N_ROWS, B_V), F32),) * 2,
        grid=(BATCH, nblk),
        in_specs=in_specs,
        out_specs=(spec(B_V, 0, fwd_row), spec(B_V, 0, bwd_row)),
        scratch_shapes=[pltpu.VMEM((B_HEADS // 2, B_DV, LANE), F32)] * 2,
        compiler_params=_params("arbitrary", "arbitrary"),
        name="gla",
    )(*ins)


def _even_out_kernel(h_ref, mod_ref, of_ref, ob_ref, r_ref, ya_ref, bng_ref, woa_ref, wob_ref, out_ref):
    o = of_ref[...] + ob_ref[...]
    parts = [_rms(o[:, h * B_DV:(h + 1) * B_DV], bng_ref[...]) for h in range(B_HEADS)]
    yb = _silu(r_ref[...]) * jnp.concatenate(parts, axis=-1)
    y = jnp.dot(ya_ref[...], woa_ref[...], preferred_element_type=F32) + _mm(yb, wob_ref[...])
    out_ref[...] = h_ref[...] + mod_ref[2] * y


def _even_out(h, mod, o_f, o_b, r, ya, wts):
    tm = TM
    consts = [wts["b_norm_g"], wts["w_out_a"], wts["w_out_b"]]
    return pl.pallas_call(
        _even_out_kernel,
        out_shape=jax.ShapeDtypeStruct((N_ROWS, D_MODEL), F32),
        grid=(N_ROWS // tm,),
        in_specs=[_rows(tm, D_MODEL), _mod_spec(tm), _rows(tm, B_V), _rows(tm, B_V), _rows(tm, B_V),
                  _rows(tm, A_WIDTH)] + [_const(a.shape) for a in consts],
        out_specs=_rows(tm, D_MODEL),
        input_output_aliases={0: 0},
        compiler_params=_params("arbitrary"),
        name="even_out",
    )(h, mod, o_f, o_b, r, ya, *consts)


def _ffn_kernel(h_ref, mod_ref, ng_ref, wg_ref, wu_ref, wd_ref, out_ref):
    x = h_ref[...]
    m = (_rms(x, ng_ref[...]) * (1.0 + mod_ref[4]) + mod_ref[3]).astype(BF16)
    acc = jnp.zeros(x.shape, F32)
    for j in range(wg_ref.shape[0]):
        gate = jnp.dot(m, wg_ref[j], preferred_element_type=F32)
        up = jnp.dot(m, wu_ref[j], preferred_element_type=F32)
        act = (_silu(gate) * up).astype(BF16)
        acc = acc + jnp.dot(act, wd_ref[j], preferred_element_type=F32)
    out_ref[...] = x + mod_ref[5] * acc


def _ffn(h, mod, ng, wts):
    tm = TM
    consts = [ng, wts["ff_g"], wts["ff_u"], wts["ff_d"]]
    return pl.pallas_call(
        _ffn_kernel,
        out_shape=jax.ShapeDtypeStruct((N_ROWS, D_MODEL), F32),
        grid=(N_ROWS // tm,),
        in_specs=[_rows(tm, D_MODEL), _mod_spec(tm)] + [_const(a.shape) for a in consts],
        out_specs=_rows(tm, D_MODEL),
        input_output_aliases={0: 0},
        compiler_params=_params("arbitrary"),
        name="ffn",
    )(h, mod, *consts)


def _odd_in_kernel(h_ref, mod_ref, ng_ref, win_ref, qg_ref, kvg_ref, wq_ref, wqs_ref, wk_ref, wv_ref,
                   place_ref, cq_ref, sq_ref, ck_ref, sk_ref, q_ref, k_ref, v_ref):
    n = _rms(h_ref[...], ng_ref[...]) * (1.0 + mod_ref[1]) + mod_ref[0]
    p = jnp.dot(n.astype(BF16), win_ref[...], preferred_element_type=F32)
    cq = _rms(p[:, :C_Q_RANK], qg_ref[...]).astype(BF16)
    ckv = _rms(p[:, C_Q_RANK:C_Q_RANK + C_KV_RANK], kvg_ref[...]).astype(BF16)
    base = C_Q_RANK + C_KV_RANK
    kr = p[:, base:base + LANE] * ck_ref[...] + p[:, base + LANE:base + 2 * LANE] * sk_ref[...]

    q_plain = jnp.dot(cq, wq_ref[...], preferred_element_type=F32)
    q_swap = jnp.dot(cq, wqs_ref[...], preferred_element_type=F32)
    cq_t, sq_t = cq_ref[...], sq_ref[...]
    for h in range(C_HEADS):
        cols = slice(h * HEAD_PAD, (h + 1) * HEAD_PAD)
        q_ref[:, cols] = ((q_plain[:, cols] * cq_t + q_swap[:, cols] * sq_t) * C_SCALE).astype(q_ref.dtype)
    k_cat = jnp.dot(ckv, wk_ref[...], preferred_element_type=F32) + _mm(kr, place_ref[...])
    k_ref[...] = k_cat.astype(k_ref.dtype)
    v_ref[...] = jnp.dot(ckv, wv_ref[...], preferred_element_type=F32).astype(v_ref.dtype)


def _odd_in(h, mod, ng, wts, rope):
    tm = TM
    n_tiles = N_ROWS // tm
    consts = [ng, wts["w_in"], wts["q_norm_g"], wts["kv_norm_g"], wts["w_q"], wts["w_q_swap"],
              wts["w_k"], wts["w_v"], wts["place"]]
    ctx_tiles = NC_ROWS // tm
    lat_tiles = SEQ // tm

    def rope_spec():
        return pl.BlockSpec((tm, LANE), lambda i: (jnp.where(i < ctx_tiles, lat_tiles, (i - ctx_tiles) % lat_tiles), 0))

    return pl.pallas_call(
        _odd_in_kernel,
        out_shape=(jax.ShapeDtypeStruct((N_ROWS, C_QK_PAD), BF16),) * 3,
        grid=(n_tiles,),
        in_specs=[_rows(tm, D_MODEL), _mod_spec(tm)] + [_const(a.shape) for a in consts] + [rope_spec()] * 4,
        out_specs=(_rows(tm, C_QK_PAD),) * 3,
        compiler_params=_params("arbitrary"),
        name="odd_in",
    )(h, mod, *consts, *rope)


def _attn_kernel(*refs, n_pieces):
    q_ref = refs[0]
    k_refs = refs[1:1 + n_pieces]
    v_refs = refs[1 + n_pieces:1 + 2 * n_pieces]
    o_ref = refs[-1]
    for pair in range(ATT_HG // 2):
        acc = None
        for hh in range(2):
            cols = slice((2 * pair + hh) * HEAD_PAD, (2 * pair + hh + 1) * HEAD_PAD)
            q = q_ref[:, cols]
            s = [lax.dot_general(q, k[:, cols], (((1,), (1,)), ((), ())), preferred_element_type=F32)
                 for k in k_refs]
            m = functools.reduce(jnp.maximum, [jnp.max(x, axis=-1, keepdims=True) for x in s])
            e = [jnp.exp(x - m) for x in s]
            l = functools.reduce(jnp.add, [jnp.sum(x, axis=-1, keepdims=True) for x in e])
            o = functools.reduce(jnp.add, [jnp.dot(x.astype(BF16), v[:, cols], preferred_element_type=F32)
                                           for x, v in zip(e, v_refs)])
            o = o * (1.0 / l)
            acc = o if acc is None else acc + o
        o_ref[:, pair * LANE:(pair + 1) * LANE] = acc.astype(o_ref.dtype)


def _attention(q, k, v, latent, prev=None):
    width = ATT_HG * HEAD_PAD
    n_hg = C_HEADS // ATT_HG
    if latent:
        tq = ATT_TQ
        nq = SEQ // tq
        q_spec = pl.BlockSpec((tq, width), lambda b, g, t: ((NC_ROWS + b * SEQ) // tq + t, g))
        kv_specs = [pl.BlockSpec((CTX_LEN, width), lambda b, g, t: (b, g)),
                    pl.BlockSpec((SEQ, width), lambda b, g, t: (NC_ROWS // SEQ + b, g))]
    else:
        tq = CTX_LEN
        nq = 1
        q_spec = pl.BlockSpec((tq, width), lambda b, g, t: (b, g))
        kv_specs = [pl.BlockSpec((CTX_LEN, width), lambda b, g, t: (b, g))]
    n_pieces = len(kv_specs)
    o_spec = pl.BlockSpec((tq, ATT_HG * C_VDIM), q_spec.index_map)
    ins = [q] + [k] * n_pieces + [v] * n_pieces
    in_specs = [q_spec] + kv_specs + kv_specs
    aliases = {}
    if prev is not None:
        aliases = {len(ins): 0}
        ins.append(prev)
        in_specs.append(pl.BlockSpec(memory_space=pl.ANY))
    return pl.pallas_call(
        functools.partial(_attn_kernel, n_pieces=n_pieces),
        out_shape=jax.ShapeDtypeStruct((N_ROWS, C_HEADS * C_VDIM), BF16),
        grid=(BATCH, n_hg, nq),
        in_specs=in_specs,
        out_specs=o_spec,
        input_output_aliases=aliases,
        compiler_params=_params("arbitrary", "arbitrary", "arbitrary"),
        name="attn_latent" if latent else "attn_context",
    )(*ins)


def _odd_out_kernel(h_ref, mod_ref, o_ref, w_ref, out_ref):
    y = jnp.dot(o_ref[...], w_ref[...], preferred_element_type=F32)
    out_ref[...] = h_ref[...] + mod_ref[2] * y


def _odd_out(h, mod, o, w_out, tile0, n_tiles):
    tm = TM
    return pl.pallas_call(
        _odd_out_kernel,
        out_shape=jax.ShapeDtypeStruct((N_ROWS, D_MODEL), F32),
        grid=(n_tiles,),
        in_specs=[_rows(tm, D_MODEL, tile0), _mod_spec(tm, tile0), _rows(tm, C_HEADS * C_VDIM, tile0),
                  _const(w_out.shape)],
        out_specs=_rows(tm, D_MODEL, tile0),
        input_output_aliases={0: 0},
        compiler_params=_params("arbitrary"),
        name="odd_out",
    )(h, mod, o, w_out)


def _route(m, router):
    m_hi, m_lo = _split_bf16(m)
    r_hi, r_lo = _split_bf16(router)
    dot = functools.partial(jnp.dot, preferred_element_type=F32)
    logits = dot(m_hi, r_hi) + (dot(m_hi, r_lo) + dot(m_lo, r_hi))
    lane = lax.broadcasted_iota(jnp.int32, logits.shape, 1)
    neg = -jnp.inf
    l1 = jnp.where(lane < N_EXPERTS, logits, neg)
    m1 = jnp.max(l1, axis=-1, keepdims=True)
    i1 = jnp.min(jnp.where(l1 == m1, lane, LANE), axis=-1, keepdims=True)
    l2 = jnp.where(lane == i1, neg, l1)
    m2 = jnp.max(l2, axis=-1, keepdims=True)
    i2 = jnp.min(jnp.where(l2 == m2, lane, LANE), axis=-1, keepdims=True)
    e2 = jnp.exp(m2 - m1)
    w1 = 1.0 / (1.0 + e2)
    w2 = e2 / (1.0 + e2)
    return i1, i2, w1, w2


R_E0, R_E1, R_RANK0, R_RANK1, R_W0, R_W1 = range(6)
R_ROWS = 8
F_VALID, F_FIRST, F_LAST = 1, 2, 4


def _moe_route_kernel(h_ref, mod_ref, ng_ref, rt_ref, m_ref, rcol_ref, rrow_ref, cnt_ref, carry_s):
    @pl.when(pl.program_id(0) == 0)
    def _():
        carry_s[...] = jnp.zeros_like(carry_s)

    m = _rms(h_ref[...], ng_ref[...]) * (1.0 + mod_ref[4]) + mod_ref[3]
    m_ref[...] = m.astype(m_ref.dtype)
    i1, i2, w1, w2 = _route(m, rt_ref[...])
    tm = m.shape[0]
    lane = lax.broadcasted_iota(jnp.int32, (tm, LANE), 1)
    picked = jnp.where(lane == i1, 1.0, jnp.where(lane == i2, 1.0, 0.0))
    ri = lax.broadcasted_iota(jnp.int32, (tm, tm), 0)
    ci = lax.broadcasted_iota(jnp.int32, (tm, tm), 1)
    earlier = jnp.where(ci < ri, 1.0, 0.0).astype(BF16)
    before = carry_s[0:1, :]
    rank_all = jnp.dot(earlier, picked.astype(BF16), preferred_element_type=F32) + before
    rank0 = jnp.sum(jnp.where(lane == i1, rank_all, 0.0), axis=-1, keepdims=True)
    rank1 = jnp.sum(jnp.where(lane == i2, rank_all, 0.0), axis=-1, keepdims=True)
    after = before + jnp.sum(picked, axis=0, keepdims=True)
    carry_s[...] = jnp.broadcast_to(after, carry_s.shape)
    sub = lax.broadcasted_iota(jnp.int32, cnt_ref.shape, 0)
    cnt_ref[...] = jnp.where(sub == 0, before, jnp.where(sub == 1, after, 0.0))

    fields = {R_E0: i1.astype(F32), R_E1: i2.astype(F32), R_RANK0: rank0, R_RANK1: rank1, R_W0: w1, R_W1: w2}
    rcol = jnp.zeros((tm, LANE), F32)
    for k, val in fields.items():
        rcol = jnp.where(lane == k, val, rcol)
    rcol_ref[...] = rcol
    rrow_ref[...] = jnp.transpose(rcol)[:R_ROWS, :]


def _moe_route(h, mod, ng, router, tile0, n_chunks):
    tm = MOE_TC
    rows = n_chunks * tm
    return pl.pallas_call(
        _moe_route_kernel,
        out_shape=(jax.ShapeDtypeStruct((rows, D_MODEL), BF16),
                   jax.ShapeDtypeStruct((rows, LANE), F32),
                   jax.ShapeDtypeStruct((R_ROWS, rows), F32),
                   jax.ShapeDtypeStruct((n_chunks, R_ROWS, LANE), F32)),
        grid=(n_chunks,),
        in_specs=[_rows(tm, D_MODEL, tile0), _mod_spec(tm, tile0), _const(ng.shape), _const(router.shape)],
        out_specs=(_rows(tm, D_MODEL), _rows(tm, LANE),
                   pl.BlockSpec((R_ROWS, tm), lambda i: (0, i)),
                   pl.BlockSpec((None, R_ROWS, LANE), lambda i: (i, 0, 0))),
        scratch_shapes=[pltpu.VMEM((R_ROWS, LANE), F32)],
        compiler_params=_params("arbitrary"),
        name="moe_route",
    )(h, mod, ng, router)


def _moe_plan(cnt, n_chunks, n_blk):
    i32 = jnp.int32
    before = cnt[:, 0, :N_EXPERTS].astype(i32)
    after = cnt[:, 1, :N_EXPERTS].astype(i32)
    padded = (after[-1] + MOE_XT - 1) // MOE_XT * MOE_XT
    ends = jnp.cumsum(padded).astype(i32)
    base = ends - padded
    used_rows = ends[-1]
    start = base[None, :] + before
    stop = base[None, :] + after
    lo = jnp.arange(n_blk, dtype=i32) * MOE_BLK
    overlap = (start[:, :, None] < lo + MOE_BLK) & (stop[:, :, None] > lo)
    inc = jnp.any(overlap, axis=1)
    fill = (lo < used_rows) & ~jnp.any(inc, axis=0)
    inc_fill = inc.at[0].set(inc[0] | fill)
    n_items = n_chunks * N_EXPERTS + n_blk

    def items(mat):
        idx = jnp.nonzero(mat.reshape(-1), size=n_items, fill_value=-1)[0].astype(i32)
        valid = idx >= 0
        n_valid = jnp.sum(valid).astype(i32)
        idx = jnp.where(valid, idx, idx[jnp.maximum(n_valid - 1, 0)])
        outer = idx // mat.shape[1]
        inner = idx % mat.shape[1]
        prev = jnp.concatenate([jnp.full((1,), -1, i32), outer[:-1]])
        nxt = jnp.concatenate([outer[1:], jnp.full((1,), -1, i32)])
        pos = jnp.arange(n_items, dtype=i32)
        first = valid & (outer != prev)
        last = valid & ((outer != nxt) | (pos == n_valid - 1))
        flags = valid.astype(i32) * F_VALID + first.astype(i32) * F_FIRST + last.astype(i32) * F_LAST
        return outer, inner, flags

    disp_blk, disp_chunk, disp_flags = items(inc_fill.T)
    comb_chunk, comb_blk, comb_flags = items(inc)
    n_xt = n_blk * MOE_BLK // MOE_XT
    t0 = jnp.arange(n_xt, dtype=i32) * MOE_XT
    tile_used = (t0 < used_rows).astype(i32)
    tile_src = jnp.minimum(jnp.arange(n_xt, dtype=i32), used_rows // MOE_XT - 1)
    tile_e = jnp.minimum(jnp.sum(t0[:, None] >= ends[None, :], axis=1), N_EXPERTS - 1).astype(i32)[tile_src]
    return {"base": base, "disp": (disp_chunk, disp_blk, disp_flags), "comb": (comb_chunk, comb_blk, comb_flags),
            "tiles": (tile_e, tile_src, tile_used)}


def _sorted_row(expert, rank, base_ref):
    start = jnp.zeros_like(expert)
    for e in range(N_EXPERTS):
        start = jnp.where(expert == float(e), base_ref[e].astype(F32), start)
    return start + rank


def _moe_dispatch_kernel(c_ref, d_ref, f_ref, base_ref, m_ref, rrow_ref, x_ref, ws_ref):
    n = pl.program_id(0)
    flag = f_ref[n]

    @pl.when((flag & F_VALID) != 0)
    def _():
        rr = rrow_ref[...]
        row0 = (d_ref[n] * MOE_BLK).astype(F32)
        rel0 = _sorted_row(rr[R_E0:R_E0 + 1], rr[R_RANK0:R_RANK0 + 1], base_ref) - row0
        rel1 = _sorted_row(rr[R_E1:R_E1 + 1], rr[R_RANK1:R_RANK1 + 1], base_ref) - row0
        ri = lax.broadcasted_iota(jnp.int32, (MOE_BLK, MOE_TC), 0).astype(F32)
        hit0 = rel0 == ri
        hit1 = rel1 == ri
        onehot = jnp.where(hit0, 1.0, jnp.where(hit1, 1.0, 0.0)).astype(BF16)
        rows = jnp.dot(onehot, m_ref[...], preferred_element_type=F32).astype(x_ref.dtype)
        w = jnp.sum(jnp.where(hit0, rr[R_W0:R_W0 + 1], jnp.where(hit1, rr[R_W1:R_W1 + 1], 0.0)),
                    axis=-1, keepdims=True)
        w = jnp.broadcast_to(w, ws_ref.shape)

        @pl.when((flag & F_FIRST) != 0)
        def _():
            x_ref[...] = rows
            ws_ref[...] = w

        @pl.when((flag & F_FIRST) == 0)
        def _():
            x_ref[...] = x_ref[...] + rows
            ws_ref[...] = ws_ref[...] + w


def _moe_dispatch(m, rrow, plan, n_blk):
    chunk, blk, flags = plan["disp"]
    n_items = chunk.shape[0]
    grid_spec = pltpu.PrefetchScalarGridSpec(
        num_scalar_prefetch=4,
        grid=(n_items,),
        in_specs=[pl.BlockSpec((MOE_TC, D_MODEL), lambda n, c, d, f, b: (c[n], 0)),
                  pl.BlockSpec((R_ROWS, MOE_TC), lambda n, c, d, f, b: (0, c[n]))],
        out_specs=(pl.BlockSpec((MOE_BLK, D_MODEL), lambda n, c, d, f, b: (d[n], 0)),
                   pl.BlockSpec((MOE_BLK, LANE), lambda n, c, d, f, b: (d[n], 0))),
    )
    return pl.pallas_call(
        _moe_dispatch_kernel,
        out_shape=(jax.ShapeDtypeStruct((n_blk * MOE_BLK, D_MODEL), BF16),
                   jax.ShapeDtypeStruct((n_blk * MOE_BLK, LANE), F32)),
        grid_spec=grid_spec,
        compiler_params=_params("arbitrary"),
        name="moe_dispatch",
    )(chunk, blk, flags, plan["base"], m, rrow)


def _moe_experts_kernel(e_ref, src_ref, used_ref, x_ref, ws_ref, wg_ref, wu_ref, wd_ref, y_ref, acc_s):
    t = pl.program_id(0)
    j = pl.program_id(1)
    last = j == pl.num_programs(1) - 1
    used = used_ref[t] != 0

    @pl.when(used)
    def _():
        x = x_ref[...]
        gate = jnp.dot(x, wg_ref[...], preferred_element_type=F32)
        up = jnp.dot(x, wu_ref[...], preferred_element_type=F32)
        act = (_silu(gate) * up * ws_ref[:, 0:1]).astype(BF16)
        part = jnp.dot(act, wd_ref[...], preferred_element_type=F32)

        @pl.when(j == 0)
        def _():
            acc_s[...] = part

        @pl.when(j != 0)
        def _():
            acc_s[...] += part

        @pl.when(last)
        def _():
            y_ref[...] = acc_s[...].astype(y_ref.dtype)

    @pl.when(jnp.logical_not(used) & last)
    def _():
        y_ref[...] = jnp.zeros_like(y_ref)


def _moe_experts(x, ws, plan, wts):
    tile_e, tile_src, tile_used = plan["tiles"]
    n_xt = tile_e.shape[0]
    th = MOE_TH
    nj = MOE_HIDDEN // th

    def hidden(t, j, used):
        return jnp.where(used[t] != 0, j, nj - 1)

    grid_spec = pltpu.PrefetchScalarGridSpec(
        num_scalar_prefetch=3,
        grid=(n_xt, nj),
        in_specs=[pl.BlockSpec((MOE_XT, D_MODEL), lambda t, j, e, s, u: (s[t], 0)),
                  pl.BlockSpec((MOE_XT, LANE), lambda t, j, e, s, u: (s[t], 0)),
                  pl.BlockSpec((None, D_MODEL, th), lambda t, j, e, s, u: (e[t], 0, hidden(t, j, u))),
                  pl.BlockSpec((None, D_MODEL, th), lambda t, j, e, s, u: (e[t], 0, hidden(t, j, u))),
                  pl.BlockSpec((None, th, D_MODEL), lambda t, j, e, s, u: (e[t], hidden(t, j, u), 0))],
        out_specs=pl.BlockSpec((MOE_XT, D_MODEL), lambda t, j, e, s, u: (t, 0)),
        scratch_shapes=[pltpu.VMEM((MOE_XT, D_MODEL), F32)],
    )
    return pl.pallas_call(
        _moe_experts_kernel,
        out_shape=jax.ShapeDtypeStruct((n_xt * MOE_XT, D_MODEL), BF16),
        grid_spec=grid_spec,
        compiler_params=_params("arbitrary", "arbitrary"),
        name="moe_experts",
    )(tile_e, tile_src, tile_used, x, ws, wts["moe_g"], wts["moe_u"], wts["moe_d"])


def _moe_combine_kernel(c_ref, d_ref, f_ref, base_ref, rcol_ref, y_ref, h_ref, mod_ref, out_ref, acc_s):
    n = pl.program_id(0)
    flag = f_ref[n]

    @pl.when((flag & F_VALID) != 0)
    def _():
        rc = rcol_ref[...]
        row0 = (d_ref[n] * MOE_BLK).astype(F32)
        rel0 = _sorted_row(rc[:, R_E0:R_E0 + 1], rc[:, R_RANK0:R_RANK0 + 1], base_ref) - row0
        rel1 = _sorted_row(rc[:, R_E1:R_E1 + 1], rc[:, R_RANK1:R_RANK1 + 1], base_ref) - row0
        li = lax.broadcasted_iota(jnp.int32, (MOE_TC, MOE_BLK), 1).astype(F32)
        onehot = jnp.where(rel0 == li, 1.0, jnp.where(rel1 == li, 1.0, 0.0)).astype(BF16)
        part = jnp.dot(onehot, y_ref[...], preferred_element_type=F32)

        @pl.when((flag & F_FIRST) != 0)
        def _():
            acc_s[...] = part

        @pl.when((flag & F_FIRST) == 0)
        def _():
            acc_s[...] += part

        @pl.when((flag & F_LAST) != 0)
        def _():
            out_ref[...] = h_ref[...] + mod_ref[5] * acc_s[...]


def _moe_combine(h, mod, rcol, y, plan, tile0):
    chunk, blk, flags = plan["comb"]
    n_items = chunk.shape[0]
    grid_spec = pltpu.PrefetchScalarGridSpec(
        num_scalar_prefetch=4,
        grid=(n_items,),
        in_specs=[pl.BlockSpec((MOE_TC, LANE), lambda n, c, d, f, b: (c[n], 0)),
                  pl.BlockSpec((MOE_BLK, D_MODEL), lambda n, c, d, f, b: (d[n], 0)),
                  pl.BlockSpec((MOE_TC, D_MODEL), lambda n, c, d, f, b: (c[n] + tile0, 0)),
                  pl.BlockSpec((N_MOD, None, 1, D_MODEL),
                               lambda n, c, d, f, b: (0, _mod_row(c[n] + tile0, MOE_TC), 0, 0))],
        out_specs=pl.BlockSpec((MOE_TC, D_MODEL), lambda n, c, d, f, b: (c[n] + tile0, 0)),
        scratch_shapes=[pltpu.VMEM((MOE_TC, D_MODEL), F32)],
    )
    return pl.pallas_call(
        _moe_combine_kernel,
        out_shape=jax.ShapeDtypeStruct((N_ROWS, D_MODEL), F32),
        grid_spec=grid_spec,
        input_output_aliases={6: 0},
        compiler_params=_params("arbitrary"),
        name="moe_combine",
    )(chunk, blk, flags, plan["base"], rcol, y, h, mod)


def _moe(h, mod, ng, wts, first_row):
    tile0 = first_row // MOE_TC
    n_chunks = (N_ROWS - first_row) // MOE_TC
    n_blk = (2 * n_chunks * MOE_TC + N_EXPERTS * MOE_XT) // MOE_BLK
    m, rcol, rrow, cnt = _moe_route(h, mod, ng, wts["router"], tile0, n_chunks)
    plan = _moe_plan(cnt, n_chunks, n_blk)
    x, ws = _moe_dispatch(m, rrow, plan, n_blk)
    y = _moe_experts(x, ws, plan, wts)
    return _moe_combine(h, mod, rcol, y, plan, tile0)


def _final_kernel(h_ref, g_ref, o_ref):
    o_ref[...] = _rms(h_ref[...], g_ref[...])


def _final_norm(h, g):
    tm = TM
    return pl.pallas_call(
        _final_kernel,
        out_shape=jax.ShapeDtypeStruct((NL_ROWS, D_MODEL), F32),
        grid=(NL_ROWS // tm,),
        in_specs=[_rows(tm, D_MODEL, NC_ROWS // tm), _const(g.shape)],
        out_specs=_rows(tm, D_MODEL),
        compiler_params=_params("arbitrary"),
        name="final_norm",
    )(h, g)


def _row_vec(x):
    return x.reshape(1, -1).astype(F32)


def _even_weights(i, ev_w_in, a_ln_g, a_ln_b, a_ws, a_bs, b_gate_w, b_gate_b, b_norm_g, ev_w_out,
                  ff_w_gate, ff_w_up, ff_w_down):
    w_in = ev_w_in[i]
    qkvr = 2 * B_QK + 2 * B_V
    gate_w = jnp.zeros((2 * B_GATE_RANK, 2 * B_QK), F32)
    gate_w = gate_w.at[:B_GATE_RANK, :B_QK].set(b_gate_w[i, 0]).at[B_GATE_RANK:, B_QK:].set(b_gate_w[i, 1])
    nch = FF_HIDDEN // FF_TH
    return {
        "w_a": w_in[:, :2 * A_WIDTH].astype(BF16),
        "w_q": w_in[:, 2 * A_WIDTH:2 * A_WIDTH + qkvr].astype(BF16),
        "w_gf": w_in[:, 2 * A_WIDTH + qkvr:].astype(BF16),
        "gate_w": gate_w.astype(BF16),
        "gate_b": b_gate_b[i].reshape(1, 2 * B_QK).astype(F32),
        "ln_g": _row_vec(a_ln_g[i]),
        "ln_b": _row_vec(a_ln_b[i]),
        "ws": a_ws[i].astype(BF16),
        "bs": jnp.broadcast_to(a_bs[i][:, :, None], (A_HEADS, A_CHUNK, A_HEAD_DIM)).astype(F32),
        "b_norm_g": _row_vec(b_norm_g[i]),
        "w_out_a": ev_w_out[i, :A_WIDTH].astype(BF16),
        "w_out_b": ev_w_out[i, A_WIDTH:].astype(BF16),
        "ff_g": jnp.transpose(ff_w_gate[i].reshape(D_MODEL, nch, FF_TH), (1, 0, 2)).astype(BF16),
        "ff_u": jnp.transpose(ff_w_up[i].reshape(D_MODEL, nch, FF_TH), (1, 0, 2)).astype(BF16),
        "ff_d": ff_w_down[i].reshape(nch, FF_TH, D_MODEL).astype(BF16),
    }


def _rope_swap_index():
    quarter = C_ROPE // 4
    idx = np.arange(C_ROPE)
    return np.concatenate([idx[quarter:2 * quarter], idx[:quarter], idx[3 * quarter:], idx[2 * quarter:3 * quarter]])


def _odd_weights(i, od_w_in, c_q_norm_g, c_w_uq, c_kv_norm_g, c_w_ukv, od_w_out,
                 moe_router, moe_w_gate, moe_w_up, moe_w_down):
    swap = _rope_swap_index()
    w_in = od_w_in[i]
    low = C_Q_RANK + C_KV_RANK
    k_rope_w = w_in[:, low:]
    pad = jnp.zeros((D_MODEL, LANE - C_ROPE), F32)
    w_in_x = jnp.concatenate([w_in[:, :low], k_rope_w, pad, k_rope_w[:, swap], pad], axis=1)

    uq = c_w_uq[i].reshape(C_Q_RANK, C_HEADS, C_NOPE + C_ROPE)
    zq = jnp.zeros((C_Q_RANK, C_HEADS, HEAD_PAD - C_NOPE - C_ROPE), F32)
    w_q = jnp.concatenate([uq, zq], axis=-1)
    w_q_swap = jnp.concatenate([jnp.zeros((C_Q_RANK, C_HEADS, C_NOPE), F32), uq[:, :, C_NOPE:][:, :, swap], zq], axis=-1)
    ukv = c_w_ukv[i].reshape(C_KV_RANK, C_HEADS, C_NOPE + C_VDIM)
    w_k = jnp.concatenate([ukv[:, :, :C_NOPE], jnp.zeros((C_KV_RANK, C_HEADS, HEAD_PAD - C_NOPE), F32)], axis=-1)
    zv = jnp.zeros((C_KV_RANK, C_HEADS // 2, C_VDIM), F32)
    uv = ukv[:, :, C_NOPE:].reshape(C_KV_RANK, C_HEADS // 2, 2, C_VDIM)
    w_v = jnp.stack([jnp.concatenate([uv[:, :, 0], zv], axis=-1),
                     jnp.concatenate([zv, uv[:, :, 1]], axis=-1)], axis=2)
    place = np.zeros((LANE, C_HEADS, HEAD_PAD), np.float32)
    for r in range(C_ROPE):
        place[r, :, C_NOPE + r] = 1.0
    router = jnp.concatenate([moe_router[i], jnp.zeros((D_MODEL, LANE - N_EXPERTS), F32)], axis=1)
    return {
        "w_in": w_in_x.astype(BF16),
        "q_norm_g": _row_vec(c_q_norm_g[i]),
        "kv_norm_g": _row_vec(c_kv_norm_g[i]),
        "w_q": w_q.reshape(C_Q_RANK, C_QK_PAD).astype(BF16),
        "w_q_swap": w_q_swap.reshape(C_Q_RANK, C_QK_PAD).astype(BF16),
        "w_k": w_k.reshape(C_KV_RANK, C_QK_PAD).astype(BF16),
        "w_v": w_v.reshape(C_KV_RANK, C_QK_PAD).astype(BF16),
        "place": jnp.asarray(place.reshape(LANE, C_QK_PAD), BF16),
        "w_out": od_w_out[i].astype(BF16),
        "router": router,
        "moe_g": moe_w_gate[i].astype(BF16),
        "moe_u": moe_w_up[i].astype(BF16),
        "moe_d": moe_w_down[i].astype(BF16),
    }


def _rope_tables(tm):
    quarter = C_ROPE // 4
    half = C_ROPE // 2
    inv_freq = ROPE_BASE ** (-jnp.arange(0, half, 2, dtype=F32) / half)
    pos = jnp.arange(SEQ, dtype=jnp.int32)
    ang_r = (pos // GRID_W).astype(F32)[:, None] * inv_freq
    ang_c = (pos % GRID_W).astype(F32)[:, None] * inv_freq
    cos = jnp.concatenate([jnp.cos(ang_r)] * 2 + [jnp.cos(ang_c)] * 2, axis=1)
    sin = jnp.concatenate([-jnp.sin(ang_r), jnp.sin(ang_r), -jnp.sin(ang_c), jnp.sin(ang_c)], axis=1)
    cos = jnp.concatenate([cos, jnp.ones((tm, C_ROPE), F32)], axis=0)
    sin = jnp.concatenate([sin, jnp.zeros((tm, C_ROPE), F32)], axis=0)
    rows = SEQ + tm
    ones = jnp.ones((rows, C_NOPE), F32)
    z = lambda w: jnp.zeros((rows, w), F32)
    cq = jnp.concatenate([ones, cos, z(HEAD_PAD - C_NOPE - C_ROPE)], axis=1)
    sq = jnp.concatenate([z(C_NOPE), sin, z(HEAD_PAD - C_NOPE - C_ROPE)], axis=1)
    ck = jnp.concatenate([cos, z(LANE - C_ROPE)], axis=1)
    sk = jnp.concatenate([sin, z(LANE - C_ROPE)], axis=1)
    return cq, sq, ck, sk


def kernel(x, c, ctx, c_ctx, mod_w, mod_b, norm_mix_g, norm_ffn_g, final_g, ev_w_in, a_ln_g, a_ln_b, a_ws, a_bs, b_gate_w, b_gate_b, b_norm_g, ev_w_out, od_w_in, c_q_norm_g, c_w_uq, c_kv_norm_g, c_w_ukv, od_w_out, ff_w_gate, ff_w_up, ff_w_down, moe_router, moe_w_gate, moe_w_up, moe_w_down):
    h = jnp.concatenate([ctx.reshape(NC_ROWS, D_MODEL), x.reshape(NL_ROWS, D_MODEL)], axis=0)
    cond = jnp.concatenate([c, c_ctx[None, :], jnp.zeros((MOD_PAD - MOD_ROWS, D_MODEL), F32)], axis=0)
    mod = _modulation(cond, mod_w, mod_b)
    rope = _rope_tables(TM)

    for layer in range(DEPTH):
        i = layer // 2
        need_ctx = layer < DEPTH - 1
        ng_mix = _row_vec(norm_mix_g[layer])
        ng_ffn = _row_vec(norm_ffn_g[layer])
        if layer % 2 == 0:
            wts = _even_weights(i, ev_w_in, a_ln_g, a_ln_b, a_ws, a_bs, b_gate_w, b_gate_b, b_norm_g,
                                ev_w_out, ff_w_gate, ff_w_up, ff_w_down)
            ya, qkv, r, g = _even_in(h, mod[layer], ng_mix, wts)
            o_f, o_b = _gla(qkv, g)
            h = _even_out(h, mod[layer], o_f, o_b, r, ya, wts)
            h = _ffn(h, mod[layer], ng_ffn, wts)
        else:
            wts = _odd_weights(i, od_w_in, c_q_norm_g, c_w_uq, c_kv_norm_g, c_w_ukv, od_w_out,
                               moe_router, moe_w_gate, moe_w_up, moe_w_down)
            q, k, v = _odd_in(h, mod[layer], ng_mix, wts, rope)
            o = _attention(q, k, v, latent=True)
            if need_ctx:
                o = _attention(q, k, v, latent=False, prev=o)
            first = 0 if need_ctx else NC_ROWS
            h = _odd_out(h, mod[layer], o, wts["w_out"], first // TM, (N_ROWS - first) // TM)
            h = _moe(h, mod[layer], ng_ffn, wts, first)
    out = _final_norm(h, _row_vec(final_g))
    return out.reshape(BATCH, SEQ, D_MODEL)
```

```python
import functools

import numpy as np
import jax
import jax.numpy as jnp
from jax import lax
from jax.experimental import pallas as pl
from jax.experimental.pallas import tpu as pltpu

D_MODEL = 1024
BATCH = 8
SEQ = 2048
DEPTH = 4
GRID_W = 64
CTX_LEN = 256
EPS = 1e-6
N_MOD = 6

A_HEADS = 4
A_HEAD_DIM = 128
A_CHUNK = 128
A_WIDTH = A_HEADS * A_HEAD_DIM

B_HEADS = 4
B_DK = 64
B_DV = 128
B_QK = B_HEADS * B_DK
B_V = B_HEADS * B_DV
B_GATE_RANK = 16
B_GATE_TEMP = 16.0
B_CHUNK = 64

C_HEADS = 16
C_NOPE = 64
C_ROPE = 32
C_VDIM = 64
C_Q_RANK = 256
C_KV_RANK = 256
C_SCALE = (C_NOPE + C_ROPE) ** -0.5
LOG2_E = 1.4426950408889634
ROPE_BASE = 10000.0

FF_HIDDEN = 2816
N_EXPERTS = 8
MOE_HIDDEN = 3584

NC_ROWS = BATCH * CTX_LEN
NL_ROWS = BATCH * SEQ
N_ROWS = NC_ROWS + NL_ROWS
MOD_ROWS = BATCH + 1
MOD_PAD = 16

LANE = 128
HEAD_PAD = 128
C_QK_PAD = C_HEADS * HEAD_PAD

TM = 512
GLA_BLK = 256
ATT_TQ = 512
ATT_HG = 4
FF_TH = 256
MOE_TC = 512
MOE_BLK = 256
MOE_XT = 1024
MOE_TH = 512
VMEM_LIMIT = 56 * 1024 * 1024

F32 = jnp.float32
BF16 = jnp.bfloat16


def _mm(a, b):
    return jnp.dot(a.astype(BF16), b.astype(BF16), preferred_element_type=F32)


def _mm_nt(a, b):
    return lax.dot_general(a.astype(BF16), b.astype(BF16), (((1,), (1,)), ((), ())),
                           preferred_element_type=F32)


def _mm_tn(a, b):
    return lax.dot_general(a.astype(BF16), b.astype(BF16), (((0,), (0,)), ((), ())),
                           preferred_element_type=F32)


def _sigmoid(x):
    return 1.0 / (1.0 + jnp.exp(-x))


def _silu(x):
    return x * _sigmoid(x)


def _gelu(x):
    return 0.5 * x * (1.0 + jnp.tanh(np.sqrt(2.0 / np.pi) * (x + 0.044715 * (x * x * x))))


def _rms(x, g):
    return x * lax.rsqrt(jnp.mean(x * x, axis=-1, keepdims=True) + EPS) * g


def _split_bf16(x):
    hi = x.astype(BF16)
    lo = (x - hi.astype(F32)).astype(BF16)
    return hi, lo


def _rows(tm, width, tile0=0, col=0):
    return pl.BlockSpec((tm, width), lambda i: (i + tile0, col))


def _const(shape):
    zeros = (0,) * len(shape)
    return pl.BlockSpec(shape, lambda *_: zeros, pipeline_mode=pl.Buffered(1))


def _mod_row(tile, tm):
    return jnp.where(tile < NC_ROWS // tm, BATCH, (tile * tm - NC_ROWS) // SEQ)


def _mod_spec(tm, tile0=0):
    return pl.BlockSpec((N_MOD, None, 1, D_MODEL), lambda i: (0, _mod_row(i + tile0, tm), 0, 0))


def _params(*sem):
    return pltpu.CompilerParams(dimension_semantics=sem, vmem_limit_bytes=VMEM_LIMIT)


def _mod_kernel(c_ref, w_ref, b_ref, o_ref):
    o_ref[...] = _mm(_silu(c_ref[...]), w_ref[...]) + b_ref[...]


def _modulation(cond, mod_w, mod_b):
    tn = 1536
    out = pl.pallas_call(
        _mod_kernel,
        out_shape=jax.ShapeDtypeStruct((DEPTH, MOD_PAD, N_MOD * D_MODEL), F32),
        grid=(DEPTH, N_MOD * D_MODEL // tn),
        in_specs=[
            pl.BlockSpec((MOD_PAD, D_MODEL), lambda l, j: (0, 0)),
            pl.BlockSpec((None, D_MODEL, tn), lambda l, j: (l, 0, j)),
            pl.BlockSpec((None, 1, tn), lambda l, j: (l, 0, j)),
        ],
        out_specs=pl.BlockSpec((None, MOD_PAD, tn), lambda l, j: (l, 0, j)),
        compiler_params=_params("arbitrary", "arbitrary"),
        name="modulation",
    )(cond, mod_w, mod_b.reshape(DEPTH, 1, N_MOD * D_MODEL))
    out = out[:, :MOD_ROWS].reshape(DEPTH, MOD_ROWS, N_MOD, 1, D_MODEL)
    return jnp.transpose(out, (0, 2, 1, 3, 4))


def _even_in_kernel(h_ref, mod_ref, ng_ref, wa_ref, wq_ref, wgf_ref, gw_ref, gb_ref,
                    lng_ref, lnb_ref, ws_ref, bs_ref, ya_ref, qkv_ref, r_ref, g_ref):
    tm = h_ref.shape[0]
    n = _rms(h_ref[...], ng_ref[...]) * (1.0 + mod_ref[1]) + mod_ref[0]
    nb = n.astype(BF16)

    ua_va = jnp.dot(nb, wa_ref[...], preferred_element_type=F32)
    u = _gelu(ua_va[:, :A_WIDTH])
    v = _gelu(ua_va[:, A_WIDTH:])
    mu = jnp.mean(v, axis=-1, keepdims=True)
    vc = v - mu
    var = jnp.mean(vc * vc, axis=-1, keepdims=True)
    v = (vc * lax.rsqrt(var + EPS) * lng_ref[...] + lnb_ref[...]).astype(BF16)
    for c in range(tm // A_CHUNK):
        rows = slice(c * A_CHUNK, (c + 1) * A_CHUNK)
        for h in range(A_HEADS):
            cols = slice(h * A_HEAD_DIM, (h + 1) * A_HEAD_DIM)
            mixed = jnp.dot(ws_ref[h], v[rows, cols], preferred_element_type=F32) + bs_ref[h]
            ya_ref[rows, cols] = (u[rows, cols] * mixed).astype(ya_ref.dtype)

    p = jnp.dot(nb, wq_ref[...], preferred_element_type=F32)
    qkv_ref[...] = p[:, :2 * B_QK + B_V].astype(qkv_ref.dtype)
    r_ref[...] = p[:, 2 * B_QK + B_V:].astype(r_ref.dtype)
    gf = jnp.dot(nb, wgf_ref[...], preferred_element_type=F32)
    logit = _mm(gf, gw_ref[...]) + gb_ref[...]
    log_sig = jnp.minimum(logit, 0.0) - jnp.log(1.0 + jnp.exp(-jnp.abs(logit)))
    g_ref[...] = log_sig / B_GATE_TEMP


def _even_in(h, mod, ng, wts):
    tm = TM
    n_tiles = N_ROWS // tm
    qkv_w = 2 * B_QK + B_V
    consts = [ng, wts["w_a"], wts["w_q"], wts["w_gf"], wts["gate_w"], wts["gate_b"],
              wts["ln_g"], wts["ln_b"], wts["ws"], wts["bs"]]
    return pl.pallas_call(
        _even_in_kernel,
        out_shape=(jax.ShapeDtypeStruct((N_ROWS, A_WIDTH), BF16),
                   jax.ShapeDtypeStruct((N_ROWS, qkv_w), BF16),
                   jax.ShapeDtypeStruct((N_ROWS, B_V), BF16),
                   jax.ShapeDtypeStruct((N_ROWS, 2 * B_QK), F32)),
        grid=(n_tiles,),
        in_specs=[_rows(tm, D_MODEL), _mod_spec(tm)] + [_const(a.shape) for a in consts],
        out_specs=(_rows(tm, A_WIDTH), _rows(tm, qkv_w), _rows(tm, B_V), _rows(tm, 2 * B_QK)),
        compiler_params=_params("arbitrary"),
        name="even_in",
    )(h, mod, *consts)


def _gla_direction(q, k, v, g, st_ref, reverse):
    n = GLA_BLK
    nch = n // B_CHUNK
    ri = lax.broadcasted_iota(jnp.int32, (n, n), 0)
    ci = lax.broadcasted_iota(jnp.int32, (n, n), 1)
    same = (ri // B_CHUNK) == (ci // B_CHUNK)
    tri = same & ((ci >= ri) if reverse else (ci <= ri))
    tri_b = jnp.where(tri, 1.0, 0.0).astype(BF16)
    same_b = jnp.where(same, 1.0, 0.0).astype(BF16)

    g_hi, g_lo = _split_bf16(g)
    dot = functools.partial(jnp.dot, preferred_element_type=F32)
    b = dot(tri_b, g_hi) + dot(tri_b, g_lo)
    b_tot = dot(same_b, g_hi) + dot(same_b, g_lo)
    q_dec = (q * (B_DK ** -0.5)) * jnp.exp(b)
    k_inv = k * jnp.exp(-b)
    k_dec = k * jnp.exp(b_tot - b)
    decay = jnp.exp(b_tot)

    lane = lax.broadcasted_iota(jnp.int32, (1, LANE), 1)
    even_lane = lane < B_DK

    outs = [None] * B_HEADS
    order = list(range(nch))[::-1] if reverse else list(range(nch))
    for p in range(B_HEADS // 2):
        lanes = slice(p * LANE, (p + 1) * LANE)
        qp = q_dec[:, lanes]
        q_heads = (jnp.where(even_lane, qp, 0.0).astype(BF16), jnp.where(even_lane, 0.0, qp).astype(BF16))
        k_inv_p = k_inv[:, lanes].astype(BF16)
        k_dec_p = k_dec[:, lanes].astype(BF16)
        intra = []
        for hh in range(2):
            h = 2 * p + hh
            scores = _mm_nt(q_heads[hh], k_inv_p)
            scores = jnp.where(tri, scores, 0.0)
            intra.append(_mm(scores, v[:, h * B_DV:(h + 1) * B_DV]))
        st = st_ref[p]
        inter = [[None] * nch for _ in range(2)]
        for c in order:
            rows = slice(c * B_CHUNK, (c + 1) * B_CHUNK)
            st_b = st.astype(BF16)
            for hh in range(2):
                inter[hh][c] = _mm_nt(q_heads[hh][rows], st_b)
            upd_e = _mm_tn(v[rows, (2 * p) * B_DV:(2 * p + 1) * B_DV], k_dec_p[rows])
            upd_o = _mm_tn(v[rows, (2 * p + 1) * B_DV:(2 * p + 2) * B_DV], k_dec_p[rows])
            st = st * decay[c * B_CHUNK:c * B_CHUNK + 1, lanes] + jnp.where(even_lane, upd_e, upd_o)
        st_ref[p] = st
        for hh in range(2):
            outs[2 * p + hh] = intra[hh] + jnp.concatenate(inter[hh], axis=0)
    return outs


def _gla_kernel(qf_ref, kf_ref, vf_ref, gf_ref, qb_ref, kb_ref, vb_ref, gb_ref,
                of_ref, ob_ref, stf_ref, stb_ref):
    @pl.when(pl.program_id(1) == 0)
    def _():
        stf_ref[...] = jnp.zeros_like(stf_ref)
        stb_ref[...] = jnp.zeros_like(stb_ref)

    outs = _gla_direction(qf_ref[...].astype(F32), kf_ref[...].astype(F32), vf_ref[...], gf_ref[...], stf_ref, False)
    for h in range(B_HEADS):
        of_ref[:, h * B_DV:(h + 1) * B_DV] = outs[h]
    outs = _gla_direction(qb_ref[...].astype(F32), kb_ref[...].astype(F32), vb_ref[...], gb_ref[...], stb_ref, True)
    for h in range(B_HEADS):
        ob_ref[:, h * B_DV:(h + 1) * B_DV] = outs[h]


def _gla(qkv, g):
    nb_ctx = CTX_LEN // GLA_BLK
    nb_lat = SEQ // GLA_BLK
    nblk = nb_ctx + nb_lat
    ctx_blocks = NC_ROWS // GLA_BLK

    def fwd_row(b, j):
        return jnp.where(j < nb_ctx, b * nb_ctx + j, ctx_blocks + b * nb_lat + (j - nb_ctx))

    def bwd_row(b, j):
        return jnp.where(j < nb_ctx, b * nb_ctx + (nb_ctx - 1 - j),
                         ctx_blocks + b * nb_lat + (nblk - 1 - j))

    def spec(width, col, row):
        return pl.BlockSpec((GLA_BLK, width), lambda b, j: (row(b, j), col))

    ins, in_specs = [], []
    for d, row in enumerate((fwd_row, bwd_row)):
        ins += [qkv, qkv, qkv, g]
        in_specs += [spec(B_QK, 0, row), spec(B_QK, 1, row), spec(B_V, 1, row), spec(B_QK, d, row)]
    return pl.pallas_call(
        _gla_kernel,
        out_shape=(jax.ShapeDtypeStruct((N_ROWS, B_V), F32),) * 2,
        grid=(BATCH, nblk),
        in_specs=in_specs,
        out_specs=(spec(B_V, 0, fwd_row), spec(B_V, 0, bwd_row)),
        scratch_shapes=[pltpu.VMEM((B_HEADS // 2, B_DV, LANE), F32)] * 2,
        compiler_params=_params("arbitrary", "arbitrary"),
        name="gla",
    )(*ins)


def _even_out_kernel(h_ref, mod_ref, of_ref, ob_ref, r_ref, ya_ref, bng_ref, woa_ref, wob_ref, out_ref):
    o = of_ref[...] + ob_ref[...]
    parts = [_rms(o[:, h * B_DV:(h + 1) * B_DV], bng_ref[...]) for h in range(B_HEADS)]
    yb = _silu(r_ref[...].astype(F32)) * jnp.concatenate(parts, axis=-1)
    y = jnp.dot(ya_ref[...], woa_ref[...], preferred_element_type=F32) + _mm(yb, wob_ref[...])
    out_ref[...] = h_ref[...] + mod_ref[2] * y


def _even_out(h, mod, o_f, o_b, r, ya, wts):
    tm = TM
    consts = [wts["b_norm_g"], wts["w_out_a"], wts["w_out_b"]]
    return pl.pallas_call(
        _even_out_kernel,
        out_shape=jax.ShapeDtypeStruct((N_ROWS, D_MODEL), F32),
        grid=(N_ROWS // tm,),
        in_specs=[_rows(tm, D_MODEL), _mod_spec(tm), _rows(tm, B_V), _rows(tm, B_V), _rows(tm, B_V),
                  _rows(tm, A_WIDTH)] + [_const(a.shape) for a in consts],
        out_specs=_rows(tm, D_MODEL),
        input_output_aliases={0: 0},
        compiler_params=_params("arbitrary"),
        name="even_out",
    )(h, mod, o_f, o_b, r, ya, *consts)


def _ffn_kernel(h_ref, mod_ref, ng_ref, wg_ref, wu_ref, wd_ref, out_ref):
    x = h_ref[...]
    m = (_rms(x, ng_ref[...]) * (1.0 + mod_ref[4]) + mod_ref[3]).astype(BF16)
    acc = jnp.zeros(x.shape, F32)
    for j in range(wg_ref.shape[0]):
        gate = jnp.dot(m, wg_ref[j], preferred_element_type=F32)
        up = jnp.dot(m, wu_ref[j], preferred_element_type=F32)
        act = (_silu(gate) * up).astype(BF16)
        acc = acc + jnp.dot(act, wd_ref[j], preferred_element_type=F32)
    out_ref[...] = x + mod_ref[5] * acc


def _ffn(h, mod, ng, wts):
    tm = TM
    consts = [ng, wts["ff_g"], wts["ff_u"], wts["ff_d"]]
    return pl.pallas_call(
        _ffn_kernel,
        out_shape=jax.ShapeDtypeStruct((N_ROWS, D_MODEL), F32),
        grid=(N_ROWS // tm,),
        in_specs=[_rows(tm, D_MODEL), _mod_spec(tm)] + [_const(a.shape) for a in consts],
        out_specs=_rows(tm, D_MODEL),
        input_output_aliases={0: 0},
        compiler_params=_params("arbitrary"),
        name="ffn",
    )(h, mod, *consts)


def _odd_in_kernel(h_ref, mod_ref, ng_ref, win_ref, qg_ref, kvg_ref, wq_ref, wqs_ref, wk_ref, wv_ref,
                   place_ref, cq_ref, sq_ref, ck_ref, sk_ref, q_ref, k_ref, v_ref):
    n = _rms(h_ref[...], ng_ref[...]) * (1.0 + mod_ref[1]) + mod_ref[0]
    p = jnp.dot(n.astype(BF16), win_ref[...], preferred_element_type=F32)
    cq = _rms(p[:, :C_Q_RANK], qg_ref[...]).astype(BF16)
    ckv = _rms(p[:, C_Q_RANK:C_Q_RANK + C_KV_RANK], kvg_ref[...]).astype(BF16)
    base = C_Q_RANK + C_KV_RANK
    kr = p[:, base:base + LANE] * ck_ref[...] + p[:, base + LANE:base + 2 * LANE] * sk_ref[...]

    q_plain = jnp.dot(cq, wq_ref[...], preferred_element_type=F32)
    q_swap = jnp.dot(cq, wqs_ref[...], preferred_element_type=F32)
    cq_t, sq_t = cq_ref[...], sq_ref[...]
    for h in range(C_HEADS):
        cols = slice(h * HEAD_PAD, (h + 1) * HEAD_PAD)
        q_ref[:, cols] = ((q_plain[:, cols] * cq_t + q_swap[:, cols] * sq_t) * (C_SCALE * LOG2_E)).astype(q_ref.dtype)
    k_cat = jnp.dot(ckv, wk_ref[...], preferred_element_type=F32) + _mm(kr, place_ref[...])
    k_ref[...] = k_cat.astype(k_ref.dtype)
    v_ref[...] = jnp.dot(ckv, wv_ref[...], preferred_element_type=F32).astype(v_ref.dtype)


def _odd_in(h, mod, ng, wts, rope):
    tm = TM
    n_tiles = N_ROWS // tm
    consts = [ng, wts["w_in"], wts["q_norm_g"], wts["kv_norm_g"], wts["w_q"], wts["w_q_swap"],
              wts["w_k"], wts["w_v"], wts["place"]]
    ctx_tiles = NC_ROWS // tm
    lat_tiles = SEQ // tm

    def rope_spec():
        return pl.BlockSpec((tm, LANE), lambda i: (jnp.where(i < ctx_tiles, lat_tiles, (i - ctx_tiles) % lat_tiles), 0))

    return pl.pallas_call(
        _odd_in_kernel,
        out_shape=(jax.ShapeDtypeStruct((N_ROWS, C_QK_PAD), BF16),) * 3,
        grid=(n_tiles,),
        in_specs=[_rows(tm, D_MODEL), _mod_spec(tm)] + [_const(a.shape) for a in consts] + [rope_spec()] * 4,
        out_specs=(_rows(tm, C_QK_PAD),) * 3,
        compiler_params=_params("arbitrary"),
        name="odd_in",
    )(h, mod, *consts, *rope)


def _attn_kernel(*refs, n_pieces):
    q_ref = refs[0]
    k_refs = refs[1:1 + n_pieces]
    v_refs = refs[1 + n_pieces:1 + 2 * n_pieces]
    o_ref = refs[-1]
    for pair in range(ATT_HG // 2):
        acc = None
        for hh in range(2):
            cols = slice((2 * pair + hh) * HEAD_PAD, (2 * pair + hh + 1) * HEAD_PAD)
            q = q_ref[:, cols]
            s = [lax.dot_general(q, k[:, cols], (((1,), (1,)), ((), ())), preferred_element_type=F32)
                 for k in k_refs]
            m = functools.reduce(jnp.maximum, [jnp.max(x, axis=-1, keepdims=True) for x in s])
            e = [jnp.exp2(x - m) for x in s]
            l = functools.reduce(jnp.add, [jnp.sum(x, axis=-1, keepdims=True) for x in e])
            o = functools.reduce(jnp.add, [jnp.dot(x.astype(BF16), v[:, cols], preferred_element_type=F32)
                                           for x, v in zip(e, v_refs)])
            o = o * (1.0 / l)
            acc = o if acc is None else acc + o
        o_ref[:, pair * LANE:(pair + 1) * LANE] = acc.astype(o_ref.dtype)


def _attention(q, k, v, latent, prev=None):
    width = ATT_HG * HEAD_PAD
    n_hg = C_HEADS // ATT_HG
    if latent:
        tq = ATT_TQ
        nq = SEQ // tq
        q_spec = pl.BlockSpec((tq, width), lambda b, g, t: ((NC_ROWS + b * SEQ) // tq + t, g))
        kv_specs = [pl.BlockSpec((CTX_LEN, width), lambda b, g, t: (b, g)),
                    pl.BlockSpec((SEQ, width), lambda b, g, t: (NC_ROWS // SEQ + b, g))]
    else:
        tq = CTX_LEN
        nq = 1
        q_spec = pl.BlockSpec((tq, width), lambda b, g, t: (b, g))
        kv_specs = [pl.BlockSpec((CTX_LEN, width), lambda b, g, t: (b, g))]
    n_pieces = len(kv_specs)
    o_spec = pl.BlockSpec((tq, ATT_HG * C_VDIM), q_spec.index_map)
    ins = [q] + [k] * n_pieces + [v] * n_pieces
    in_specs = [q_spec] + kv_specs + kv_specs
    aliases = {}
    if prev is not None:
        aliases = {len(ins): 0}
        ins.append(prev)
        in_specs.append(pl.BlockSpec(memory_space=pl.ANY))
    return pl.pallas_call(
        functools.partial(_attn_kernel, n_pieces=n_pieces),
        out_shape=jax.ShapeDtypeStruct((N_ROWS, C_HEADS * C_VDIM), BF16),
        grid=(BATCH, n_hg, nq),
        in_specs=in_specs,
        out_specs=o_spec,
        input_output_aliases=aliases,
        compiler_params=_params("arbitrary", "arbitrary", "arbitrary"),
        name="attn_latent" if latent else "attn_context",
    )(*ins)


def _odd_out_kernel(h_ref, mod_ref, o_ref, w_ref, out_ref):
    y = jnp.dot(o_ref[...], w_ref[...], preferred_element_type=F32)
    out_ref[...] = h_ref[...] + mod_ref[2] * y


def _odd_out(h, mod, o, w_out, tile0, n_tiles):
    tm = TM
    return pl.pallas_call(
        _odd_out_kernel,
        out_shape=jax.ShapeDtypeStruct((N_ROWS, D_MODEL), F32),
        grid=(n_tiles,),
        in_specs=[_rows(tm, D_MODEL, tile0), _mod_spec(tm, tile0), _rows(tm, C_HEADS * C_VDIM, tile0),
                  _const(w_out.shape)],
        out_specs=_rows(tm, D_MODEL, tile0),
        input_output_aliases={0: 0},
        compiler_params=_params("arbitrary"),
        name="odd_out",
    )(h, mod, o, w_out)


def _route(m, router):
    m_hi, m_lo = _split_bf16(m)
    r_hi, r_lo = _split_bf16(router)
    dot = functools.partial(jnp.dot, preferred_element_type=F32)
    logits = dot(m_hi, r_hi) + (dot(m_hi, r_lo) + dot(m_lo, r_hi))
    lane = lax.broadcasted_iota(jnp.int32, logits.shape, 1)
    neg = -jnp.inf
    l1 = jnp.where(lane < N_EXPERTS, logits, neg)
    m1 = jnp.max(l1, axis=-1, keepdims=True)
    i1 = jnp.min(jnp.where(l1 == m1, lane, LANE), axis=-1, keepdims=True)
    l2 = jnp.where(lane == i1, neg, l1)
    m2 = jnp.max(l2, axis=-1, keepdims=True)
    i2 = jnp.min(jnp.where(l2 == m2, lane, LANE), axis=-1, keepdims=True)
    e2 = jnp.exp(m2 - m1)
    w1 = 1.0 / (1.0 + e2)
    w2 = e2 / (1.0 + e2)
    return i1, i2, w1, w2


R_E0, R_E1, R_RANK0, R_RANK1, R_W0, R_W1 = range(6)
R_ROWS = 8
F_VALID, F_FIRST, F_LAST = 1, 2, 4


def _moe_route_kernel(h_ref, mod_ref, ng_ref, rt_ref, m_ref, rcol_ref, rrow_ref, cnt_ref, carry_s):
    @pl.when(pl.program_id(0) == 0)
    def _():
        carry_s[...] = jnp.zeros_like(carry_s)

    m = _rms(h_ref[...], ng_ref[...]) * (1.0 + mod_ref[4]) + mod_ref[3]
    m_ref[...] = m.astype(m_ref.dtype)
    i1, i2, w1, w2 = _route(m, rt_ref[...])
    tm = m.shape[0]
    lane = lax.broadcasted_iota(jnp.int32, (tm, LANE), 1)
    picked = jnp.where(lane == i1, 1.0, jnp.where(lane == i2, 1.0, 0.0))
    ri = lax.broadcasted_iota(jnp.int32, (tm, tm), 0)
    ci = lax.broadcasted_iota(jnp.int32, (tm, tm), 1)
    earlier = jnp.where(ci < ri, 1.0, 0.0).astype(BF16)
    before = carry_s[0:1, :]
    rank_all = jnp.dot(earlier, picked.astype(BF16), preferred_element_type=F32) + before
    rank0 = jnp.sum(jnp.where(lane == i1, rank_all, 0.0), axis=-1, keepdims=True)
    rank1 = jnp.sum(jnp.where(lane == i2, rank_all, 0.0), axis=-1, keepdims=True)
    after = before + jnp.sum(picked, axis=0, keepdims=True)
    carry_s[...] = jnp.broadcast_to(after, carry_s.shape)
    sub = lax.broadcasted_iota(jnp.int32, cnt_ref.shape, 0)
    cnt_ref[...] = jnp.where(sub == 0, before, jnp.where(sub == 1, after, 0.0))

    fields = {R_E0: i1.astype(F32), R_E1: i2.astype(F32), R_RANK0: rank0, R_RANK1: rank1, R_W0: w1, R_W1: w2}
    rcol = jnp.zeros((tm, LANE), F32)
    for k, val in fields.items():
        rcol = jnp.where(lane == k, val, rcol)
    rcol_ref[...] = rcol
    rrow_ref[...] = jnp.transpose(rcol)[:R_ROWS, :]


def _moe_route(h, mod, ng, router, tile0, n_chunks):
    tm = MOE_TC
    rows = n_chunks * tm
    return pl.pallas_call(
        _moe_route_kernel,
        out_shape=(jax.ShapeDtypeStruct((rows, D_MODEL), BF16),
                   jax.ShapeDtypeStruct((rows, LANE), F32),
                   jax.ShapeDtypeStruct((R_ROWS, rows), F32),
                   jax.ShapeDtypeStruct((n_chunks, R_ROWS, LANE), F32)),
        grid=(n_chunks,),
        in_specs=[_rows(tm, D_MODEL, tile0), _mod_spec(tm, tile0), _const(ng.shape), _const(router.shape)],
        out_specs=(_rows(tm, D_MODEL), _rows(tm, LANE),
                   pl.BlockSpec((R_ROWS, tm), lambda i: (0, i)),
                   pl.BlockSpec((None, R_ROWS, LANE), lambda i: (i, 0, 0))),
        scratch_shapes=[pltpu.VMEM((R_ROWS, LANE), F32)],
        compiler_params=_params("arbitrary"),
        name="moe_route",
    )(h, mod, ng, router)


def _moe_plan(cnt, n_chunks, n_blk):
    i32 = jnp.int32
    before = cnt[:, 0, :N_EXPERTS].astype(i32)
    after = cnt[:, 1, :N_EXPERTS].astype(i32)
    padded = (after[-1] + MOE_XT - 1) // MOE_XT * MOE_XT
    ends = jnp.cumsum(padded).astype(i32)
    base = ends - padded
    used_rows = ends[-1]
    start = base[None, :] + before
    stop = base[None, :] + after
    lo = jnp.arange(n_blk, dtype=i32) * MOE_BLK
    overlap = (start[:, :, None] < lo + MOE_BLK) & (stop[:, :, None] > lo)
    inc = jnp.any(overlap, axis=1)
    fill = (lo < used_rows) & ~jnp.any(inc, axis=0)
    inc_fill = inc.at[0].set(inc[0] | fill)
    n_items = n_chunks * N_EXPERTS + n_blk

    def items(mat):
        idx = jnp.nonzero(mat.reshape(-1), size=n_items, fill_value=-1)[0].astype(i32)
        valid = idx >= 0
        n_valid = jnp.sum(valid).astype(i32)
        idx = jnp.where(valid, idx, idx[jnp.maximum(n_valid - 1, 0)])
        outer = idx // mat.shape[1]
        inner = idx % mat.shape[1]
        prev = jnp.concatenate([jnp.full((1,), -1, i32), outer[:-1]])
        nxt = jnp.concatenate([outer[1:], jnp.full((1,), -1, i32)])
        pos = jnp.arange(n_items, dtype=i32)
        first = valid & (outer != prev)
        last = valid & ((outer != nxt) | (pos == n_valid - 1))
        flags = valid.astype(i32) * F_VALID + first.astype(i32) * F_FIRST + last.astype(i32) * F_LAST
        return outer, inner, flags

    disp_blk, disp_chunk, disp_flags = items(inc_fill.T)
    comb_chunk, comb_blk, comb_flags = items(inc)
    n_xt = n_blk * MOE_BLK // MOE_XT
    t0 = jnp.arange(n_xt, dtype=i32) * MOE_XT
    tile_used = (t0 < used_rows).astype(i32)
    tile_src = jnp.minimum(jnp.arange(n_xt, dtype=i32), used_rows // MOE_XT - 1)
    tile_e = jnp.minimum(jnp.sum(t0[:, None] >= ends[None, :], axis=1), N_EXPERTS - 1).astype(i32)[tile_src]
    return {"base": base, "disp": (disp_chunk, disp_blk, disp_flags), "comb": (comb_chunk, comb_blk, comb_flags),
            "tiles": (tile_e, tile_src, tile_used)}


def _sorted_row(expert, rank, base_ref):
    start = jnp.zeros_like(expert)
    for e in range(N_EXPERTS):
        start = jnp.where(expert == float(e), base_ref[e].astype(F32), start)
    return start + rank


def _moe_dispatch_kernel(c_ref, d_ref, f_ref, base_ref, m_ref, rrow_ref, x_ref):
    n = pl.program_id(0)
    flag = f_ref[n]

    @pl.when((flag & F_VALID) != 0)
    def _():
        rr = rrow_ref[...]
        row0 = (d_ref[n] * MOE_BLK).astype(F32)
        rel0 = _sorted_row(rr[R_E0:R_E0 + 1], rr[R_RANK0:R_RANK0 + 1], base_ref) - row0
        rel1 = _sorted_row(rr[R_E1:R_E1 + 1], rr[R_RANK1:R_RANK1 + 1], base_ref) - row0
        ri = lax.broadcasted_iota(jnp.int32, (MOE_BLK, MOE_TC), 0).astype(F32)
        onehot = jnp.where(rel0 == ri, 1.0, jnp.where(rel1 == ri, 1.0, 0.0)).astype(BF16)
        rows = jnp.dot(onehot, m_ref[...], preferred_element_type=F32).astype(x_ref.dtype)

        @pl.when((flag & F_FIRST) != 0)
        def _():
            x_ref[...] = rows

        @pl.when((flag & F_FIRST) == 0)
        def _():
            x_ref[...] = x_ref[...] + rows


def _moe_dispatch(m, rrow, plan, n_blk):
    chunk, blk, flags = plan["disp"]
    n_items = chunk.shape[0]
    grid_spec = pltpu.PrefetchScalarGridSpec(
        num_scalar_prefetch=4,
        grid=(n_items,),
        in_specs=[pl.BlockSpec((MOE_TC, D_MODEL), lambda n, c, d, f, b: (c[n], 0)),
                  pl.BlockSpec((R_ROWS, MOE_TC), lambda n, c, d, f, b: (0, c[n]))],
        out_specs=pl.BlockSpec((MOE_BLK, D_MODEL), lambda n, c, d, f, b: (d[n], 0)),
    )
    return pl.pallas_call(
        _moe_dispatch_kernel,
        out_shape=jax.ShapeDtypeStruct((n_blk * MOE_BLK, D_MODEL), BF16),
        grid_spec=grid_spec,
        compiler_params=_params("arbitrary"),
        name="moe_dispatch",
    )(chunk, blk, flags, plan["base"], m, rrow)


def _moe_experts_kernel(e_ref, src_ref, used_ref, x_ref, wg_ref, wu_ref, wd_ref, y_ref, acc_s):
    t = pl.program_id(0)
    j = pl.program_id(1)
    last = j == pl.num_programs(1) - 1
    used = used_ref[t] != 0

    @pl.when(used)
    def _():
        x = x_ref[...]
        gate = jnp.dot(x, wg_ref[...].astype(BF16), preferred_element_type=F32)
        up = jnp.dot(x, wu_ref[...].astype(BF16), preferred_element_type=F32)
        act = (_silu(gate) * up).astype(BF16)
        part = jnp.dot(act, wd_ref[...].astype(BF16), preferred_element_type=F32)

        @pl.when(j == 0)
        def _():
            acc_s[...] = part

        @pl.when(j != 0)
        def _():
            acc_s[...] += part

        @pl.when(last)
        def _():
            y_ref[...] = acc_s[...].astype(y_ref.dtype)

    @pl.when(jnp.logical_not(used) & last)
    def _():
        y_ref[...] = jnp.zeros_like(y_ref)


def _moe_experts(x, plan, wts):
    tile_e, tile_src, tile_used = plan["tiles"]
    n_xt = tile_e.shape[0]
    th = MOE_TH
    nj = MOE_HIDDEN // th

    def hidden(t, j, used):
        return jnp.where(used[t] != 0, j, nj - 1)

    grid_spec = pltpu.PrefetchScalarGridSpec(
        num_scalar_prefetch=3,
        grid=(n_xt, nj),
        in_specs=[pl.BlockSpec((MOE_XT, D_MODEL), lambda t, j, e, s, u: (s[t], 0)),
                  pl.BlockSpec((None, D_MODEL, th), lambda t, j, e, s, u: (e[t], 0, hidden(t, j, u))),
                  pl.BlockSpec((None, D_MODEL, th), lambda t, j, e, s, u: (e[t], 0, hidden(t, j, u))),
                  pl.BlockSpec((None, th, D_MODEL), lambda t, j, e, s, u: (e[t], hidden(t, j, u), 0))],
        out_specs=pl.BlockSpec((MOE_XT, D_MODEL), lambda t, j, e, s, u: (t, 0)),
        scratch_shapes=[pltpu.VMEM((MOE_XT, D_MODEL), F32)],
    )
    return pl.pallas_call(
        _moe_experts_kernel,
        out_shape=jax.ShapeDtypeStruct((n_xt * MOE_XT, D_MODEL), BF16),
        grid_spec=grid_spec,
        compiler_params=_params("arbitrary", "arbitrary"),
        name="moe_experts",
    )(tile_e, tile_src, tile_used, x, wts["moe_g"], wts["moe_u"], wts["moe_d"])


def _moe_combine_kernel(c_ref, d_ref, f_ref, base_ref, rcol_ref, y_ref, h_ref, mod_ref, out_ref, acc_s):
    n = pl.program_id(0)
    flag = f_ref[n]

    @pl.when((flag & F_VALID) != 0)
    def _():
        rc = rcol_ref[...]
        row0 = (d_ref[n] * MOE_BLK).astype(F32)
        rel0 = _sorted_row(rc[:, R_E0:R_E0 + 1], rc[:, R_RANK0:R_RANK0 + 1], base_ref) - row0
        rel1 = _sorted_row(rc[:, R_E1:R_E1 + 1], rc[:, R_RANK1:R_RANK1 + 1], base_ref) - row0
        li = lax.broadcasted_iota(jnp.int32, (MOE_TC, MOE_BLK), 1).astype(F32)
        gates = jnp.where(rel0 == li, rc[:, R_W0:R_W0 + 1], jnp.where(rel1 == li, rc[:, R_W1:R_W1 + 1], 0.0))
        part = jnp.dot(gates.astype(BF16), y_ref[...], preferred_element_type=F32)

        @pl.when((flag & F_FIRST) != 0)
        def _():
            acc_s[...] = part

        @pl.when((flag & F_FIRST) == 0)
        def _():
            acc_s[...] += part

        @pl.when((flag & F_LAST) != 0)
        def _():
            out_ref[...] = h_ref[...] + mod_ref[5] * acc_s[...]


def _moe_combine(h, mod, rcol, y, plan, tile0):
    chunk, blk, flags = plan["comb"]
    n_items = chunk.shape[0]
    grid_spec = pltpu.PrefetchScalarGridSpec(
        num_scalar_prefetch=4,
        grid=(n_items,),
        in_specs=[pl.BlockSpec((MOE_TC, LANE), lambda n, c, d, f, b: (c[n], 0)),
                  pl.BlockSpec((MOE_BLK, D_MODEL), lambda n, c, d, f, b: (d[n], 0)),
                  pl.BlockSpec((MOE_TC, D_MODEL), lambda n, c, d, f, b: (c[n] + tile0, 0)),
                  pl.BlockSpec((N_MOD, None, 1, D_MODEL),
                               lambda n, c, d, f, b: (0, _mod_row(c[n] + tile0, MOE_TC), 0, 0))],
        out_specs=pl.BlockSpec((MOE_TC, D_MODEL), lambda n, c, d, f, b: (c[n] + tile0, 0)),
        scratch_shapes=[pltpu.VMEM((MOE_TC, D_MODEL), F32)],
    )
    return pl.pallas_call(
        _moe_combine_kernel,
        out_shape=jax.ShapeDtypeStruct((N_ROWS, D_MODEL), F32),
        grid_spec=grid_spec,
        input_output_aliases={6: 0},
        compiler_params=_params("arbitrary"),
        name="moe_combine",
    )(chunk, blk, flags, plan["base"], rcol, y, h, mod)


def _moe(h, mod, ng, wts, first_row):
    tile0 = first_row // MOE_TC
    n_chunks = (N_ROWS - first_row) // MOE_TC
    n_blk = (2 * n_chunks * MOE_TC + N_EXPERTS * MOE_XT) // MOE_BLK
    m, rcol, rrow, cnt = _moe_route(h, mod, ng, wts["router"], tile0, n_chunks)
    plan = _moe_plan(cnt, n_chunks, n_blk)
    x = _moe_dispatch(m, rrow, plan, n_blk)
    y = _moe_experts(x, plan, wts)
    return _moe_combine(h, mod, rcol, y, plan, tile0)


def _final_kernel(h_ref, g_ref, o_ref):
    o_ref[...] = _rms(h_ref[...], g_ref[...])


def _final_norm(h, g):
    tm = TM
    return pl.pallas_call(
        _final_kernel,
        out_shape=jax.ShapeDtypeStruct((NL_ROWS, D_MODEL), F32),
        grid=(NL_ROWS // tm,),
        in_specs=[_rows(tm, D_MODEL, NC_ROWS // tm), _const(g.shape)],
        out_specs=_rows(tm, D_MODEL),
        compiler_params=_params("arbitrary"),
        name="final_norm",
    )(h, g)


def _row_vec(x):
    return x.reshape(1, -1).astype(F32)


def _even_weights(i, ev_w_in, a_ln_g, a_ln_b, a_ws, a_bs, b_gate_w, b_gate_b, b_norm_g, ev_w_out,
                  ff_w_gate, ff_w_up, ff_w_down):
    w_in = ev_w_in[i]
    qkvr = 2 * B_QK + 2 * B_V
    gate_w = jnp.zeros((2 * B_GATE_RANK, 2 * B_QK), F32)
    gate_w = gate_w.at[:B_GATE_RANK, :B_QK].set(b_gate_w[i, 0]).at[B_GATE_RANK:, B_QK:].set(b_gate_w[i, 1])
    nch = FF_HIDDEN // FF_TH
    return {
        "w_a": w_in[:, :2 * A_WIDTH].astype(BF16),
        "w_q": w_in[:, 2 * A_WIDTH:2 * A_WIDTH + qkvr].astype(BF16),
        "w_gf": w_in[:, 2 * A_WIDTH + qkvr:].astype(BF16),
        "gate_w": gate_w.astype(BF16),
        "gate_b": b_gate_b[i].reshape(1, 2 * B_QK).astype(F32),
        "ln_g": _row_vec(a_ln_g[i]),
        "ln_b": _row_vec(a_ln_b[i]),
        "ws": a_ws[i].astype(BF16),
        "bs": jnp.broadcast_to(a_bs[i][:, :, None], (A_HEADS, A_CHUNK, A_HEAD_DIM)).astype(F32),
        "b_norm_g": _row_vec(b_norm_g[i]),
        "w_out_a": ev_w_out[i, :A_WIDTH].astype(BF16),
        "w_out_b": ev_w_out[i, A_WIDTH:].astype(BF16),
        "ff_g": jnp.transpose(ff_w_gate[i].reshape(D_MODEL, nch, FF_TH), (1, 0, 2)).astype(BF16),
        "ff_u": jnp.transpose(ff_w_up[i].reshape(D_MODEL, nch, FF_TH), (1, 0, 2)).astype(BF16),
        "ff_d": ff_w_down[i].reshape(nch, FF_TH, D_MODEL).astype(BF16),
    }


def _rope_swap_index():
    quarter = C_ROPE // 4
    idx = np.arange(C_ROPE)
    return np.concatenate([idx[quarter:2 * quarter], idx[:quarter], idx[3 * quarter:], idx[2 * quarter:3 * quarter]])


def _odd_weights(i, od_w_in, c_q_norm_g, c_w_uq, c_kv_norm_g, c_w_ukv, od_w_out,
                 moe_router, moe_w_gate, moe_w_up, moe_w_down):
    swap = _rope_swap_index()
    w_in = od_w_in[i]
    low = C_Q_RANK + C_KV_RANK
    k_rope_w = w_in[:, low:]
    pad = jnp.zeros((D_MODEL, LANE - C_ROPE), F32)
    w_in_x = jnp.concatenate([w_in[:, :low], k_rope_w, pad, k_rope_w[:, swap], pad], axis=1)

    uq = c_w_uq[i].reshape(C_Q_RANK, C_HEADS, C_NOPE + C_ROPE)
    zq = jnp.zeros((C_Q_RANK, C_HEADS, HEAD_PAD - C_NOPE - C_ROPE), F32)
    w_q = jnp.concatenate([uq, zq], axis=-1)
    w_q_swap = jnp.concatenate([jnp.zeros((C_Q_RANK, C_HEADS, C_NOPE), F32), uq[:, :, C_NOPE:][:, :, swap], zq], axis=-1)
    ukv = c_w_ukv[i].reshape(C_KV_RANK, C_HEADS, C_NOPE + C_VDIM)
    w_k = jnp.concatenate([ukv[:, :, :C_NOPE], jnp.zeros((C_KV_RANK, C_HEADS, HEAD_PAD - C_NOPE), F32)], axis=-1)
    zv = jnp.zeros((C_KV_RANK, C_HEADS // 2, C_VDIM), F32)
    uv = ukv[:, :, C_NOPE:].reshape(C_KV_RANK, C_HEADS // 2, 2, C_VDIM)
    w_v = jnp.stack([jnp.concatenate([uv[:, :, 0], zv], axis=-1),
                     jnp.concatenate([zv, uv[:, :, 1]], axis=-1)], axis=2)
    place = np.zeros((LANE, C_HEADS, HEAD_PAD), np.float32)
    for r in range(C_ROPE):
        place[r, :, C_NOPE + r] = 1.0
    router = jnp.concatenate([moe_router[i], jnp.zeros((D_MODEL, LANE - N_EXPERTS), F32)], axis=1)
    return {
        "w_in": w_in_x.astype(BF16),
        "q_norm_g": _row_vec(c_q_norm_g[i]),
        "kv_norm_g": _row_vec(c_kv_norm_g[i]),
        "w_q": w_q.reshape(C_Q_RANK, C_QK_PAD).astype(BF16),
        "w_q_swap": w_q_swap.reshape(C_Q_RANK, C_QK_PAD).astype(BF16),
        "w_k": w_k.reshape(C_KV_RANK, C_QK_PAD).astype(BF16),
        "w_v": w_v.reshape(C_KV_RANK, C_QK_PAD).astype(BF16),
        "place": jnp.asarray(place.reshape(LANE, C_QK_PAD), BF16),
        "w_out": od_w_out[i].astype(BF16),
        "router": router,
        "moe_g": moe_w_gate[i],
        "moe_u": moe_w_up[i],
        "moe_d": moe_w_down[i],
    }


def _rope_tables(tm):
    quarter = C_ROPE // 4
    half = C_ROPE // 2
    inv_freq = ROPE_BASE ** (-jnp.arange(0, half, 2, dtype=F32) / half)
    pos = jnp.arange(SEQ, dtype=jnp.int32)
    ang_r = (pos // GRID_W).astype(F32)[:, None] * inv_freq
    ang_c = (pos % GRID_W).astype(F32)[:, None] * inv_freq
    cos = jnp.concatenate([jnp.cos(ang_r)] * 2 + [jnp.cos(ang_c)] * 2, axis=1)
    sin = jnp.concatenate([-jnp.sin(ang_r), jnp.sin(ang_r), -jnp.sin(ang_c), jnp.sin(ang_c)], axis=1)
    cos = jnp.concatenate([cos, jnp.ones((tm, C_ROPE), F32)], axis=0)
    sin = jnp.concatenate([sin, jnp.zeros((tm, C_ROPE), F32)], axis=0)
    rows = SEQ + tm
    ones = jnp.ones((rows, C_NOPE), F32)
    z = lambda w: jnp.zeros((rows, w), F32)
    cq = jnp.concatenate([ones, cos, z(HEAD_PAD - C_NOPE - C_ROPE)], axis=1)
    sq = jnp.concatenate([z(C_NOPE), sin, z(HEAD_PAD - C_NOPE - C_ROPE)], axis=1)
    ck = jnp.concatenate([cos, z(LANE - C_ROPE)], axis=1)
    sk = jnp.concatenate([sin, z(LANE - C_ROPE)], axis=1)
    return cq, sq, ck, sk


def kernel(x, c, ctx, c_ctx, mod_w, mod_b, norm_mix_g, norm_ffn_g, final_g, ev_w_in, a_ln_g, a_ln_b, a_ws, a_bs, b_gate_w, b_gate_b, b_norm_g, ev_w_out, od_w_in, c_q_norm_g, c_w_uq, c_kv_norm_g, c_w_ukv, od_w_out, ff_w_gate, ff_w_up, ff_w_down, moe_router, moe_w_gate, moe_w_up, moe_w_down):
    h = jnp.concatenate([ctx.reshape(NC_ROWS, D_MODEL), x.reshape(NL_ROWS, D_MODEL)], axis=0)
    cond = jnp.concatenate([c, c_ctx[None, :], jnp.zeros((MOD_PAD - MOD_ROWS, D_MODEL), F32)], axis=0)
    mod = _modulation(cond, mod_w, mod_b)
    rope = _rope_tables(TM)

    for layer in range(DEPTH):
        i = layer // 2
        need_ctx = layer < DEPTH - 1
        ng_mix = _row_vec(norm_mix_g[layer])
        ng_ffn = _row_vec(norm_ffn_g[layer])
        if layer % 2 == 0:
            wts = _even_weights(i, ev_w_in, a_ln_g, a_ln_b, a_ws, a_bs, b_gate_w, b_gate_b, b_norm_g,
                                ev_w_out, ff_w_gate, ff_w_up, ff_w_down)
            ya, qkv, r, g = _even_in(h, mod[layer], ng_mix, wts)
            o_f, o_b = _gla(qkv, g)
            h = _even_out(h, mod[layer], o_f, o_b, r, ya, wts)
            h = _ffn(h, mod[layer], ng_ffn, wts)
        else:
            wts = _odd_weights(i, od_w_in, c_q_norm_g, c_w_uq, c_kv_norm_g, c_w_ukv, od_w_out,
                               moe_router, moe_w_gate, moe_w_up, moe_w_down)
            q, k, v = _odd_in(h, mod[layer], ng_mix, wts, rope)
            o = _attention(q, k, v, latent=True)
            if need_ctx:
                o = _attention(q, k, v, latent=False, prev=o)
            first = 0 if need_ctx else NC_ROWS
            h = _odd_out(h, mod[layer], o, wts["w_out"], first // TM, (N_ROWS - first) // TM)
            h = _moe(h, mod[layer], ng_ffn, wts, first)
    out = _final_norm(h, _row_vec(final_g))
    return out.reshape(BATCH, SEQ, D_MODEL)
```

```python
import functools

import numpy as np
import jax
import jax.numpy as jnp
from jax import lax
from jax.experimental import pallas as pl
from jax.experimental.pallas import tpu as pltpu

D_MODEL = 1024
BATCH = 8
SEQ = 2048
DEPTH = 4
GRID_W = 64
CTX_LEN = 256
EPS = 1e-6
N_MOD = 6

A_HEADS = 4
A_HEAD_DIM = 128
A_CHUNK = 128
A_WIDTH = A_HEADS * A_HEAD_DIM

B_HEADS = 4
B_DK = 64
B_DV = 128
B_QK = B_HEADS * B_DK
B_V = B_HEADS * B_DV
B_GATE_RANK = 16
B_GATE_TEMP = 16.0
B_CHUNK = 64

C_HEADS = 16
C_NOPE = 64
C_ROPE = 32
C_VDIM = 64
C_Q_RANK = 256
C_KV_RANK = 256
C_SCALE = (C_NOPE + C_ROPE) ** -0.5
LOG2_E = 1.4426950408889634
ROPE_BASE = 10000.0

FF_HIDDEN = 2816
N_EXPERTS = 8
MOE_HIDDEN = 3584

NC_ROWS = BATCH * CTX_LEN
NL_ROWS = BATCH * SEQ
N_ROWS = NC_ROWS + NL_ROWS
MOD_ROWS = BATCH + 1
MOD_PAD = 16

LANE = 128
HEAD_PAD = 128
C_QK_PAD = C_HEADS * HEAD_PAD

TM = 512
GLA_BLK = 256
ATT_TQ = 512
ATT_HG = 4
FF_TH = 256
MOE_TC = 512
MOE_BLK = 256
MOE_XT = 1024
MOE_TH = 512
VMEM_LIMIT = 56 * 1024 * 1024

F32 = jnp.float32
BF16 = jnp.bfloat16


def _mm(a, b):
    return jnp.dot(a.astype(BF16), b.astype(BF16), preferred_element_type=F32)


def _mm_nt(a, b):
    return lax.dot_general(a.astype(BF16), b.astype(BF16), (((1,), (1,)), ((), ())),
                           preferred_element_type=F32)


def _mm_tn(a, b):
    return lax.dot_general(a.astype(BF16), b.astype(BF16), (((0,), (0,)), ((), ())),
                           preferred_element_type=F32)


def _sigmoid(x):
    return 1.0 / (1.0 + jnp.exp(-x))


def _silu(x):
    return x * _sigmoid(x)


def _gelu(x):
    return 0.5 * x * (1.0 + jnp.tanh(np.sqrt(2.0 / np.pi) * (x + 0.044715 * (x * x * x))))


def _rms(x, g):
    return x * lax.rsqrt(jnp.mean(x * x, axis=-1, keepdims=True) + EPS) * g


def _split_bf16(x):
    hi = x.astype(BF16)
    lo = (x - hi.astype(F32)).astype(BF16)
    return hi, lo


def _rows(tm, width, tile0=0, col=0):
    return pl.BlockSpec((tm, width), lambda i: (i + tile0, col))


def _const(shape):
    zeros = (0,) * len(shape)
    return pl.BlockSpec(shape, lambda *_: zeros, pipeline_mode=pl.Buffered(1))


def _mod_row(tile, tm):
    return jnp.where(tile < NC_ROWS // tm, BATCH, (tile * tm - NC_ROWS) // SEQ)


def _mod_spec(tm, tile0=0):
    return pl.BlockSpec((N_MOD, None, 1, D_MODEL), lambda i: (0, _mod_row(i + tile0, tm), 0, 0))


def _params(*sem):
    return pltpu.CompilerParams(dimension_semantics=sem, vmem_limit_bytes=VMEM_LIMIT)


def _mod_kernel(c_ref, w_ref, b_ref, o_ref):
    o_ref[...] = _mm(_silu(c_ref[...]), w_ref[...]) + b_ref[...]


def _modulation(cond, mod_w, mod_b):
    tn = 1536
    out = pl.pallas_call(
        _mod_kernel,
        out_shape=jax.ShapeDtypeStruct((DEPTH, MOD_PAD, N_MOD * D_MODEL), F32),
        grid=(DEPTH, N_MOD * D_MODEL // tn),
        in_specs=[
            pl.BlockSpec((MOD_PAD, D_MODEL), lambda l, j: (0, 0)),
            pl.BlockSpec((None, D_MODEL, tn), lambda l, j: (l, 0, j)),
            pl.BlockSpec((None, 1, tn), lambda l, j: (l, 0, j)),
        ],
        out_specs=pl.BlockSpec((None, MOD_PAD, tn), lambda l, j: (l, 0, j)),
        compiler_params=_params("arbitrary", "arbitrary"),
        name="modulation",
    )(cond, mod_w, mod_b.reshape(DEPTH, 1, N_MOD * D_MODEL))
    out = out[:, :MOD_ROWS].reshape(DEPTH, MOD_ROWS, N_MOD, 1, D_MODEL)
    return jnp.transpose(out, (0, 2, 1, 3, 4))


def _even_in_kernel(h_ref, mod_ref, ng_ref, wa_ref, wq_ref, wgf_ref, gw_ref, gb_ref,
                    lng_ref, lnb_ref, ws_ref, bs_ref, ya_ref, qkv_ref, r_ref, g_ref):
    tm = h_ref.shape[0]
    n = _rms(h_ref[...], ng_ref[...]) * (1.0 + mod_ref[1]) + mod_ref[0]
    nb = n.astype(BF16)

    ua_va = jnp.dot(nb, wa_ref[...], preferred_element_type=F32)
    u = _gelu(ua_va[:, :A_WIDTH])
    v = _gelu(ua_va[:, A_WIDTH:])
    mu = jnp.mean(v, axis=-1, keepdims=True)
    vc = v - mu
    var = jnp.mean(vc * vc, axis=-1, keepdims=True)
    v = (vc * lax.rsqrt(var + EPS) * lng_ref[...] + lnb_ref[...]).astype(BF16)
    for c in range(tm // A_CHUNK):
        rows = slice(c * A_CHUNK, (c + 1) * A_CHUNK)
        for h in range(A_HEADS):
            cols = slice(h * A_HEAD_DIM, (h + 1) * A_HEAD_DIM)
            mixed = jnp.dot(ws_ref[h], v[rows, cols], preferred_element_type=F32) + bs_ref[h]
            ya_ref[rows, cols] = (u[rows, cols] * mixed).astype(ya_ref.dtype)

    p = jnp.dot(nb, wq_ref[...], preferred_element_type=F32)
    qkv_ref[...] = p[:, :2 * B_QK + B_V].astype(qkv_ref.dtype)
    r_ref[...] = p[:, 2 * B_QK + B_V:].astype(r_ref.dtype)
    gf = jnp.dot(nb, wgf_ref[...], preferred_element_type=F32)
    logit = _mm(gf, gw_ref[...]) + gb_ref[...]
    log_sig = jnp.minimum(logit, 0.0) - jnp.log(1.0 + jnp.exp(-jnp.abs(logit)))
    g_ref[...] = log_sig / B_GATE_TEMP


def _even_in(h, mod, ng, wts):
    tm = TM
    n_tiles = N_ROWS // tm
    qkv_w = 2 * B_QK + B_V
    consts = [ng, wts["w_a"], wts["w_q"], wts["w_gf"], wts["gate_w"], wts["gate_b"],
              wts["ln_g"], wts["ln_b"], wts["ws"], wts["bs"]]
    return pl.pallas_call(
        _even_in_kernel,
        out_shape=(jax.ShapeDtypeStruct((N_ROWS, A_WIDTH), BF16),
                   jax.ShapeDtypeStruct((N_ROWS, qkv_w), BF16),
                   jax.ShapeDtypeStruct((N_ROWS, B_V), BF16),
                   jax.ShapeDtypeStruct((N_ROWS, 2 * B_QK), F32)),
        grid=(n_tiles,),
        in_specs=[_rows(tm, D_MODEL), _mod_spec(tm)] + [_const(a.shape) for a in consts],
        out_specs=(_rows(tm, A_WIDTH), _rows(tm, qkv_w), _rows(tm, B_V), _rows(tm, 2 * B_QK)),
        compiler_params=_params("arbitrary"),
        name="even_in",
    )(h, mod, *consts)


def _gla_direction(q, k, v, g, st_ref, reverse):
    n = GLA_BLK
    nch = n // B_CHUNK
    ri = lax.broadcasted_iota(jnp.int32, (n, n), 0)
    ci = lax.broadcasted_iota(jnp.int32, (n, n), 1)
    same = (ri // B_CHUNK) == (ci // B_CHUNK)
    tri = same & ((ci >= ri) if reverse else (ci <= ri))
    tri_b = jnp.where(tri, 1.0, 0.0).astype(BF16)
    same_b = jnp.where(same, 1.0, 0.0).astype(BF16)

    g_hi, g_lo = _split_bf16(g)
    dot = functools.partial(jnp.dot, preferred_element_type=F32)
    b = dot(tri_b, g_hi) + dot(tri_b, g_lo)
    b_tot = dot(same_b, g_hi) + dot(same_b, g_lo)
    q_dec = (q * (B_DK ** -0.5)) * jnp.exp(b)
    k_inv = k * jnp.exp(-b)
    k_dec = k * jnp.exp(b_tot - b)
    decay = jnp.exp(b_tot)

    lane = lax.broadcasted_iota(jnp.int32, (1, LANE), 1)
    even_lane = lane < B_DK

    outs = [None] * B_HEADS
    order = list(range(nch))[::-1] if reverse else list(range(nch))
    for p in range(B_HEADS // 2):
        lanes = slice(p * LANE, (p + 1) * LANE)
        qp = q_dec[:, lanes]
        q_heads = (jnp.where(even_lane, qp, 0.0).astype(BF16), jnp.where(even_lane, 0.0, qp).astype(BF16))
        k_inv_p = k_inv[:, lanes].astype(BF16)
        k_dec_p = k_dec[:, lanes].astype(BF16)
        intra = []
        for hh in range(2):
            h = 2 * p + hh
            scores = _mm_nt(q_heads[hh], k_inv_p)
            scores = jnp.where(tri, scores, 0.0)
            intra.append(_mm(scores, v[:, h * B_DV:(h + 1) * B_DV]))
        st = st_ref[p]
        inter = [[None] * nch for _ in range(2)]
        for c in order:
            rows = slice(c * B_CHUNK, (c + 1) * B_CHUNK)
            st_b = st.astype(BF16)
            for hh in range(2):
                inter[hh][c] = _mm_nt(q_heads[hh][rows], st_b)
            upd_e = _mm_tn(v[rows, (2 * p) * B_DV:(2 * p + 1) * B_DV], k_dec_p[rows])
            upd_o = _mm_tn(v[rows, (2 * p + 1) * B_DV:(2 * p + 2) * B_DV], k_dec_p[rows])
            st = st * decay[c * B_CHUNK:c * B_CHUNK + 1, lanes] + jnp.where(even_lane, upd_e, upd_o)
        st_ref[p] = st
        for hh in range(2):
            outs[2 * p + hh] = intra[hh] + jnp.concatenate(inter[hh], axis=0)
    return outs


def _gla_kernel(qf_ref, kf_ref, vf_ref, gf_ref, qb_ref, kb_ref, vb_ref, gb_ref,
                of_ref, ob_ref, stf_ref, stb_ref):
    @pl.when(pl.program_id(1) == 0)
    def _():
        stf_ref[...] = jnp.zeros_like(stf_ref)
        stb_ref[...] = jnp.zeros_like(stb_ref)

    outs = _gla_direction(qf_ref[...].astype(F32), kf_ref[...].astype(F32), vf_ref[...], gf_ref[...], stf_ref, False)
    for h in range(B_HEADS):
        of_ref[:, h * B_DV:(h + 1) * B_DV] = outs[h]
    outs = _gla_direction(qb_ref[...].astype(F32), kb_ref[...].astype(F32), vb_ref[...], gb_ref[...], stb_ref, True)
    for h in range(B_HEADS):
        ob_ref[:, h * B_DV:(h + 1) * B_DV] = outs[h]


def _gla(qkv, g):
    nb_ctx = CTX_LEN // GLA_BLK
    nb_lat = SEQ // GLA_BLK
    nblk = nb_ctx + nb_lat
    ctx_blocks = NC_ROWS // GLA_BLK

    def fwd_row(b, j):
        return jnp.where(j < nb_ctx, b * nb_ctx + j, ctx_blocks + b * nb_lat + (j - nb_ctx))

    def bwd_row(b, j):
        return jnp.where(j < nb_ctx, b * nb_ctx + (nb_ctx - 1 - j),
                         ctx_blocks + b * nb_lat + (nblk - 1 - j))

    def spec(width, col, row):
        return pl.BlockSpec((GLA_BLK, width), lambda b, j: (row(b, j), col))

    ins, in_specs = [], []
    for d, row in enumerate((fwd_row, bwd_row)):
        ins += [qkv, qkv, qkv, g]
        in_specs += [spec(B_QK, 0, row), spec(B_QK, 1, row), spec(B_V, 1, row), spec(B_QK, d, row)]
    return pl.pallas_call(
        _gla_kernel,
        out_shape=(jax.ShapeDtypeStruct((N_ROWS, B_V), F32),) * 2,
        grid=(BATCH, nblk),
        in_specs=in_specs,
        out_specs=(spec(B_V, 0, fwd_row), spec(B_V, 0, bwd_row)),
        scratch_shapes=[pltpu.VMEM((B_HEADS // 2, B_DV, LANE), F32)] * 2,
        compiler_params=_params("arbitrary", "arbitrary"),
        name="gla",
    )(*ins)


def _even_out_kernel(h_ref, mod_ref, of_ref, ob_ref, r_ref, ya_ref, bng_ref, woa_ref, wob_ref, out_ref):
    o = of_ref[...] + ob_ref[...]
    parts = [_rms(o[:, h * B_DV:(h + 1) * B_DV], bng_ref[...]) for h in range(B_HEADS)]
    yb = _silu(r_ref[...].astype(F32)) * jnp.concatenate(parts, axis=-1)
    y = jnp.dot(ya_ref[...], woa_ref[...], preferred_element_type=F32) + _mm(yb, wob_ref[...])
    out_ref[...] = h_ref[...] + mod_ref[2] * y


def _even_out(h, mod, o_f, o_b, r, ya, wts):
    tm = TM
    consts = [wts["b_norm_g"], wts["w_out_a"], wts["w_out_b"]]
    return pl.pallas_call(
        _even_out_kernel,
        out_shape=jax.ShapeDtypeStruct((N_ROWS, D_MODEL), F32),
        grid=(N_ROWS // tm,),
        in_specs=[_rows(tm, D_MODEL), _mod_spec(tm), _rows(tm, B_V), _rows(tm, B_V), _rows(tm, B_V),
                  _rows(tm, A_WIDTH)] + [_const(a.shape) for a in consts],
        out_specs=_rows(tm, D_MODEL),
        input_output_aliases={0: 0},
        compiler_params=_params("arbitrary"),
        name="even_out",
    )(h, mod, o_f, o_b, r, ya, *consts)


def _ffn_kernel(h_ref, mod_ref, ng_ref, wg_ref, wu_ref, wd_ref, out_ref):
    x = h_ref[...]
    m = (_rms(x, ng_ref[...]) * (1.0 + mod_ref[4]) + mod_ref[3]).astype(BF16)
    acc = jnp.zeros(x.shape, F32)
    for j in range(wg_ref.shape[0]):
        gate = jnp.dot(m, wg_ref[j], preferred_element_type=F32)
        up = jnp.dot(m, wu_ref[j], preferred_element_type=F32)
        act = (_silu(gate) * up).astype(BF16)
        acc = acc + jnp.dot(act, wd_ref[j], preferred_element_type=F32)
    out_ref[...] = x + mod_ref[5] * acc


def _ffn(h, mod, ng, wts):
    tm = TM
    consts = [ng, wts["ff_g"], wts["ff_u"], wts["ff_d"]]
    return pl.pallas_call(
        _ffn_kernel,
        out_shape=jax.ShapeDtypeStruct((N_ROWS, D_MODEL), F32),
        grid=(N_ROWS // tm,),
        in_specs=[_rows(tm, D_MODEL), _mod_spec(tm)] + [_const(a.shape) for a in consts],
        out_specs=_rows(tm, D_MODEL),
        input_output_aliases={0: 0},
        compiler_params=_params("arbitrary"),
        name="ffn",
    )(h, mod, *consts)


def _odd_in_kernel(h_ref, mod_ref, ng_ref, win_ref, qg_ref, kvg_ref, wq_ref, wqs_ref, wk_ref, wv_ref,
                   place_ref, vone_ref, cq_ref, sq_ref, ck_ref, sk_ref, q_ref, k_ref, v_ref):
    n = _rms(h_ref[...], ng_ref[...]) * (1.0 + mod_ref[1]) + mod_ref[0]
    p = jnp.dot(n.astype(BF16), win_ref[...], preferred_element_type=F32)
    cq = _rms(p[:, :C_Q_RANK], qg_ref[...]).astype(BF16)
    ckv = _rms(p[:, C_Q_RANK:C_Q_RANK + C_KV_RANK], kvg_ref[...]).astype(BF16)
    base = C_Q_RANK + C_KV_RANK
    kr = p[:, base:base + LANE] * ck_ref[...] + p[:, base + LANE:base + 2 * LANE] * sk_ref[...]

    q_plain = jnp.dot(cq, wq_ref[...], preferred_element_type=F32)
    q_swap = jnp.dot(cq, wqs_ref[...], preferred_element_type=F32)
    cq_t, sq_t = cq_ref[...], sq_ref[...]
    for h in range(C_HEADS):
        cols = slice(h * HEAD_PAD, (h + 1) * HEAD_PAD)
        q_ref[:, cols] = ((q_plain[:, cols] * cq_t + q_swap[:, cols] * sq_t) * (C_SCALE * LOG2_E)).astype(q_ref.dtype)
    k_cat = jnp.dot(ckv, wk_ref[...], preferred_element_type=F32) + _mm(kr, place_ref[...])
    k_ref[...] = k_cat.astype(k_ref.dtype)
    v_ref[...] = (jnp.dot(ckv, wv_ref[...], preferred_element_type=F32) + vone_ref[...]).astype(v_ref.dtype)


def _odd_in(h, mod, ng, wts, rope):
    tm = TM
    n_tiles = N_ROWS // tm
    consts = [ng, wts["w_in"], wts["q_norm_g"], wts["kv_norm_g"], wts["w_q"], wts["w_q_swap"],
              wts["w_k"], wts["w_v"], wts["place"], wts["v_one"]]
    ctx_tiles = NC_ROWS // tm
    lat_tiles = SEQ // tm

    def rope_spec():
        return pl.BlockSpec((tm, LANE), lambda i: (jnp.where(i < ctx_tiles, lat_tiles, (i - ctx_tiles) % lat_tiles), 0))

    return pl.pallas_call(
        _odd_in_kernel,
        out_shape=(jax.ShapeDtypeStruct((N_ROWS, C_QK_PAD), BF16),) * 3,
        grid=(n_tiles,),
        in_specs=[_rows(tm, D_MODEL), _mod_spec(tm)] + [_const(a.shape) for a in consts] + [rope_spec()] * 4,
        out_specs=(_rows(tm, C_QK_PAD),) * 3,
        compiler_params=_params("arbitrary"),
        name="odd_in",
    )(h, mod, *consts, *rope)


def _attn_kernel(*refs, n_pieces):
    q_ref = refs[0]
    k_refs = refs[1:1 + n_pieces]
    v_refs = refs[1 + n_pieces:1 + 2 * n_pieces]
    o_ref = refs[-1]
    for pair in range(ATT_HG // 2):
        acc = None
        for hh in range(2):
            cols = slice((2 * pair + hh) * HEAD_PAD, (2 * pair + hh + 1) * HEAD_PAD)
            q = q_ref[:, cols]
            s = [lax.dot_general(q, k[:, cols], (((1,), (1,)), ((), ())), preferred_element_type=F32)
                 for k in k_refs]
            m = functools.reduce(jnp.maximum, [jnp.max(x, axis=-1, keepdims=True) for x in s])
            e = [jnp.exp2(x - m) for x in s]
            o = functools.reduce(jnp.add, [jnp.dot(x.astype(BF16), v[:, cols], preferred_element_type=F32)
                                           for x, v in zip(e, v_refs)])
            ones_lane = C_VDIM if hh == 0 else 0
            l = o[:, ones_lane:ones_lane + 1]
            lane = lax.broadcasted_iota(jnp.int32, o.shape, 1)
            own = (lane < C_VDIM) if hh == 0 else (lane >= C_VDIM)
            o = jnp.where(own, o * (1.0 / l), 0.0)
            acc = o if acc is None else acc + o
        o_ref[:, pair * LANE:(pair + 1) * LANE] = acc.astype(o_ref.dtype)


def _attention(q, k, v, latent, prev=None):
    width = ATT_HG * HEAD_PAD
    n_hg = C_HEADS // ATT_HG
    if latent:
        tq = ATT_TQ
        nq = SEQ // tq
        q_spec = pl.BlockSpec((tq, width), lambda b, g, t: ((NC_ROWS + b * SEQ) // tq + t, g))
        kv_specs = [pl.BlockSpec((CTX_LEN, width), lambda b, g, t: (b, g)),
                    pl.BlockSpec((SEQ, width), lambda b, g, t: (NC_ROWS // SEQ + b, g))]
    else:
        tq = CTX_LEN
        nq = 1
        q_spec = pl.BlockSpec((tq, width), lambda b, g, t: (b, g))
        kv_specs = [pl.BlockSpec((CTX_LEN, width), lambda b, g, t: (b, g))]
    n_pieces = len(kv_specs)
    o_spec = pl.BlockSpec((tq, ATT_HG * C_VDIM), q_spec.index_map)
    ins = [q] + [k] * n_pieces + [v] * n_pieces
    in_specs = [q_spec] + kv_specs + kv_specs
    aliases = {}
    if prev is not None:
        aliases = {len(ins): 0}
        ins.append(prev)
        in_specs.append(pl.BlockSpec(memory_space=pl.ANY))
    return pl.pallas_call(
        functools.partial(_attn_kernel, n_pieces=n_pieces),
        out_shape=jax.ShapeDtypeStruct((N_ROWS, C_HEADS * C_VDIM), BF16),
        grid=(BATCH, n_hg, nq),
        in_specs=in_specs,
        out_specs=o_spec,
        input_output_aliases=aliases,
        compiler_params=_params("arbitrary", "arbitrary", "arbitrary"),
        name="attn_latent" if latent else "attn_context",
    )(*ins)


def _odd_out_kernel(h_ref, mod_ref, o_ref, w_ref, out_ref):
    y = jnp.dot(o_ref[...], w_ref[...], preferred_element_type=F32)
    out_ref[...] = h_ref[...] + mod_ref[2] * y


def _odd_out(h, mod, o, w_out, tile0, n_tiles):
    tm = TM
    return pl.pallas_call(
        _odd_out_kernel,
        out_shape=jax.ShapeDtypeStruct((N_ROWS, D_MODEL), F32),
        grid=(n_tiles,),
        in_specs=[_rows(tm, D_MODEL, tile0), _mod_spec(tm, tile0), _rows(tm, C_HEADS * C_VDIM, tile0),
                  _const(w_out.shape)],
        out_specs=_rows(tm, D_MODEL, tile0),
        input_output_aliases={0: 0},
        compiler_params=_params("arbitrary"),
        name="odd_out",
    )(h, mod, o, w_out)


def _route(m, router):
    m_hi, m_lo = _split_bf16(m)
    r_hi, r_lo = _split_bf16(router)
    dot = functools.partial(jnp.dot, preferred_element_type=F32)
    logits = dot(m_hi, r_hi) + (dot(m_hi, r_lo) + dot(m_lo, r_hi))
    lane = lax.broadcasted_iota(jnp.int32, logits.shape, 1)
    neg = -jnp.inf
    l1 = jnp.where(lane < N_EXPERTS, logits, neg)
    m1 = jnp.max(l1, axis=-1, keepdims=True)
    i1 = jnp.min(jnp.where(l1 == m1, lane, LANE), axis=-1, keepdims=True)
    l2 = jnp.where(lane == i1, neg, l1)
    m2 = jnp.max(l2, axis=-1, keepdims=True)
    i2 = jnp.min(jnp.where(l2 == m2, lane, LANE), axis=-1, keepdims=True)
    e2 = jnp.exp(m2 - m1)
    w1 = 1.0 / (1.0 + e2)
    w2 = e2 / (1.0 + e2)
    return i1, i2, w1, w2


R_E0, R_E1, R_RANK0, R_RANK1, R_W0, R_W1 = range(6)
R_ROWS = 8
F_VALID, F_FIRST, F_LAST = 1, 2, 4


def _moe_route_kernel(h_ref, mod_ref, ng_ref, rt_ref, m_ref, rcol_ref, rrow_ref, cnt_ref, carry_s):
    @pl.when(pl.program_id(0) == 0)
    def _():
        carry_s[...] = jnp.zeros_like(carry_s)

    m = _rms(h_ref[...], ng_ref[...]) * (1.0 + mod_ref[4]) + mod_ref[3]
    m_ref[...] = m.astype(m_ref.dtype)
    i1, i2, w1, w2 = _route(m, rt_ref[...])
    tm = m.shape[0]
    lane = lax.broadcasted_iota(jnp.int32, (tm, LANE), 1)
    picked = jnp.where(lane == i1, 1.0, jnp.where(lane == i2, 1.0, 0.0))
    ri = lax.broadcasted_iota(jnp.int32, (tm, tm), 0)
    ci = lax.broadcasted_iota(jnp.int32, (tm, tm), 1)
    earlier = jnp.where(ci < ri, 1.0, 0.0).astype(BF16)
    before = carry_s[0:1, :]
    rank_all = jnp.dot(earlier, picked.astype(BF16), preferred_element_type=F32) + before
    rank0 = jnp.sum(jnp.where(lane == i1, rank_all, 0.0), axis=-1, keepdims=True)
    rank1 = jnp.sum(jnp.where(lane == i2, rank_all, 0.0), axis=-1, keepdims=True)
    after = before + jnp.sum(picked, axis=0, keepdims=True)
    carry_s[...] = jnp.broadcast_to(after, carry_s.shape)
    sub = lax.broadcasted_iota(jnp.int32, cnt_ref.shape, 0)
    cnt_ref[...] = jnp.where(sub == 0, before, jnp.where(sub == 1, after, 0.0))

    fields = {R_E0: i1.astype(F32), R_E1: i2.astype(F32), R_RANK0: rank0, R_RANK1: rank1, R_W0: w1, R_W1: w2}
    rcol = jnp.zeros((tm, LANE), F32)
    for k, val in fields.items():
        rcol = jnp.where(lane == k, val, rcol)
    rcol_ref[...] = rcol
    rrow_ref[...] = jnp.transpose(rcol)[:R_ROWS, :]


def _moe_route(h, mod, ng, router, tile0, n_chunks):
    tm = MOE_TC
    rows = n_chunks * tm
    return pl.pallas_call(
        _moe_route_kernel,
        out_shape=(jax.ShapeDtypeStruct((rows, D_MODEL), BF16),
                   jax.ShapeDtypeStruct((rows, LANE), F32),
                   jax.ShapeDtypeStruct((R_ROWS, rows), F32),
                   jax.ShapeDtypeStruct((n_chunks, R_ROWS, LANE), F32)),
        grid=(n_chunks,),
        in_specs=[_rows(tm, D_MODEL, tile0), _mod_spec(tm, tile0), _const(ng.shape), _const(router.shape)],
        out_specs=(_rows(tm, D_MODEL), _rows(tm, LANE),
                   pl.BlockSpec((R_ROWS, tm), lambda i: (0, i)),
                   pl.BlockSpec((None, R_ROWS, LANE), lambda i: (i, 0, 0))),
        scratch_shapes=[pltpu.VMEM((R_ROWS, LANE), F32)],
        compiler_params=_params("arbitrary"),
        name="moe_route",
    )(h, mod, ng, router)


def _moe_plan(cnt, n_chunks, n_blk):
    i32 = jnp.int32
    before = cnt[:, 0, :N_EXPERTS].astype(i32)
    after = cnt[:, 1, :N_EXPERTS].astype(i32)
    padded = (after[-1] + MOE_XT - 1) // MOE_XT * MOE_XT
    ends = jnp.cumsum(padded).astype(i32)
    base = ends - padded
    used_rows = ends[-1]
    start = base[None, :] + before
    stop = base[None, :] + after
    lo = jnp.arange(n_blk, dtype=i32) * MOE_BLK
    overlap = (start[:, :, None] < lo + MOE_BLK) & (stop[:, :, None] > lo)
    inc = jnp.any(overlap, axis=1)
    fill = (lo < used_rows) & ~jnp.any(inc, axis=0)
    inc_fill = inc.at[0].set(inc[0] | fill)
    n_items = n_chunks * N_EXPERTS + n_blk

    def items(mat):
        idx = jnp.nonzero(mat.reshape(-1), size=n_items, fill_value=-1)[0].astype(i32)
        valid = idx >= 0
        n_valid = jnp.sum(valid).astype(i32)
        idx = jnp.where(valid, idx, idx[jnp.maximum(n_valid - 1, 0)])
        outer = idx // mat.shape[1]
        inner = idx % mat.shape[1]
        prev = jnp.concatenate([jnp.full((1,), -1, i32), outer[:-1]])
        nxt = jnp.concatenate([outer[1:], jnp.full((1,), -1, i32)])
        pos = jnp.arange(n_items, dtype=i32)
        first = valid & (outer != prev)
        last = valid & ((outer != nxt) | (pos == n_valid - 1))
        flags = valid.astype(i32) * F_VALID + first.astype(i32) * F_FIRST + last.astype(i32) * F_LAST
        return outer, inner, flags

    disp_blk, disp_chunk, disp_flags = items(inc_fill.T)
    comb_chunk, comb_blk, comb_flags = items(inc)
    n_xt = n_blk * MOE_BLK // MOE_XT
    t0 = jnp.arange(n_xt, dtype=i32) * MOE_XT
    tile_used = (t0 < used_rows).astype(i32)
    tile_src = jnp.minimum(jnp.arange(n_xt, dtype=i32), used_rows // MOE_XT - 1)
    tile_e = jnp.minimum(jnp.sum(t0[:, None] >= ends[None, :], axis=1), N_EXPERTS - 1).astype(i32)[tile_src]
    return {"base": base, "disp": (disp_chunk, disp_blk, disp_flags), "comb": (comb_chunk, comb_blk, comb_flags),
            "tiles": (tile_e, tile_src, tile_used)}


def _sorted_row(expert, rank, base_ref):
    start = jnp.zeros_like(expert)
    for e in range(N_EXPERTS):
        start = jnp.where(expert == float(e), base_ref[e].astype(F32), start)
    return start + rank


def _moe_dispatch_kernel(c_ref, d_ref, f_ref, base_ref, m_ref, rrow_ref, x_ref):
    n = pl.program_id(0)
    flag = f_ref[n]

    @pl.when((flag & F_VALID) != 0)
    def _():
        rr = rrow_ref[...]
        row0 = (d_ref[n] * MOE_BLK).astype(F32)
        rel0 = _sorted_row(rr[R_E0:R_E0 + 1], rr[R_RANK0:R_RANK0 + 1], base_ref) - row0
        rel1 = _sorted_row(rr[R_E1:R_E1 + 1], rr[R_RANK1:R_RANK1 + 1], base_ref) - row0
        ri = lax.broadcasted_iota(jnp.int32, (MOE_BLK, MOE_TC), 0).astype(F32)
        onehot = jnp.where(rel0 == ri, 1.0, jnp.where(rel1 == ri, 1.0, 0.0)).astype(BF16)
        rows = jnp.dot(onehot, m_ref[...], preferred_element_type=F32).astype(x_ref.dtype)

        @pl.when((flag & F_FIRST) != 0)
        def _():
            x_ref[...] = rows

        @pl.when((flag & F_FIRST) == 0)
        def _():
            x_ref[...] = x_ref[...] + rows


def _moe_dispatch(m, rrow, plan, n_blk):
    chunk, blk, flags = plan["disp"]
    n_items = chunk.shape[0]
    grid_spec = pltpu.PrefetchScalarGridSpec(
        num_scalar_prefetch=4,
        grid=(n_items,),
        in_specs=[pl.BlockSpec((MOE_TC, D_MODEL), lambda n, c, d, f, b: (c[n], 0)),
                  pl.BlockSpec((R_ROWS, MOE_TC), lambda n, c, d, f, b: (0, c[n]))],
        out_specs=pl.BlockSpec((MOE_BLK, D_MODEL), lambda n, c, d, f, b: (d[n], 0)),
    )
    return pl.pallas_call(
        _moe_dispatch_kernel,
        out_shape=jax.ShapeDtypeStruct((n_blk * MOE_BLK, D_MODEL), BF16),
        grid_spec=grid_spec,
        compiler_params=_params("arbitrary"),
        name="moe_dispatch",
    )(chunk, blk, flags, plan["base"], m, rrow)


def _moe_experts_kernel(e_ref, src_ref, used_ref, x_ref, wg_ref, wu_ref, wd_ref, y_ref, acc_s):
    t = pl.program_id(0)
    j = pl.program_id(1)
    last = j == pl.num_programs(1) - 1
    used = used_ref[t] != 0

    @pl.when(used)
    def _():
        x = x_ref[...]
        gate = jnp.dot(x, wg_ref[...].astype(BF16), preferred_element_type=F32)
        up = jnp.dot(x, wu_ref[...].astype(BF16), preferred_element_type=F32)
        act = (_silu(gate) * up).astype(BF16)
        part = jnp.dot(act, wd_ref[...].astype(BF16), preferred_element_type=F32)

        @pl.when(j == 0)
        def _():
            acc_s[...] = part

        @pl.when(j != 0)
        def _():
            acc_s[...] += part

        @pl.when(last)
        def _():
            y_ref[...] = acc_s[...].astype(y_ref.dtype)

    @pl.when(jnp.logical_not(used) & last)
    def _():
        y_ref[...] = jnp.zeros_like(y_ref)


def _moe_experts(x, plan, wts):
    tile_e, tile_src, tile_used = plan["tiles"]
    n_xt = tile_e.shape[0]
    li = wts["moe_layer"]
    th = MOE_TH
    nj = MOE_HIDDEN // th

    def hidden(t, j, used):
        return jnp.where(used[t] != 0, j, nj - 1)

    grid_spec = pltpu.PrefetchScalarGridSpec(
        num_scalar_prefetch=3,
        grid=(n_xt, nj),
        in_specs=[pl.BlockSpec((MOE_XT, D_MODEL), lambda t, j, e, s, u: (s[t], 0)),
                  pl.BlockSpec((None, None, D_MODEL, th), lambda t, j, e, s, u: (li, e[t], 0, hidden(t, j, u))),
                  pl.BlockSpec((None, None, D_MODEL, th), lambda t, j, e, s, u: (li, e[t], 0, hidden(t, j, u))),
                  pl.BlockSpec((None, None, th, D_MODEL), lambda t, j, e, s, u: (li, e[t], hidden(t, j, u), 0))],
        out_specs=pl.BlockSpec((MOE_XT, D_MODEL), lambda t, j, e, s, u: (t, 0)),
        scratch_shapes=[pltpu.VMEM((MOE_XT, D_MODEL), F32)],
    )
    return pl.pallas_call(
        _moe_experts_kernel,
        out_shape=jax.ShapeDtypeStruct((n_xt * MOE_XT, D_MODEL), BF16),
        grid_spec=grid_spec,
        compiler_params=_params("arbitrary", "arbitrary"),
        name="moe_experts",
    )(tile_e, tile_src, tile_used, x, wts["moe_g"], wts["moe_u"], wts["moe_d"])


def _moe_combine_kernel(c_ref, d_ref, f_ref, base_ref, rcol_ref, y_ref, h_ref, mod_ref, out_ref, acc_s):
    n = pl.program_id(0)
    flag = f_ref[n]

    @pl.when((flag & F_VALID) != 0)
    def _():
        rc = rcol_ref[...]
        row0 = (d_ref[n] * MOE_BLK).astype(F32)
        rel0 = _sorted_row(rc[:, R_E0:R_E0 + 1], rc[:, R_RANK0:R_RANK0 + 1], base_ref) - row0
        rel1 = _sorted_row(rc[:, R_E1:R_E1 + 1], rc[:, R_RANK1:R_RANK1 + 1], base_ref) - row0
        li = lax.broadcasted_iota(jnp.int32, (MOE_TC, MOE_BLK), 1).astype(F32)
        gates = jnp.where(rel0 == li, rc[:, R_W0:R_W0 + 1], jnp.where(rel1 == li, rc[:, R_W1:R_W1 + 1], 0.0))
        part = jnp.dot(gates.astype(BF16), y_ref[...], preferred_element_type=F32)

        @pl.when((flag & F_FIRST) != 0)
        def _():
            acc_s[...] = part

        @pl.when((flag & F_FIRST) == 0)
        def _():
            acc_s[...] += part

        @pl.when((flag & F_LAST) != 0)
        def _():
            out_ref[...] = h_ref[...] + mod_ref[5] * acc_s[...]


def _moe_combine(h, mod, rcol, y, plan, tile0):
    chunk, blk, flags = plan["comb"]
    n_items = chunk.shape[0]
    grid_spec = pltpu.PrefetchScalarGridSpec(
        num_scalar_prefetch=4,
        grid=(n_items,),
        in_specs=[pl.BlockSpec((MOE_TC, LANE), lambda n, c, d, f, b: (c[n], 0)),
                  pl.BlockSpec((MOE_BLK, D_MODEL), lambda n, c, d, f, b: (d[n], 0)),
                  pl.BlockSpec((MOE_TC, D_MODEL), lambda n, c, d, f, b: (c[n] + tile0, 0)),
                  pl.BlockSpec((N_MOD, None, 1, D_MODEL),
                               lambda n, c, d, f, b: (0, _mod_row(c[n] + tile0, MOE_TC), 0, 0))],
        out_specs=pl.BlockSpec((MOE_TC, D_MODEL), lambda n, c, d, f, b: (c[n] + tile0, 0)),
        scratch_shapes=[pltpu.VMEM((MOE_TC, D_MODEL), F32)],
    )
    return pl.pallas_call(
        _moe_combine_kernel,
        out_shape=jax.ShapeDtypeStruct((N_ROWS, D_MODEL), F32),
        grid_spec=grid_spec,
        input_output_aliases={6: 0},
        compiler_params=_params("arbitrary"),
        name="moe_combine",
    )(chunk, blk, flags, plan["base"], rcol, y, h, mod)


def _moe(h, mod, ng, wts, first_row):
    tile0 = first_row // MOE_TC
    n_chunks = (N_ROWS - first_row) // MOE_TC
    n_blk = (2 * n_chunks * MOE_TC + N_EXPERTS * MOE_XT) // MOE_BLK
    m, rcol, rrow, cnt = _moe_route(h, mod, ng, wts["router"], tile0, n_chunks)
    plan = _moe_plan(cnt, n_chunks, n_blk)
    x = _moe_dispatch(m, rrow, plan, n_blk)
    y = _moe_experts(x, plan, wts)
    return _moe_combine(h, mod, rcol, y, plan, tile0)


def _final_kernel(h_ref, g_ref, o_ref):
    o_ref[...] = _rms(h_ref[...], g_ref[...])


def _final_norm(h, g):
    tm = TM
    return pl.pallas_call(
        _final_kernel,
        out_shape=jax.ShapeDtypeStruct((NL_ROWS, D_MODEL), F32),
        grid=(NL_ROWS // tm,),
        in_specs=[_rows(tm, D_MODEL, NC_ROWS // tm), _const(g.shape)],
        out_specs=_rows(tm, D_MODEL),
        compiler_params=_params("arbitrary"),
        name="final_norm",
    )(h, g)


def _row_vec(x):
    return x.reshape(1, -1).astype(F32)


def _even_weights(i, ev_w_in, a_ln_g, a_ln_b, a_ws, a_bs, b_gate_w, b_gate_b, b_norm_g, ev_w_out,
                  ff_w_gate, ff_w_up, ff_w_down):
    w_in = ev_w_in[i]
    qkvr = 2 * B_QK + 2 * B_V
    gate_w = jnp.zeros((2 * B_GATE_RANK, 2 * B_QK), F32)
    gate_w = gate_w.at[:B_GATE_RANK, :B_QK].set(b_gate_w[i, 0]).at[B_GATE_RANK:, B_QK:].set(b_gate_w[i, 1])
    nch = FF_HIDDEN // FF_TH
    return {
        "w_a": w_in[:, :2 * A_WIDTH].astype(BF16),
        "w_q": w_in[:, 2 * A_WIDTH:2 * A_WIDTH + qkvr].astype(BF16),
        "w_gf": w_in[:, 2 * A_WIDTH + qkvr:].astype(BF16),
        "gate_w": gate_w.astype(BF16),
        "gate_b": b_gate_b[i].reshape(1, 2 * B_QK).astype(F32),
        "ln_g": _row_vec(a_ln_g[i]),
        "ln_b": _row_vec(a_ln_b[i]),
        "ws": a_ws[i].astype(BF16),
        "bs": jnp.broadcast_to(a_bs[i][:, :, None], (A_HEADS, A_CHUNK, A_HEAD_DIM)).astype(F32),
        "b_norm_g": _row_vec(b_norm_g[i]),
        "w_out_a": ev_w_out[i, :A_WIDTH].astype(BF16),
        "w_out_b": ev_w_out[i, A_WIDTH:].astype(BF16),
        "ff_g": jnp.transpose(ff_w_gate[i].reshape(D_MODEL, nch, FF_TH), (1, 0, 2)).astype(BF16),
        "ff_u": jnp.transpose(ff_w_up[i].reshape(D_MODEL, nch, FF_TH), (1, 0, 2)).astype(BF16),
        "ff_d": ff_w_down[i].reshape(nch, FF_TH, D_MODEL).astype(BF16),
    }


def _rope_swap_index():
    quarter = C_ROPE // 4
    idx = np.arange(C_ROPE)
    return np.concatenate([idx[quarter:2 * quarter], idx[:quarter], idx[3 * quarter:], idx[2 * quarter:3 * quarter]])


def _odd_weights(i, od_w_in, c_q_norm_g, c_w_uq, c_kv_norm_g, c_w_ukv, od_w_out,
                 moe_router, moe_w_gate, moe_w_up, moe_w_down):
    swap = _rope_swap_index()
    w_in = od_w_in[i]
    low = C_Q_RANK + C_KV_RANK
    k_rope_w = w_in[:, low:]
    pad = jnp.zeros((D_MODEL, LANE - C_ROPE), F32)
    w_in_x = jnp.concatenate([w_in[:, :low], k_rope_w, pad, k_rope_w[:, swap], pad], axis=1)

    uq = c_w_uq[i].reshape(C_Q_RANK, C_HEADS, C_NOPE + C_ROPE)
    zq = jnp.zeros((C_Q_RANK, C_HEADS, HEAD_PAD - C_NOPE - C_ROPE), F32)
    w_q = jnp.concatenate([uq, zq], axis=-1)
    w_q_swap = jnp.concatenate([jnp.zeros((C_Q_RANK, C_HEADS, C_NOPE), F32), uq[:, :, C_NOPE:][:, :, swap], zq], axis=-1)
    ukv = c_w_ukv[i].reshape(C_KV_RANK, C_HEADS, C_NOPE + C_VDIM)
    w_k = jnp.concatenate([ukv[:, :, :C_NOPE], jnp.zeros((C_KV_RANK, C_HEADS, HEAD_PAD - C_NOPE), F32)], axis=-1)
    zv = jnp.zeros((C_KV_RANK, C_HEADS // 2, C_VDIM), F32)
    uv = ukv[:, :, C_NOPE:].reshape(C_KV_RANK, C_HEADS // 2, 2, C_VDIM)
    w_v = jnp.stack([jnp.concatenate([uv[:, :, 0], zv], axis=-1),
                     jnp.concatenate([zv, uv[:, :, 1]], axis=-1)], axis=2)
    v_one = np.zeros((C_HEADS // 2, 2, HEAD_PAD), np.float32)
    v_one[:, 0, C_VDIM] = 1.0
    v_one[:, 1, 0] = 1.0
    place = np.zeros((LANE, C_HEADS, HEAD_PAD), np.float32)
    for r in range(C_ROPE):
        place[r, :, C_NOPE + r] = 1.0
    router = jnp.concatenate([moe_router[i], jnp.zeros((D_MODEL, LANE - N_EXPERTS), F32)], axis=1)
    return {
        "w_in": w_in_x.astype(BF16),
        "q_norm_g": _row_vec(c_q_norm_g[i]),
        "kv_norm_g": _row_vec(c_kv_norm_g[i]),
        "w_q": w_q.reshape(C_Q_RANK, C_QK_PAD).astype(BF16),
        "w_q_swap": w_q_swap.reshape(C_Q_RANK, C_QK_PAD).astype(BF16),
        "w_k": w_k.reshape(C_KV_RANK, C_QK_PAD).astype(BF16),
        "w_v": w_v.reshape(C_KV_RANK, C_QK_PAD).astype(BF16),
        "place": jnp.asarray(place.reshape(LANE, C_QK_PAD), BF16),
        "v_one": jnp.asarray(v_one.reshape(1, C_QK_PAD), F32),
        "w_out": od_w_out[i].astype(BF16),
        "router": router,
        "moe_layer": i,
        "moe_g": moe_w_gate,
        "moe_u": moe_w_up,
        "moe_d": moe_w_down,
    }


def _rope_tables(tm):
    quarter = C_ROPE // 4
    half = C_ROPE // 2
    inv_freq = ROPE_BASE ** (-jnp.arange(0, half, 2, dtype=F32) / half)
    pos = jnp.arange(SEQ, dtype=jnp.int32)
    ang_r = (pos // GRID_W).astype(F32)[:, None] * inv_freq
    ang_c = (pos % GRID_W).astype(F32)[:, None] * inv_freq
    cos = jnp.concatenate([jnp.cos(ang_r)] * 2 + [jnp.cos(ang_c)] * 2, axis=1)
    sin = jnp.concatenate([-jnp.sin(ang_r), jnp.sin(ang_r), -jnp.sin(ang_c), jnp.sin(ang_c)], axis=1)
    cos = jnp.concatenate([cos, jnp.ones((tm, C_ROPE), F32)], axis=0)
    sin = jnp.concatenate([sin, jnp.zeros((tm, C_ROPE), F32)], axis=0)
    rows = SEQ + tm
    ones = jnp.ones((rows, C_NOPE), F32)
    z = lambda w: jnp.zeros((rows, w), F32)
    cq = jnp.concatenate([ones, cos, z(HEAD_PAD - C_NOPE - C_ROPE)], axis=1)
    sq = jnp.concatenate([z(C_NOPE), sin, z(HEAD_PAD - C_NOPE - C_ROPE)], axis=1)
    ck = jnp.concatenate([cos, z(LANE - C_ROPE)], axis=1)
    sk = jnp.concatenate([sin, z(LANE - C_ROPE)], axis=1)
    return cq, sq, ck, sk


def kernel(x, c, ctx, c_ctx, mod_w, mod_b, norm_mix_g, norm_ffn_g, final_g, ev_w_in, a_ln_g, a_ln_b, a_ws, a_bs, b_gate_w, b_gate_b, b_norm_g, ev_w_out, od_w_in, c_q_norm_g, c_w_uq, c_kv_norm_g, c_w_ukv, od_w_out, ff_w_gate, ff_w_up, ff_w_down, moe_router, moe_w_gate, moe_w_up, moe_w_down):
    h = jnp.concatenate([ctx.reshape(NC_ROWS, D_MODEL), x.reshape(NL_ROWS, D_MODEL)], axis=0)
    cond = jnp.concatenate([c, c_ctx[None, :], jnp.zeros((MOD_PAD - MOD_ROWS, D_MODEL), F32)], axis=0)
    mod = _modulation(cond, mod_w, mod_b)
    rope = _rope_tables(TM)

    for layer in range(DEPTH):
        i = layer // 2
        need_ctx = layer < DEPTH - 1
        ng_mix = _row_vec(norm_mix_g[layer])
        ng_ffn = _row_vec(norm_ffn_g[layer])
        if layer % 2 == 0:
            wts = _even_weights(i, ev_w_in, a_ln_g, a_ln_b, a_ws, a_bs, b_gate_w, b_gate_b, b_norm_g,
                                ev_w_out, ff_w_gate, ff_w_up, ff_w_down)
            ya, qkv, r, g = _even_in(h, mod[layer], ng_mix, wts)
            o_f, o_b = _gla(qkv, g)
            h = _even_out(h, mod[layer], o_f, o_b, r, ya, wts)
            h = _ffn(h, mod[layer], ng_ffn, wts)
        else:
            wts = _odd_weights(i, od_w_in, c_q_norm_g, c_w_uq, c_kv_norm_g, c_w_ukv, od_w_out,
                               moe_router, moe_w_gate, moe_w_up, moe_w_down)
            q, k, v = _odd_in(h, mod[layer], ng_mix, wts, rope)
            o = _attention(q, k, v, latent=True)
            if need_ctx:
                o = _attention(q, k, v, latent=False, prev=o)
            first = 0 if need_ctx else NC_ROWS
            h = _odd_out(h, mod[layer], o, wts["w_out"], first // TM, (N_ROWS - first) // TM)
            h = _moe(h, mod[layer], ng_ffn, wts, first)
    out = _final_norm(h, _row_vec(final_g))
    return out.reshape(BATCH, SEQ, D_MODEL)
```

```python
import functools

import numpy as np
import jax
import jax.numpy as jnp
from jax import lax
from jax.experimental import pallas as pl
from jax.experimental.pallas import tpu as pltpu

D_MODEL = 1024
BATCH = 8
SEQ = 2048
DEPTH = 4
GRID_W = 64
CTX_LEN = 256
EPS = 1e-6
N_MOD = 6

A_HEADS = 4
A_HEAD_DIM = 128
A_CHUNK = 128
A_WIDTH = A_HEADS * A_HEAD_DIM

B_HEADS = 4
B_DK = 64
B_DV = 128
B_QK = B_HEADS * B_DK
B_V = B_HEADS * B_DV
B_GATE_RANK = 16
B_GATE_TEMP = 16.0
B_CHUNK = 64

C_HEADS = 16
C_NOPE = 64
C_ROPE = 32
C_VDIM = 64
C_Q_RANK = 256
C_KV_RANK = 256
C_SCALE = (C_NOPE + C_ROPE) ** -0.5
LOG2_E = 1.4426950408889634
ROPE_BASE = 10000.0

FF_HIDDEN = 2816
N_EXPERTS = 8
MOE_HIDDEN = 3584

NC_ROWS = BATCH * CTX_LEN
NL_ROWS = BATCH * SEQ
N_ROWS = NC_ROWS + NL_ROWS
MOD_ROWS = BATCH + 1
MOD_PAD = 16

LANE = 128
HEAD_PAD = 128
C_QK_PAD = C_HEADS * HEAD_PAD

TM = 512
GLA_BLK = 256
ATT_TQ = 512
ATT_HG = 4
FF_TH = 256
MOE_TC = 512
MOE_BLK = 256
MOE_XT = 1024
MOE_TH = 512
VMEM_LIMIT = 56 * 1024 * 1024

F32 = jnp.float32
BF16 = jnp.bfloat16


def _mm(a, b):
    return jnp.dot(a.astype(BF16), b.astype(BF16), preferred_element_type=F32)


def _mm_nt(a, b):
    return lax.dot_general(a.astype(BF16), b.astype(BF16), (((1,), (1,)), ((), ())),
                           preferred_element_type=F32)


def _mm_tn(a, b):
    return lax.dot_general(a.astype(BF16), b.astype(BF16), (((0,), (0,)), ((), ())),
                           preferred_element_type=F32)


def _sigmoid(x):
    return 1.0 / (1.0 + jnp.exp(-x))


def _silu(x):
    return x * _sigmoid(x)


def _gelu(x):
    return 0.5 * x * (1.0 + jnp.tanh(np.sqrt(2.0 / np.pi) * (x + 0.044715 * (x * x * x))))


def _rms(x, g):
    return x * lax.rsqrt(jnp.mean(x * x, axis=-1, keepdims=True) + EPS) * g


def _split_bf16(x):
    hi = x.astype(BF16)
    lo = (x - hi.astype(F32)).astype(BF16)
    return hi, lo


def _rows(tm, width, tile0=0, col=0):
    return pl.BlockSpec((tm, width), lambda i: (i + tile0, col))


def _const(shape):
    zeros = (0,) * len(shape)
    return pl.BlockSpec(shape, lambda *_: zeros, pipeline_mode=pl.Buffered(1))


def _mod_row(tile, tm):
    return jnp.where(tile < NC_ROWS // tm, BATCH, (tile * tm - NC_ROWS) // SEQ)


def _mod_spec(tm, tile0=0):
    return pl.BlockSpec((N_MOD, None, 1, D_MODEL), lambda i: (0, _mod_row(i + tile0, tm), 0, 0))


def _params(*sem):
    return pltpu.CompilerParams(dimension_semantics=sem, vmem_limit_bytes=VMEM_LIMIT)


def _mod_kernel(c_ref, w_ref, b_ref, o_ref):
    o_ref[...] = _mm(_silu(c_ref[...]), w_ref[...]) + b_ref[...]


def _modulation(cond, mod_w, mod_b):
    tn = 1536
    out = pl.pallas_call(
        _mod_kernel,
        out_shape=jax.ShapeDtypeStruct((DEPTH, MOD_PAD, N_MOD * D_MODEL), F32),
        grid=(DEPTH, N_MOD * D_MODEL // tn),
        in_specs=[
            pl.BlockSpec((MOD_PAD, D_MODEL), lambda l, j: (0, 0)),
            pl.BlockSpec((None, D_MODEL, tn), lambda l, j: (l, 0, j)),
            pl.BlockSpec((None, 1, tn), lambda l, j: (l, 0, j)),
        ],
        out_specs=pl.BlockSpec((None, MOD_PAD, tn), lambda l, j: (l, 0, j)),
        compiler_params=_params("arbitrary", "arbitrary"),
        name="modulation",
    )(cond, mod_w, mod_b.reshape(DEPTH, 1, N_MOD * D_MODEL))
    out = out[:, :MOD_ROWS].reshape(DEPTH, MOD_ROWS, N_MOD, 1, D_MODEL)
    return jnp.transpose(out, (0, 2, 1, 3, 4))


def _even_in_kernel(h_ref, mod_ref, ng_ref, wa_ref, wq_ref, wgf_ref, gw_ref, gb_ref,
                    lng_ref, lnb_ref, ws_ref, bs_ref, ya_ref, qkv_ref, r_ref, g_ref):
    tm = h_ref.shape[0]
    n = _rms(h_ref[...], ng_ref[...]) * (1.0 + mod_ref[1]) + mod_ref[0]
    nb = n.astype(BF16)

    ua_va = jnp.dot(nb, wa_ref[...], preferred_element_type=F32)
    u = _gelu(ua_va[:, :A_WIDTH])
    v = _gelu(ua_va[:, A_WIDTH:])
    mu = jnp.mean(v, axis=-1, keepdims=True)
    vc = v - mu
    var = jnp.mean(vc * vc, axis=-1, keepdims=True)
    v = (vc * lax.rsqrt(var + EPS) * lng_ref[...] + lnb_ref[...]).astype(BF16)
    for c in range(tm // A_CHUNK):
        rows = slice(c * A_CHUNK, (c + 1) * A_CHUNK)
        for h in range(A_HEADS):
            cols = slice(h * A_HEAD_DIM, (h + 1) * A_HEAD_DIM)
            mixed = jnp.dot(ws_ref[h], v[rows, cols], preferred_element_type=F32) + bs_ref[h]
            ya_ref[rows, cols] = (u[rows, cols] * mixed).astype(ya_ref.dtype)

    p = jnp.dot(nb, wq_ref[...], preferred_element_type=F32)
    qkv_ref[...] = p[:, :2 * B_QK + B_V].astype(qkv_ref.dtype)
    r_ref[...] = p[:, 2 * B_QK + B_V:].astype(r_ref.dtype)
    gf = jnp.dot(nb, wgf_ref[...], preferred_element_type=F32)
    logit = _mm(gf, gw_ref[...]) + gb_ref[...]
    log_sig = jnp.minimum(logit, 0.0) - jnp.log(1.0 + jnp.exp(-jnp.abs(logit)))
    g_ref[...] = log_sig / B_GATE_TEMP


def _even_in(h, mod, ng, wts):
    tm = TM
    n_tiles = N_ROWS // tm
    qkv_w = 2 * B_QK + B_V
    consts = [ng, wts["w_a"], wts["w_q"], wts["w_gf"], wts["gate_w"], wts["gate_b"],
              wts["ln_g"], wts["ln_b"], wts["ws"], wts["bs"]]
    return pl.pallas_call(
        _even_in_kernel,
        out_shape=(jax.ShapeDtypeStruct((N_ROWS, A_WIDTH), BF16),
                   jax.ShapeDtypeStruct((N_ROWS, qkv_w), BF16),
                   jax.ShapeDtypeStruct((N_ROWS, B_V), BF16),
                   jax.ShapeDtypeStruct((N_ROWS, 2 * B_QK), F32)),
        grid=(n_tiles,),
        in_specs=[_rows(tm, D_MODEL), _mod_spec(tm)] + [_const(a.shape) for a in consts],
        out_specs=(_rows(tm, A_WIDTH), _rows(tm, qkv_w), _rows(tm, B_V), _rows(tm, 2 * B_QK)),
        compiler_params=_params("arbitrary"),
        name="even_in",
    )(h, mod, *consts)


def _gla_direction(q, k, v, g, st_ref, reverse):
    n = GLA_BLK
    nch = n // B_CHUNK
    ri = lax.broadcasted_iota(jnp.int32, (n, n), 0)
    ci = lax.broadcasted_iota(jnp.int32, (n, n), 1)
    same = (ri // B_CHUNK) == (ci // B_CHUNK)
    tri = same & ((ci >= ri) if reverse else (ci <= ri))
    tri_b = jnp.where(tri, 1.0, 0.0).astype(BF16)
    same_b = jnp.where(same, 1.0, 0.0).astype(BF16)

    g_hi, g_lo = _split_bf16(g)
    dot = functools.partial(jnp.dot, preferred_element_type=F32)
    b = dot(tri_b, g_hi) + dot(tri_b, g_lo)
    b_tot = dot(same_b, g_hi) + dot(same_b, g_lo)
    q_dec = (q * (B_DK ** -0.5)) * jnp.exp(b)
    k_inv = k * jnp.exp(-b)
    k_dec = k * jnp.exp(b_tot - b)
    decay = jnp.exp(b_tot)

    lane = lax.broadcasted_iota(jnp.int32, (1, LANE), 1)
    even_lane = lane < B_DK

    outs = [None] * B_HEADS
    order = list(range(nch))[::-1] if reverse else list(range(nch))
    for p in range(B_HEADS // 2):
        lanes = slice(p * LANE, (p + 1) * LANE)
        qp = q_dec[:, lanes]
        q_heads = (jnp.where(even_lane, qp, 0.0).astype(BF16), jnp.where(even_lane, 0.0, qp).astype(BF16))
        k_inv_p = k_inv[:, lanes].astype(BF16)
        k_dec_p = k_dec[:, lanes].astype(BF16)
        intra = []
        for hh in range(2):
            h = 2 * p + hh
            scores = _mm_nt(q_heads[hh], k_inv_p)
            scores = jnp.where(tri, scores, 0.0)
            intra.append(_mm(scores, v[:, h * B_DV:(h + 1) * B_DV]))
        st = st_ref[p]
        inter = [[None] * nch for _ in range(2)]
        for c in order:
            rows = slice(c * B_CHUNK, (c + 1) * B_CHUNK)
            st_b = st.astype(BF16)
            for hh in range(2):
                inter[hh][c] = _mm_nt(q_heads[hh][rows], st_b)
            upd_e = _mm_tn(v[rows, (2 * p) * B_DV:(2 * p + 1) * B_DV], k_dec_p[rows])
            upd_o = _mm_tn(v[rows, (2 * p + 1) * B_DV:(2 * p + 2) * B_DV], k_dec_p[rows])
            st = st * decay[c * B_CHUNK:c * B_CHUNK + 1, lanes] + jnp.where(even_lane, upd_e, upd_o)
        st_ref[p] = st
        for hh in range(2):
            outs[2 * p + hh] = intra[hh] + jnp.concatenate(inter[hh], axis=0)
    return outs


def _gla_kernel(qf_ref, kf_ref, vf_ref, gf_ref, qb_ref, kb_ref, vb_ref, gb_ref,
                of_ref, ob_ref, stf_ref, stb_ref):
    @pl.when(pl.program_id(1) == 0)
    def _():
        stf_ref[...] = jnp.zeros_like(stf_ref)
        stb_ref[...] = jnp.zeros_like(stb_ref)

    outs = _gla_direction(qf_ref[...].astype(F32), kf_ref[...].astype(F32), vf_ref[...], gf_ref[...], stf_ref, False)
    for h in range(B_HEADS):
        of_ref[:, h * B_DV:(h + 1) * B_DV] = outs[h]
    outs = _gla_direction(qb_ref[...].astype(F32), kb_ref[...].astype(F32), vb_ref[...], gb_ref[...], stb_ref, True)
    for h in range(B_HEADS):
        ob_ref[:, h * B_DV:(h + 1) * B_DV] = outs[h]


def _gla(qkv, g):
    nb_ctx = CTX_LEN // GLA_BLK
    nb_lat = SEQ // GLA_BLK
    nblk = nb_ctx + nb_lat
    ctx_blocks = NC_ROWS // GLA_BLK

    def fwd_row(b, j):
        return jnp.where(j < nb_ctx, b * nb_ctx + j, ctx_blocks + b * nb_lat + (j - nb_ctx))

    def bwd_row(b, j):
        return jnp.where(j < nb_ctx, b * nb_ctx + (nb_ctx - 1 - j),
                         ctx_blocks + b * nb_lat + (nblk - 1 - j))

    def spec(width, col, row):
        return pl.BlockSpec((GLA_BLK, width), lambda b, j: (row(b, j), col))

    ins, in_specs = [], []
    for d, row in enumerate((fwd_row, bwd_row)):
        ins += [qkv, qkv, qkv, g]
        in_specs += [spec(B_QK, 0, row), spec(B_QK, 1, row), spec(B_V, 1, row), spec(B_QK, d, row)]
    return pl.pallas_call(
        _gla_kernel,
        out_shape=(jax.ShapeDtypeStruct((N_ROWS, B_V), F32),) * 2,
        grid=(BATCH, nblk),
        in_specs=in_specs,
        out_specs=(spec(B_V, 0, fwd_row), spec(B_V, 0, bwd_row)),
        scratch_shapes=[pltpu.VMEM((B_HEADS // 2, B_DV, LANE), F32)] * 2,
        compiler_params=_params("arbitrary", "arbitrary"),
        name="gla",
    )(*ins)


def _even_out_kernel(h_ref, mod_ref, of_ref, ob_ref, r_ref, ya_ref, bng_ref, woa_ref, wob_ref, out_ref):
    o = of_ref[...] + ob_ref[...]
    parts = [_rms(o[:, h * B_DV:(h + 1) * B_DV], bng_ref[...]) for h in range(B_HEADS)]
    yb = _silu(r_ref[...].astype(F32)) * jnp.concatenate(parts, axis=-1)
    y = jnp.dot(ya_ref[...], woa_ref[...], preferred_element_type=F32) + _mm(yb, wob_ref[...])
    out_ref[...] = h_ref[...] + mod_ref[2] * y


def _even_out(h, mod, o_f, o_b, r, ya, wts):
    tm = TM
    consts = [wts["b_norm_g"], wts["w_out_a"], wts["w_out_b"]]
    return pl.pallas_call(
        _even_out_kernel,
        out_shape=jax.ShapeDtypeStruct((N_ROWS, D_MODEL), F32),
        grid=(N_ROWS // tm,),
        in_specs=[_rows(tm, D_MODEL), _mod_spec(tm), _rows(tm, B_V), _rows(tm, B_V), _rows(tm, B_V),
                  _rows(tm, A_WIDTH)] + [_const(a.shape) for a in consts],
        out_specs=_rows(tm, D_MODEL),
        input_output_aliases={0: 0},
        compiler_params=_params("arbitrary"),
        name="even_out",
    )(h, mod, o_f, o_b, r, ya, *consts)


def _ffn_kernel(h_ref, mod_ref, ng_ref, wg_ref, wu_ref, wd_ref, out_ref):
    x = h_ref[...]
    m = (_rms(x, ng_ref[...]) * (1.0 + mod_ref[4]) + mod_ref[3]).astype(BF16)
    acc = jnp.zeros(x.shape, F32)
    for j in range(wg_ref.shape[0]):
        gate = jnp.dot(m, wg_ref[j], preferred_element_type=F32)
        up = jnp.dot(m, wu_ref[j], preferred_element_type=F32)
        act = (_silu(gate) * up).astype(BF16)
        acc = acc + jnp.dot(act, wd_ref[j], preferred_element_type=F32)
    out_ref[...] = x + mod_ref[5] * acc


def _ffn(h, mod, ng, wts):
    tm = TM
    consts = [ng, wts["ff_g"], wts["ff_u"], wts["ff_d"]]
    return pl.pallas_call(
        _ffn_kernel,
        out_shape=jax.ShapeDtypeStruct((N_ROWS, D_MODEL), F32),
        grid=(N_ROWS // tm,),
        in_specs=[_rows(tm, D_MODEL), _mod_spec(tm)] + [_const(a.shape) for a in consts],
        out_specs=_rows(tm, D_MODEL),
        input_output_aliases={0: 0},
        compiler_params=_params("arbitrary"),
        name="ffn",
    )(h, mod, *consts)


def _odd_in_kernel(h_ref, mod_ref, ng_ref, win_ref, qg_ref, kvg_ref, wq_ref, wqs_ref, wk_ref, wv_ref,
                   place_ref, vone_ref, cq_ref, sq_ref, ck_ref, sk_ref, q_ref, k_ref, v_ref):
    n = _rms(h_ref[...], ng_ref[...]) * (1.0 + mod_ref[1]) + mod_ref[0]
    p = jnp.dot(n.astype(BF16), win_ref[...], preferred_element_type=F32)
    cq = _rms(p[:, :C_Q_RANK], qg_ref[...]).astype(BF16)
    ckv = _rms(p[:, C_Q_RANK:C_Q_RANK + C_KV_RANK], kvg_ref[...]).astype(BF16)
    base = C_Q_RANK + C_KV_RANK
    kr = p[:, base:base + LANE] * ck_ref[...] + p[:, base + LANE:base + 2 * LANE] * sk_ref[...]

    q_plain = jnp.dot(cq, wq_ref[...], preferred_element_type=F32)
    q_swap = jnp.dot(cq, wqs_ref[...], preferred_element_type=F32)
    cq_t, sq_t = cq_ref[...], sq_ref[...]
    for h in range(C_HEADS):
        cols = slice(h * HEAD_PAD, (h + 1) * HEAD_PAD)
        q_ref[:, cols] = ((q_plain[:, cols] * cq_t + q_swap[:, cols] * sq_t) * (C_SCALE * LOG2_E)).astype(q_ref.dtype)
    k_cat = jnp.dot(ckv, wk_ref[...], preferred_element_type=F32) + _mm(kr, place_ref[...])
    k_ref[...] = k_cat.astype(k_ref.dtype)
    v_ref[...] = (jnp.dot(ckv, wv_ref[...], preferred_element_type=F32) + vone_ref[...]).astype(v_ref.dtype)


def _odd_in(h, mod, ng, wts, rope):
    tm = TM
    n_tiles = N_ROWS // tm
    consts = [ng, wts["w_in"], wts["q_norm_g"], wts["kv_norm_g"], wts["w_q"], wts["w_q_swap"],
              wts["w_k"], wts["w_v"], wts["place"], wts["v_one"]]
    ctx_tiles = NC_ROWS // tm
    lat_tiles = SEQ // tm

    def rope_spec():
        return pl.BlockSpec((tm, LANE), lambda i: (jnp.where(i < ctx_tiles, lat_tiles, (i - ctx_tiles) % lat_tiles), 0))

    return pl.pallas_call(
        _odd_in_kernel,
        out_shape=(jax.ShapeDtypeStruct((N_ROWS, C_QK_PAD), BF16),) * 3,
        grid=(n_tiles,),
        in_specs=[_rows(tm, D_MODEL), _mod_spec(tm)] + [_const(a.shape) for a in consts] + [rope_spec()] * 4,
        out_specs=(_rows(tm, C_QK_PAD),) * 3,
        compiler_params=_params("arbitrary"),
        name="odd_in",
    )(h, mod, *consts, *rope)


def _attn_kernel(*refs, n_pieces):
    q_ref = refs[0]
    k_refs = refs[1:1 + n_pieces]
    v_refs = refs[1 + n_pieces:1 + 2 * n_pieces]
    o_ref = refs[-1]
    for pair in range(ATT_HG // 2):
        acc = None
        for hh in range(2):
            cols = slice((2 * pair + hh) * HEAD_PAD, (2 * pair + hh + 1) * HEAD_PAD)
            q = q_ref[:, cols]
            s = [lax.dot_general(q, k[:, cols], (((1,), (1,)), ((), ())), preferred_element_type=F32)
                 for k in k_refs]
            m = functools.reduce(jnp.maximum, [jnp.max(x, axis=-1, keepdims=True) for x in s])
            e = [jnp.exp2(x - m) for x in s]
            o = functools.reduce(jnp.add, [jnp.dot(x.astype(BF16), v[:, cols], preferred_element_type=F32)
                                           for x, v in zip(e, v_refs)])
            ones_lane = C_VDIM if hh == 0 else 0
            l = o[:, ones_lane:ones_lane + 1]
            lane = lax.broadcasted_iota(jnp.int32, o.shape, 1)
            own = (lane < C_VDIM) if hh == 0 else (lane >= C_VDIM)
            o = jnp.where(own, o * (1.0 / l), 0.0)
            acc = o if acc is None else acc + o
        o_ref[:, pair * LANE:(pair + 1) * LANE] = acc.astype(o_ref.dtype)


def _attention(q, k, v, latent, prev=None):
    width = ATT_HG * HEAD_PAD
    n_hg = C_HEADS // ATT_HG
    if latent:
        tq = ATT_TQ
        nq = SEQ // tq
        q_spec = pl.BlockSpec((tq, width), lambda b, g, t: ((NC_ROWS + b * SEQ) // tq + t, g))
        kv_specs = [pl.BlockSpec((CTX_LEN, width), lambda b, g, t: (b, g)),
                    pl.BlockSpec((SEQ, width), lambda b, g, t: (NC_ROWS // SEQ + b, g))]
    else:
        tq = CTX_LEN
        nq = 1
        q_spec = pl.BlockSpec((tq, width), lambda b, g, t: (b, g))
        kv_specs = [pl.BlockSpec((CTX_LEN, width), lambda b, g, t: (b, g))]
    n_pieces = len(kv_specs)
    o_spec = pl.BlockSpec((tq, ATT_HG * C_VDIM), q_spec.index_map)
    ins = [q] + [k] * n_pieces + [v] * n_pieces
    in_specs = [q_spec] + kv_specs + kv_specs
    aliases = {}
    if prev is not None:
        aliases = {len(ins): 0}
        ins.append(prev)
        in_specs.append(pl.BlockSpec(memory_space=pl.ANY))
    return pl.pallas_call(
        functools.partial(_attn_kernel, n_pieces=n_pieces),
        out_shape=jax.ShapeDtypeStruct((N_ROWS, C_HEADS * C_VDIM), BF16),
        grid=(BATCH, n_hg, nq),
        in_specs=in_specs,
        out_specs=o_spec,
        input_output_aliases=aliases,
        compiler_params=_params("arbitrary", "arbitrary", "arbitrary"),
        name="attn_latent" if latent else "attn_context",
    )(*ins)


def _odd_out_kernel(h_ref, mod_ref, o_ref, w_ref, out_ref):
    y = jnp.dot(o_ref[...], w_ref[...], preferred_element_type=F32)
    out_ref[...] = h_ref[...] + mod_ref[2] * y


def _odd_out(h, mod, o, w_out, tile0, n_tiles):
    tm = TM
    return pl.pallas_call(
        _odd_out_kernel,
        out_shape=jax.ShapeDtypeStruct((N_ROWS, D_MODEL), F32),
        grid=(n_tiles,),
        in_specs=[_rows(tm, D_MODEL, tile0), _mod_spec(tm, tile0), _rows(tm, C_HEADS * C_VDIM, tile0),
                  _const(w_out.shape)],
        out_specs=_rows(tm, D_MODEL, tile0),
        input_output_aliases={0: 0},
        compiler_params=_params("arbitrary"),
        name="odd_out",
    )(h, mod, o, w_out)


def _route(m, router):
    m_hi, m_lo = _split_bf16(m)
    r_hi, r_lo = _split_bf16(router)
    dot = functools.partial(jnp.dot, preferred_element_type=F32)
    logits = dot(m_hi, r_hi) + (dot(m_hi, r_lo) + dot(m_lo, r_hi))
    lane = lax.broadcasted_iota(jnp.int32, logits.shape, 1)
    neg = -jnp.inf
    l1 = jnp.where(lane < N_EXPERTS, logits, neg)
    m1 = jnp.max(l1, axis=-1, keepdims=True)
    i1 = jnp.min(jnp.where(l1 == m1, lane, LANE), axis=-1, keepdims=True)
    l2 = jnp.where(lane == i1, neg, l1)
    m2 = jnp.max(l2, axis=-1, keepdims=True)
    i2 = jnp.min(jnp.where(l2 == m2, lane, LANE), axis=-1, keepdims=True)
    e2 = jnp.exp(m2 - m1)
    w1 = 1.0 / (1.0 + e2)
    w2 = e2 / (1.0 + e2)
    return i1, i2, w1, w2


R_E0, R_E1, R_RANK0, R_RANK1, R_W0, R_W1 = range(6)
R_POS0, R_POS1 = R_RANK0, R_RANK1
R_ROWS = 8
F_VALID, F_FIRST, F_LAST = 1, 2, 4


def _moe_route_kernel(h_ref, mod_ref, ng_ref, rt_ref, m_ref, rcol_ref, rrow_ref, cnt_ref, carry_s):
    @pl.when(pl.program_id(0) == 0)
    def _():
        carry_s[...] = jnp.zeros_like(carry_s)

    m = _rms(h_ref[...], ng_ref[...]) * (1.0 + mod_ref[4]) + mod_ref[3]
    m_ref[...] = m.astype(m_ref.dtype)
    i1, i2, w1, w2 = _route(m, rt_ref[...])
    tm = m.shape[0]
    lane = lax.broadcasted_iota(jnp.int32, (tm, LANE), 1)
    picked = jnp.where(lane == i1, 1.0, jnp.where(lane == i2, 1.0, 0.0))
    ri = lax.broadcasted_iota(jnp.int32, (tm, tm), 0)
    ci = lax.broadcasted_iota(jnp.int32, (tm, tm), 1)
    earlier = jnp.where(ci < ri, 1.0, 0.0).astype(BF16)
    before = carry_s[0:1, :]
    rank_all = jnp.dot(earlier, picked.astype(BF16), preferred_element_type=F32) + before
    rank0 = jnp.sum(jnp.where(lane == i1, rank_all, 0.0), axis=-1, keepdims=True)
    rank1 = jnp.sum(jnp.where(lane == i2, rank_all, 0.0), axis=-1, keepdims=True)
    after = before + jnp.sum(picked, axis=0, keepdims=True)
    carry_s[...] = jnp.broadcast_to(after, carry_s.shape)
    sub = lax.broadcasted_iota(jnp.int32, cnt_ref.shape, 0)
    cnt_ref[...] = jnp.where(sub == 0, before, jnp.where(sub == 1, after, 0.0))

    fields = {R_E0: i1.astype(F32), R_E1: i2.astype(F32), R_RANK0: rank0, R_RANK1: rank1, R_W0: w1, R_W1: w2}
    rcol = jnp.zeros((tm, LANE), F32)
    for k, val in fields.items():
        rcol = jnp.where(lane == k, val, rcol)
    rcol_ref[...] = rcol
    rrow_ref[...] = jnp.transpose(rcol)[:R_ROWS, :]


def _moe_route(h, mod, ng, router, tile0, n_chunks):
    tm = MOE_TC
    rows = n_chunks * tm
    return pl.pallas_call(
        _moe_route_kernel,
        out_shape=(jax.ShapeDtypeStruct((rows, D_MODEL), BF16),
                   jax.ShapeDtypeStruct((rows, LANE), F32),
                   jax.ShapeDtypeStruct((R_ROWS, rows), F32),
                   jax.ShapeDtypeStruct((n_chunks, R_ROWS, LANE), F32)),
        grid=(n_chunks,),
        in_specs=[_rows(tm, D_MODEL, tile0), _mod_spec(tm, tile0), _const(ng.shape), _const(router.shape)],
        out_specs=(_rows(tm, D_MODEL), _rows(tm, LANE),
                   pl.BlockSpec((R_ROWS, tm), lambda i: (0, i)),
                   pl.BlockSpec((None, R_ROWS, LANE), lambda i: (i, 0, 0))),
        scratch_shapes=[pltpu.VMEM((R_ROWS, LANE), F32)],
        compiler_params=_params("arbitrary"),
        name="moe_route",
    )(h, mod, ng, router)


def _moe_plan(cnt, n_chunks, n_blk):
    i32 = jnp.int32
    before = cnt[:, 0, :N_EXPERTS].astype(i32)
    after = cnt[:, 1, :N_EXPERTS].astype(i32)
    padded = (after[-1] + MOE_XT - 1) // MOE_XT * MOE_XT
    ends = jnp.cumsum(padded).astype(i32)
    base = ends - padded
    used_rows = ends[-1]
    start = base[None, :] + before
    stop = base[None, :] + after
    lo = jnp.arange(n_blk, dtype=i32) * MOE_BLK
    overlap = (start[:, :, None] < lo + MOE_BLK) & (stop[:, :, None] > lo)
    inc = jnp.any(overlap, axis=1)
    fill = (lo < used_rows) & ~jnp.any(inc, axis=0)
    inc_fill = inc.at[0].set(inc[0] | fill)
    n_items = n_chunks * N_EXPERTS + n_blk

    def items(mat):
        idx = jnp.nonzero(mat.reshape(-1), size=n_items, fill_value=-1)[0].astype(i32)
        valid = idx >= 0
        n_valid = jnp.sum(valid).astype(i32)
        idx = jnp.where(valid, idx, idx[jnp.maximum(n_valid - 1, 0)])
        outer = idx // mat.shape[1]
        inner = idx % mat.shape[1]
        prev = jnp.concatenate([jnp.full((1,), -1, i32), outer[:-1]])
        nxt = jnp.concatenate([outer[1:], jnp.full((1,), -1, i32)])
        pos = jnp.arange(n_items, dtype=i32)
        first = valid & (outer != prev)
        last = valid & ((outer != nxt) | (pos == n_valid - 1))
        flags = valid.astype(i32) * F_VALID + first.astype(i32) * F_FIRST + last.astype(i32) * F_LAST
        return outer, inner, flags

    disp_blk, disp_chunk, disp_flags = items(inc_fill.T)
    comb_chunk, comb_blk, comb_flags = items(inc)
    n_xt = n_blk * MOE_BLK // MOE_XT
    t0 = jnp.arange(n_xt, dtype=i32) * MOE_XT
    tile_used = (t0 < used_rows).astype(i32)
    tile_src = jnp.minimum(jnp.arange(n_xt, dtype=i32), used_rows // MOE_XT - 1)
    tile_e = jnp.minimum(jnp.sum(t0[:, None] >= ends[None, :], axis=1), N_EXPERTS - 1).astype(i32)[tile_src]
    return {"base": base, "disp": (disp_chunk, disp_blk, disp_flags), "comb": (comb_chunk, comb_blk, comb_flags),
            "tiles": (tile_e, tile_src, tile_used)}


def _sorted_rows(rcol, rrow, base):
    base_f = base.astype(F32)
    b0 = base_f[rcol[:, R_E0].astype(jnp.int32)]
    b1 = base_f[rcol[:, R_E1].astype(jnp.int32)]
    lane = jnp.arange(LANE)
    rcol = rcol + jnp.where(lane == R_POS0, b0[:, None], 0.0) + jnp.where(lane == R_POS1, b1[:, None], 0.0)
    sub = jnp.arange(R_ROWS)[:, None]
    rrow = rrow + jnp.where(sub == R_POS0, b0[None, :], 0.0) + jnp.where(sub == R_POS1, b1[None, :], 0.0)
    return rcol, rrow


def _moe_dispatch_kernel(c_ref, d_ref, f_ref, m_ref, rrow_ref, x_ref):
    n = pl.program_id(0)
    flag = f_ref[n]

    @pl.when((flag & F_VALID) != 0)
    def _():
        rr = rrow_ref[...]
        row0 = (d_ref[n] * MOE_BLK).astype(F32)
        rel0 = rr[R_POS0:R_POS0 + 1] - row0
        rel1 = rr[R_POS1:R_POS1 + 1] - row0
        ri = lax.broadcasted_iota(jnp.int32, (MOE_BLK, MOE_TC), 0).astype(F32)
        onehot = jnp.where(rel0 == ri, 1.0, jnp.where(rel1 == ri, 1.0, 0.0)).astype(BF16)
        rows = jnp.dot(onehot, m_ref[...], preferred_element_type=F32).astype(x_ref.dtype)

        @pl.when((flag & F_FIRST) != 0)
        def _():
            x_ref[...] = rows

        @pl.when((flag & F_FIRST) == 0)
        def _():
            x_ref[...] = x_ref[...] + rows


def _moe_dispatch(m, rrow, plan, n_blk):
    chunk, blk, flags = plan["disp"]
    n_items = chunk.shape[0]
    grid_spec = pltpu.PrefetchScalarGridSpec(
        num_scalar_prefetch=3,
        grid=(n_items,),
        in_specs=[pl.BlockSpec((MOE_TC, D_MODEL), lambda n, c, d, f: (c[n], 0)),
                  pl.BlockSpec((R_ROWS, MOE_TC), lambda n, c, d, f: (0, c[n]))],
        out_specs=pl.BlockSpec((MOE_BLK, D_MODEL), lambda n, c, d, f: (d[n], 0)),
    )
    return pl.pallas_call(
        _moe_dispatch_kernel,
        out_shape=jax.ShapeDtypeStruct((n_blk * MOE_BLK, D_MODEL), BF16),
        grid_spec=grid_spec,
        compiler_params=_params("arbitrary"),
        name="moe_dispatch",
    )(chunk, blk, flags, m, rrow)


def _moe_experts_kernel(e_ref, src_ref, used_ref, x_ref, wg_ref, wu_ref, wd_ref, y_ref, acc_s):
    t = pl.program_id(0)
    j = pl.program_id(1)
    last = j == pl.num_programs(1) - 1
    used = used_ref[t] != 0

    @pl.when(used)
    def _():
        x = x_ref[...]
        gate = jnp.dot(x, wg_ref[...].astype(BF16), preferred_element_type=F32)
        up = jnp.dot(x, wu_ref[...].astype(BF16), preferred_element_type=F32)
        act = (_silu(gate) * up).astype(BF16)
        part = jnp.dot(act, wd_ref[...].astype(BF16), preferred_element_type=F32)

        @pl.when(j == 0)
        def _():
            acc_s[...] = part

        @pl.when(j != 0)
        def _():
            acc_s[...] += part

        @pl.when(last)
        def _():
            y_ref[...] = acc_s[...].astype(y_ref.dtype)

    @pl.when(jnp.logical_not(used) & last)
    def _():
        y_ref[...] = jnp.zeros_like(y_ref)


def _moe_experts(x, plan, wts):
    tile_e, tile_src, tile_used = plan["tiles"]
    n_xt = tile_e.shape[0]
    li = wts["moe_layer"]
    th = MOE_TH
    nj = MOE_HIDDEN // th

    def hidden(t, j, used):
        return jnp.where(used[t] != 0, j, nj - 1)

    grid_spec = pltpu.PrefetchScalarGridSpec(
        num_scalar_prefetch=3,
        grid=(n_xt, nj),
        in_specs=[pl.BlockSpec((MOE_XT, D_MODEL), lambda t, j, e, s, u: (s[t], 0)),
                  pl.BlockSpec((None, None, D_MODEL, th), lambda t, j, e, s, u: (li, e[t], 0, hidden(t, j, u))),
                  pl.BlockSpec((None, None, D_MODEL, th), lambda t, j, e, s, u: (li, e[t], 0, hidden(t, j, u))),
                  pl.BlockSpec((None, None, th, D_MODEL), lambda t, j, e, s, u: (li, e[t], hidden(t, j, u), 0))],
        out_specs=pl.BlockSpec((MOE_XT, D_MODEL), lambda t, j, e, s, u: (t, 0)),
        scratch_shapes=[pltpu.VMEM((MOE_XT, D_MODEL), F32)],
    )
    return pl.pallas_call(
        _moe_experts_kernel,
        out_shape=jax.ShapeDtypeStruct((n_xt * MOE_XT, D_MODEL), BF16),
        grid_spec=grid_spec,
        compiler_params=_params("arbitrary", "arbitrary"),
        name="moe_experts",
    )(tile_e, tile_src, tile_used, x, wts["moe_g"], wts["moe_u"], wts["moe_d"])


def _moe_combine_kernel(c_ref, d_ref, f_ref, rcol_ref, y_ref, h_ref, mod_ref, out_ref, acc_s):
    n = pl.program_id(0)
    flag = f_ref[n]

    @pl.when((flag & F_VALID) != 0)
    def _():
        rc = rcol_ref[...]
        row0 = (d_ref[n] * MOE_BLK).astype(F32)
        rel0 = rc[:, R_POS0:R_POS0 + 1] - row0
        rel1 = rc[:, R_POS1:R_POS1 + 1] - row0
        li = lax.broadcasted_iota(jnp.int32, (MOE_TC, MOE_BLK), 1).astype(F32)
        gates = jnp.where(rel0 == li, rc[:, R_W0:R_W0 + 1], jnp.where(rel1 == li, rc[:, R_W1:R_W1 + 1], 0.0))
        part = jnp.dot(gates.astype(BF16), y_ref[...], preferred_element_type=F32)

        @pl.when((flag & F_FIRST) != 0)
        def _():
            acc_s[...] = part

        @pl.when((flag & F_FIRST) == 0)
        def _():
            acc_s[...] += part

        @pl.when((flag & F_LAST) != 0)
        def _():
            out_ref[...] = h_ref[...] + mod_ref[5] * acc_s[...]


def _moe_combine(h, mod, rcol, y, plan, tile0):
    chunk, blk, flags = plan["comb"]
    n_items = chunk.shape[0]
    grid_spec = pltpu.PrefetchScalarGridSpec(
        num_scalar_prefetch=3,
        grid=(n_items,),
        in_specs=[pl.BlockSpec((MOE_TC, LANE), lambda n, c, d, f: (c[n], 0)),
                  pl.BlockSpec((MOE_BLK, D_MODEL), lambda n, c, d, f: (d[n], 0)),
                  pl.BlockSpec((MOE_TC, D_MODEL), lambda n, c, d, f: (c[n] + tile0, 0)),
                  pl.BlockSpec((N_MOD, None, 1, D_MODEL),
                               lambda n, c, d, f: (0, _mod_row(c[n] + tile0, MOE_TC), 0, 0))],
        out_specs=pl.BlockSpec((MOE_TC, D_MODEL), lambda n, c, d, f: (c[n] + tile0, 0)),
        scratch_shapes=[pltpu.VMEM((MOE_TC, D_MODEL), F32)],
    )
    return pl.pallas_call(
        _moe_combine_kernel,
        out_shape=jax.ShapeDtypeStruct((N_ROWS, D_MODEL), F32),
        grid_spec=grid_spec,
        input_output_aliases={5: 0},
        compiler_params=_params("arbitrary"),
        name="moe_combine",
    )(chunk, blk, flags, rcol, y, h, mod)


def _moe(h, mod, ng, wts, first_row):
    tile0 = first_row // MOE_TC
    n_chunks = (N_ROWS - first_row) // MOE_TC
    n_blk = (2 * n_chunks * MOE_TC + N_EXPERTS * MOE_XT) // MOE_BLK
    m, rcol, rrow, cnt = _moe_route(h, mod, ng, wts["router"], tile0, n_chunks)
    plan = _moe_plan(cnt, n_chunks, n_blk)
    rcol, rrow = _sorted_rows(rcol, rrow, plan["base"])
    x = _moe_dispatch(m, rrow, plan, n_blk)
    y = _moe_experts(x, plan, wts)
    return _moe_combine(h, mod, rcol, y, plan, tile0)


def _final_kernel(h_ref, g_ref, o_ref):
    o_ref[...] = _rms(h_ref[...], g_ref[...])


def _final_norm(h, g):
    tm = TM
    return pl.pallas_call(
        _final_kernel,
        out_shape=jax.ShapeDtypeStruct((NL_ROWS, D_MODEL), F32),
        grid=(NL_ROWS // tm,),
        in_specs=[_rows(tm, D_MODEL, NC_ROWS // tm), _const(g.shape)],
        out_specs=_rows(tm, D_MODEL),
        compiler_params=_params("arbitrary"),
        name="final_norm",
    )(h, g)


def _row_vec(x):
    return x.reshape(1, -1).astype(F32)


def _even_weights(i, ev_w_in, a_ln_g, a_ln_b, a_ws, a_bs, b_gate_w, b_gate_b, b_norm_g, ev_w_out,
                  ff_w_gate, ff_w_up, ff_w_down):
    w_in = ev_w_in[i]
    qkvr = 2 * B_QK + 2 * B_V
    gate_w = jnp.zeros((2 * B_GATE_RANK, 2 * B_QK), F32)
    gate_w = gate_w.at[:B_GATE_RANK, :B_QK].set(b_gate_w[i, 0]).at[B_GATE_RANK:, B_QK:].set(b_gate_w[i, 1])
    nch = FF_HIDDEN // FF_TH
    return {
        "w_a": w_in[:, :2 * A_WIDTH].astype(BF16),
        "w_q": w_in[:, 2 * A_WIDTH:2 * A_WIDTH + qkvr].astype(BF16),
        "w_gf": w_in[:, 2 * A_WIDTH + qkvr:].astype(BF16),
        "gate_w": gate_w.astype(BF16),
        "gate_b": b_gate_b[i].reshape(1, 2 * B_QK).astype(F32),
        "ln_g": _row_vec(a_ln_g[i]),
        "ln_b": _row_vec(a_ln_b[i]),
        "ws": a_ws[i].astype(BF16),
        "bs": jnp.broadcast_to(a_bs[i][:, :, None], (A_HEADS, A_CHUNK, A_HEAD_DIM)).astype(F32),
        "b_norm_g": _row_vec(b_norm_g[i]),
        "w_out_a": ev_w_out[i, :A_WIDTH].astype(BF16),
        "w_out_b": ev_w_out[i, A_WIDTH:].astype(BF16),
        "ff_g": jnp.transpose(ff_w_gate[i].reshape(D_MODEL, nch, FF_TH), (1, 0, 2)).astype(BF16),
        "ff_u": jnp.transpose(ff_w_up[i].reshape(D_MODEL, nch, FF_TH), (1, 0, 2)).astype(BF16),
        "ff_d": ff_w_down[i].reshape(nch, FF_TH, D_MODEL).astype(BF16),
    }


def _rope_swap_index():
    quarter = C_ROPE // 4
    idx = np.arange(C_ROPE)
    return np.concatenate([idx[quarter:2 * quarter], idx[:quarter], idx[3 * quarter:], idx[2 * quarter:3 * quarter]])


def _odd_weights(i, od_w_in, c_q_norm_g, c_w_uq, c_kv_norm_g, c_w_ukv, od_w_out,
                 moe_router, moe_w_gate, moe_w_up, moe_w_down):
    swap = _rope_swap_index()
    w_in = od_w_in[i]
    low = C_Q_RANK + C_KV_RANK
    k_rope_w = w_in[:, low:]
    pad = jnp.zeros((D_MODEL, LANE - C_ROPE), F32)
    w_in_x = jnp.concatenate([w_in[:, :low], k_rope_w, pad, k_rope_w[:, swap], pad], axis=1)

    uq = c_w_uq[i].reshape(C_Q_RANK, C_HEADS, C_NOPE + C_ROPE)
    zq = jnp.zeros((C_Q_RANK, C_HEADS, HEAD_PAD - C_NOPE - C_ROPE), F32)
    w_q = jnp.concatenate([uq, zq], axis=-1)
    w_q_swap = jnp.concatenate([jnp.zeros((C_Q_RANK, C_HEADS, C_NOPE), F32), uq[:, :, C_NOPE:][:, :, swap], zq], axis=-1)
    ukv = c_w_ukv[i].reshape(C_KV_RANK, C_HEADS, C_NOPE + C_VDIM)
    w_k = jnp.concatenate([ukv[:, :, :C_NOPE], jnp.zeros((C_KV_RANK, C_HEADS, HEAD_PAD - C_NOPE), F32)], axis=-1)
    zv = jnp.zeros((C_KV_RANK, C_HEADS // 2, C_VDIM), F32)
    uv = ukv[:, :, C_NOPE:].reshape(C_KV_RANK, C_HEADS // 2, 2, C_VDIM)
    w_v = jnp.stack([jnp.concatenate([uv[:, :, 0], zv], axis=-1),
                     jnp.concatenate([zv, uv[:, :, 1]], axis=-1)], axis=2)
    v_one = np.zeros((C_HEADS // 2, 2, HEAD_PAD), np.float32)
    v_one[:, 0, C_VDIM] = 1.0
    v_one[:, 1, 0] = 1.0
    place = np.zeros((LANE, C_HEADS, HEAD_PAD), np.float32)
    for r in range(C_ROPE):
        place[r, :, C_NOPE + r] = 1.0
    router = jnp.concatenate([moe_router[i], jnp.zeros((D_MODEL, LANE - N_EXPERTS), F32)], axis=1)
    return {
        "w_in": w_in_x.astype(BF16),
        "q_norm_g": _row_vec(c_q_norm_g[i]),
        "kv_norm_g": _row_vec(c_kv_norm_g[i]),
        "w_q": w_q.reshape(C_Q_RANK, C_QK_PAD).astype(BF16),
        "w_q_swap": w_q_swap.reshape(C_Q_RANK, C_QK_PAD).astype(BF16),
        "w_k": w_k.reshape(C_KV_RANK, C_QK_PAD).astype(BF16),
        "w_v": w_v.reshape(C_KV_RANK, C_QK_PAD).astype(BF16),
        "place": jnp.asarray(place.reshape(LANE, C_QK_PAD), BF16),
        "v_one": jnp.asarray(v_one.reshape(1, C_QK_PAD), F32),
        "w_out": od_w_out[i].astype(BF16),
        "router": router,
        "moe_layer": i,
        "moe_g": moe_w_gate,
        "moe_u": moe_w_up,
        "moe_d": moe_w_down,
    }


def _rope_tables(tm):
    quarter = C_ROPE // 4
    half = C_ROPE // 2
    inv_freq = ROPE_BASE ** (-jnp.arange(0, half, 2, dtype=F32) / half)
    pos = jnp.arange(SEQ, dtype=jnp.int32)
    ang_r = (pos // GRID_W).astype(F32)[:, None] * inv_freq
    ang_c = (pos % GRID_W).astype(F32)[:, None] * inv_freq
    cos = jnp.concatenate([jnp.cos(ang_r)] * 2 + [jnp.cos(ang_c)] * 2, axis=1)
    sin = jnp.concatenate([-jnp.sin(ang_r), jnp.sin(ang_r), -jnp.sin(ang_c), jnp.sin(ang_c)], axis=1)
    cos = jnp.concatenate([cos, jnp.ones((tm, C_ROPE), F32)], axis=0)
    sin = jnp.concatenate([sin, jnp.zeros((tm, C_ROPE), F32)], axis=0)
    rows = SEQ + tm
    ones = jnp.ones((rows, C_NOPE), F32)
    z = lambda w: jnp.zeros((rows, w), F32)
    cq = jnp.concatenate([ones, cos, z(HEAD_PAD - C_NOPE - C_ROPE)], axis=1)
    sq = jnp.concatenate([z(C_NOPE), sin, z(HEAD_PAD - C_NOPE - C_ROPE)], axis=1)
    ck = jnp.concatenate([cos, z(LANE - C_ROPE)], axis=1)
    sk = jnp.concatenate([sin, z(LANE - C_ROPE)], axis=1)
    return cq, sq, ck, sk


def kernel(x, c, ctx, c_ctx, mod_w, mod_b, norm_mix_g, norm_ffn_g, final_g, ev_w_in, a_ln_g, a_ln_b, a_ws, a_bs, b_gate_w, b_gate_b, b_norm_g, ev_w_out, od_w_in, c_q_norm_g, c_w_uq, c_kv_norm_g, c_w_ukv, od_w_out, ff_w_gate, ff_w_up, ff_w_down, moe_router, moe_w_gate, moe_w_up, moe_w_down):
    h = jnp.concatenate([ctx.reshape(NC_ROWS, D_MODEL), x.reshape(NL_ROWS, D_MODEL)], axis=0)
    cond = jnp.concatenate([c, c_ctx[None, :], jnp.zeros((MOD_PAD - MOD_ROWS, D_MODEL), F32)], axis=0)
    mod = _modulation(cond, mod_w, mod_b)
    rope = _rope_tables(TM)

    for layer in range(DEPTH):
        i = layer // 2
        need_ctx = layer < DEPTH - 1
        ng_mix = _row_vec(norm_mix_g[layer])
        ng_ffn = _row_vec(norm_ffn_g[layer])
        if layer % 2 == 0:
            wts = _even_weights(i, ev_w_in, a_ln_g, a_ln_b, a_ws, a_bs, b_gate_w, b_gate_b, b_norm_g,
                                ev_w_out, ff_w_gate, ff_w_up, ff_w_down)
            ya, qkv, r, g = _even_in(h, mod[layer], ng_mix, wts)
            o_f, o_b = _gla(qkv, g)
            h = _even_out(h, mod[layer], o_f, o_b, r, ya, wts)
            h = _ffn(h, mod[layer], ng_ffn, wts)
        else:
            wts = _odd_weights(i, od_w_in, c_q_norm_g, c_w_uq, c_kv_norm_g, c_w_ukv, od_w_out,
                               moe_router, moe_w_gate, moe_w_up, moe_w_down)
            q, k, v = _odd_in(h, mod[layer], ng_mix, wts, rope)
            o = _attention(q, k, v, latent=True)
            if need_ctx:
                o = _attention(q, k, v, latent=False, prev=o)
            first = 0 if need_ctx else NC_ROWS
            h = _odd_out(h, mod[layer], o, wts["w_out"], first // TM, (N_ROWS - first) // TM)
            h = _moe(h, mod[layer], ng_ffn, wts, first)
    out = _final_norm(h, _row_vec(final_g))
    return out.reshape(BATCH, SEQ, D_MODEL)
```

```python
import functools

import numpy as np
import jax
import jax.numpy as jnp
from jax import lax
from jax.experimental import pallas as pl
from jax.experimental.pallas import tpu as pltpu

D_MODEL = 1024
BATCH = 8
SEQ = 2048
DEPTH = 4
GRID_W = 64
CTX_LEN = 256
EPS = 1e-6
N_MOD = 6

A_HEADS = 4
A_HEAD_DIM = 128
A_CHUNK = 128
A_WIDTH = A_HEADS * A_HEAD_DIM

B_HEADS = 4
B_DK = 64
B_DV = 128
B_QK = B_HEADS * B_DK
B_V = B_HEADS * B_DV
B_GATE_RANK = 16
B_GATE_TEMP = 16.0
B_CHUNK = 64

C_HEADS = 16
C_NOPE = 64
C_ROPE = 32
C_VDIM = 64
C_Q_RANK = 256
C_KV_RANK = 256
C_SCALE = (C_NOPE + C_ROPE) ** -0.5
LOG2_E = 1.4426950408889634
ROPE_BASE = 10000.0

FF_HIDDEN = 2816
N_EXPERTS = 8
MOE_HIDDEN = 3584

NC_ROWS = BATCH * CTX_LEN
NL_ROWS = BATCH * SEQ
N_ROWS = NC_ROWS + NL_ROWS
MOD_ROWS = BATCH + 1
MOD_PAD = 16

LANE = 128
HEAD_PAD = 128
C_QK_PAD = C_HEADS * HEAD_PAD

TM = 512
GLA_BLK = 256
ATT_TQ = 512
ATT_HG = 4
FF_TH = 256
MOE_TC = 512
MOE_BLK = 256
MOE_XT = 1024
MOE_TH = 896
VMEM_LIMIT = 56 * 1024 * 1024

F32 = jnp.float32
BF16 = jnp.bfloat16


def _mm(a, b):
    return jnp.dot(a.astype(BF16), b.astype(BF16), preferred_element_type=F32)


def _mm_nt(a, b):
    return lax.dot_general(a.astype(BF16), b.astype(BF16), (((1,), (1,)), ((), ())),
                           preferred_element_type=F32)


def _mm_tn(a, b):
    return lax.dot_general(a.astype(BF16), b.astype(BF16), (((0,), (0,)), ((), ())),
                           preferred_element_type=F32)


def _sigmoid(x):
    return 1.0 / (1.0 + jnp.exp(-x))


def _silu(x):
    return x * _sigmoid(x)


def _gelu(x):
    return 0.5 * x * (1.0 + jnp.tanh(np.sqrt(2.0 / np.pi) * (x + 0.044715 * (x * x * x))))


def _rms(x, g):
    return x * lax.rsqrt(jnp.mean(x * x, axis=-1, keepdims=True) + EPS) * g


def _split_bf16(x):
    hi = x.astype(BF16)
    lo = (x - hi.astype(F32)).astype(BF16)
    return hi, lo


def _rows(tm, width, tile0=0, col=0):
    return pl.BlockSpec((tm, width), lambda i: (i + tile0, col))


def _const(shape):
    zeros = (0,) * len(shape)
    return pl.BlockSpec(shape, lambda *_: zeros, pipeline_mode=pl.Buffered(1))


def _mod_row(tile, tm):
    return jnp.where(tile < NC_ROWS // tm, BATCH, (tile * tm - NC_ROWS) // SEQ)


def _mod_spec(tm, tile0=0):
    return pl.BlockSpec((N_MOD, None, 1, D_MODEL), lambda i: (0, _mod_row(i + tile0, tm), 0, 0))


def _params(*sem):
    return pltpu.CompilerParams(dimension_semantics=sem, vmem_limit_bytes=VMEM_LIMIT)


def _mod_kernel(c_ref, w_ref, b_ref, o_ref):
    o_ref[...] = _mm(_silu(c_ref[...]), w_ref[...]) + b_ref[...]


def _modulation(cond, mod_w, mod_b):
    tn = 1536
    out = pl.pallas_call(
        _mod_kernel,
        out_shape=jax.ShapeDtypeStruct((DEPTH, MOD_PAD, N_MOD * D_MODEL), F32),
        grid=(DEPTH, N_MOD * D_MODEL // tn),
        in_specs=[
            pl.BlockSpec((MOD_PAD, D_MODEL), lambda l, j: (0, 0)),
            pl.BlockSpec((None, D_MODEL, tn), lambda l, j: (l, 0, j)),
            pl.BlockSpec((None, 1, tn), lambda l, j: (l, 0, j)),
        ],
        out_specs=pl.BlockSpec((None, MOD_PAD, tn), lambda l, j: (l, 0, j)),
        compiler_params=_params("arbitrary", "arbitrary"),
        name="modulation",
    )(cond, mod_w, mod_b.reshape(DEPTH, 1, N_MOD * D_MODEL))
    out = out[:, :MOD_ROWS].reshape(DEPTH, MOD_ROWS, N_MOD, 1, D_MODEL)
    return jnp.transpose(out, (0, 2, 1, 3, 4))


def _even_in_kernel(h_ref, mod_ref, ng_ref, wa_ref, wq_ref, wgf_ref, gw_ref, gb_ref,
                    lng_ref, lnb_ref, ws_ref, bs_ref, ya_ref, qkv_ref, r_ref, g_ref):
    tm = h_ref.shape[0]
    n = _rms(h_ref[...], ng_ref[...]) * (1.0 + mod_ref[1]) + mod_ref[0]
    nb = n.astype(BF16)

    ua_va = jnp.dot(nb, wa_ref[...], preferred_element_type=F32)
    u = _gelu(ua_va[:, :A_WIDTH])
    v = _gelu(ua_va[:, A_WIDTH:])
    mu = jnp.mean(v, axis=-1, keepdims=True)
    vc = v - mu
    var = jnp.mean(vc * vc, axis=-1, keepdims=True)
    v = (vc * lax.rsqrt(var + EPS) * lng_ref[...] + lnb_ref[...]).astype(BF16)
    for c in range(tm // A_CHUNK):
        rows = slice(c * A_CHUNK, (c + 1) * A_CHUNK)
        for h in range(A_HEADS):
            cols = slice(h * A_HEAD_DIM, (h + 1) * A_HEAD_DIM)
            mixed = jnp.dot(ws_ref[h], v[rows, cols], preferred_element_type=F32) + bs_ref[h]
            ya_ref[rows, cols] = (u[rows, cols] * mixed).astype(ya_ref.dtype)

    p = jnp.dot(nb, wq_ref[...], preferred_element_type=F32)
    qkv_ref[...] = p[:, :2 * B_QK + B_V].astype(qkv_ref.dtype)
    r_ref[...] = p[:, 2 * B_QK + B_V:].astype(r_ref.dtype)
    gf = jnp.dot(nb, wgf_ref[...], preferred_element_type=F32)
    logit = _mm(gf, gw_ref[...]) + gb_ref[...]
    log_sig = jnp.minimum(logit, 0.0) - jnp.log(1.0 + jnp.exp(-jnp.abs(logit)))
    g_ref[...] = log_sig / B_GATE_TEMP


def _even_in(h, mod, ng, wts):
    tm = TM
    n_tiles = N_ROWS // tm
    qkv_w = 2 * B_QK + B_V
    consts = [ng, wts["w_a"], wts["w_q"], wts["w_gf"], wts["gate_w"], wts["gate_b"],
              wts["ln_g"], wts["ln_b"], wts["ws"], wts["bs"]]
    return pl.pallas_call(
        _even_in_kernel,
        out_shape=(jax.ShapeDtypeStruct((N_ROWS, A_WIDTH), BF16),
                   jax.ShapeDtypeStruct((N_ROWS, qkv_w), BF16),
                   jax.ShapeDtypeStruct((N_ROWS, B_V), BF16),
                   jax.ShapeDtypeStruct((N_ROWS, 2 * B_QK), F32)),
        grid=(n_tiles,),
        in_specs=[_rows(tm, D_MODEL), _mod_spec(tm)] + [_const(a.shape) for a in consts],
        out_specs=(_rows(tm, A_WIDTH), _rows(tm, qkv_w), _rows(tm, B_V), _rows(tm, 2 * B_QK)),
        compiler_params=_params("arbitrary"),
        name="even_in",
    )(h, mod, *consts)


def _gla_direction(q, k, v, g, st_ref, reverse):
    n = GLA_BLK
    nch = n // B_CHUNK
    ri = lax.broadcasted_iota(jnp.int32, (n, n), 0)
    ci = lax.broadcasted_iota(jnp.int32, (n, n), 1)
    same = (ri // B_CHUNK) == (ci // B_CHUNK)
    tri = same & ((ci >= ri) if reverse else (ci <= ri))
    tri_b = jnp.where(tri, 1.0, 0.0).astype(BF16)
    same_b = jnp.where(same, 1.0, 0.0).astype(BF16)

    g_hi, g_lo = _split_bf16(g)
    dot = functools.partial(jnp.dot, preferred_element_type=F32)
    b = dot(tri_b, g_hi) + dot(tri_b, g_lo)
    b_tot = dot(same_b, g_hi) + dot(same_b, g_lo)
    q_dec = (q * (B_DK ** -0.5)) * jnp.exp(b)
    k_inv = k * jnp.exp(-b)
    k_dec = k * jnp.exp(b_tot - b)
    decay = jnp.exp(b_tot)

    lane = lax.broadcasted_iota(jnp.int32, (1, LANE), 1)
    even_lane = lane < B_DK

    outs = [None] * B_HEADS
    order = list(range(nch))[::-1] if reverse else list(range(nch))
    for p in range(B_HEADS // 2):
        lanes = slice(p * LANE, (p + 1) * LANE)
        qp = q_dec[:, lanes]
        q_heads = (jnp.where(even_lane, qp, 0.0).astype(BF16), jnp.where(even_lane, 0.0, qp).astype(BF16))
        k_inv_p = k_inv[:, lanes].astype(BF16)
        k_dec_p = k_dec[:, lanes].astype(BF16)
        intra = []
        for hh in range(2):
            h = 2 * p + hh
            scores = _mm_nt(q_heads[hh], k_inv_p)
            scores = jnp.where(tri, scores, 0.0)
            intra.append(_mm(scores, v[:, h * B_DV:(h + 1) * B_DV]))
        st = st_ref[p]
        inter = [[None] * nch for _ in range(2)]
        for c in order:
            rows = slice(c * B_CHUNK, (c + 1) * B_CHUNK)
            st_b = st.astype(BF16)
            for hh in range(2):
                inter[hh][c] = _mm_nt(q_heads[hh][rows], st_b)
            upd_e = _mm_tn(v[rows, (2 * p) * B_DV:(2 * p + 1) * B_DV], k_dec_p[rows])
            upd_o = _mm_tn(v[rows, (2 * p + 1) * B_DV:(2 * p + 2) * B_DV], k_dec_p[rows])
            st = st * decay[c * B_CHUNK:c * B_CHUNK + 1, lanes] + jnp.where(even_lane, upd_e, upd_o)
        st_ref[p] = st
        for hh in range(2):
            outs[2 * p + hh] = intra[hh] + jnp.concatenate(inter[hh], axis=0)
    return outs


def _gla_kernel(qf_ref, kf_ref, vf_ref, gf_ref, qb_ref, kb_ref, vb_ref, gb_ref,
                of_ref, ob_ref, stf_ref, stb_ref):
    @pl.when(pl.program_id(1) == 0)
    def _():
        stf_ref[...] = jnp.zeros_like(stf_ref)
        stb_ref[...] = jnp.zeros_like(stb_ref)

    outs = _gla_direction(qf_ref[...].astype(F32), kf_ref[...].astype(F32), vf_ref[...], gf_ref[...], stf_ref, False)
    for h in range(B_HEADS):
        of_ref[:, h * B_DV:(h + 1) * B_DV] = outs[h]
    outs = _gla_direction(qb_ref[...].astype(F32), kb_ref[...].astype(F32), vb_ref[...], gb_ref[...], stb_ref, True)
    for h in range(B_HEADS):
        ob_ref[:, h * B_DV:(h + 1) * B_DV] = outs[h]


def _gla(qkv, g):
    nb_ctx = CTX_LEN // GLA_BLK
    nb_lat = SEQ // GLA_BLK
    nblk = nb_ctx + nb_lat
    ctx_blocks = NC_ROWS // GLA_BLK

    def fwd_row(b, j):
        return jnp.where(j < nb_ctx, b * nb_ctx + j, ctx_blocks + b * nb_lat + (j - nb_ctx))

    def bwd_row(b, j):
        return jnp.where(j < nb_ctx, b * nb_ctx + (nb_ctx - 1 - j),
                         ctx_blocks + b * nb_lat + (nblk - 1 - j))

    def spec(width, col, row):
        return pl.BlockSpec((GLA_BLK, width), lambda b, j: (row(b, j), col))

    ins, in_specs = [], []
    for d, row in enumerate((fwd_row, bwd_row)):
        ins += [qkv, qkv, qkv, g]
        in_specs += [spec(B_QK, 0, row), spec(B_QK, 1, row), spec(B_V, 1, row), spec(B_QK, d, row)]
    return pl.pallas_call(
        _gla_kernel,
        out_shape=(jax.ShapeDtypeStruct((N_ROWS, B_V), F32),) * 2,
        grid=(BATCH, nblk),
        in_specs=in_specs,
        out_specs=(spec(B_V, 0, fwd_row), spec(B_V, 0, bwd_row)),
        scratch_shapes=[pltpu.VMEM((B_HEADS // 2, B_DV, LANE), F32)] * 2,
        compiler_params=_params("arbitrary", "arbitrary"),
        name="gla",
    )(*ins)


def _even_out_kernel(h_ref, mod_ref, of_ref, ob_ref, r_ref, ya_ref, bng_ref, woa_ref, wob_ref, out_ref):
    o = of_ref[...] + ob_ref[...]
    parts = [_rms(o[:, h * B_DV:(h + 1) * B_DV], bng_ref[...]) for h in range(B_HEADS)]
    yb = _silu(r_ref[...].astype(F32)) * jnp.concatenate(parts, axis=-1)
    y = jnp.dot(ya_ref[...], woa_ref[...], preferred_element_type=F32) + _mm(yb, wob_ref[...])
    out_ref[...] = h_ref[...] + mod_ref[2] * y


def _even_out(h, mod, o_f, o_b, r, ya, wts):
    tm = TM
    consts = [wts["b_norm_g"], wts["w_out_a"], wts["w_out_b"]]
    return pl.pallas_call(
        _even_out_kernel,
        out_shape=jax.ShapeDtypeStruct((N_ROWS, D_MODEL), F32),
        grid=(N_ROWS // tm,),
        in_specs=[_rows(tm, D_MODEL), _mod_spec(tm), _rows(tm, B_V), _rows(tm, B_V), _rows(tm, B_V),
                  _rows(tm, A_WIDTH)] + [_const(a.shape) for a in consts],
        out_specs=_rows(tm, D_MODEL),
        input_output_aliases={0: 0},
        compiler_params=_params("arbitrary"),
        name="even_out",
    )(h, mod, o_f, o_b, r, ya, *consts)


def _ffn_kernel(h_ref, mod_ref, ng_ref, wg_ref, wu_ref, wd_ref, out_ref):
    x = h_ref[...]
    m = (_rms(x, ng_ref[...]) * (1.0 + mod_ref[4]) + mod_ref[3]).astype(BF16)
    acc = jnp.zeros(x.shape, F32)
    for j in range(wg_ref.shape[0]):
        gate = jnp.dot(m, wg_ref[j], preferred_element_type=F32)
        up = jnp.dot(m, wu_ref[j], preferred_element_type=F32)
        act = (_silu(gate) * up).astype(BF16)
        acc = acc + jnp.dot(act, wd_ref[j], preferred_element_type=F32)
    out_ref[...] = x + mod_ref[5] * acc


def _ffn(h, mod, ng, wts):
    tm = TM
    consts = [ng, wts["ff_g"], wts["ff_u"], wts["ff_d"]]
    return pl.pallas_call(
        _ffn_kernel,
        out_shape=jax.ShapeDtypeStruct((N_ROWS, D_MODEL), F32),
        grid=(N_ROWS // tm,),
        in_specs=[_rows(tm, D_MODEL), _mod_spec(tm)] + [_const(a.shape) for a in consts],
        out_specs=_rows(tm, D_MODEL),
        input_output_aliases={0: 0},
        compiler_params=_params("arbitrary"),
        name="ffn",
    )(h, mod, *consts)


def _odd_in_kernel(h_ref, mod_ref, ng_ref, win_ref, qg_ref, kvg_ref, wq_ref, wqs_ref, wk_ref, wv_ref,
                   place_ref, vone_ref, cq_ref, sq_ref, ck_ref, sk_ref, q_ref, k_ref, v_ref):
    n = _rms(h_ref[...], ng_ref[...]) * (1.0 + mod_ref[1]) + mod_ref[0]
    p = jnp.dot(n.astype(BF16), win_ref[...], preferred_element_type=F32)
    cq = _rms(p[:, :C_Q_RANK], qg_ref[...]).astype(BF16)
    ckv = _rms(p[:, C_Q_RANK:C_Q_RANK + C_KV_RANK], kvg_ref[...]).astype(BF16)
    base = C_Q_RANK + C_KV_RANK
    kr = p[:, base:base + LANE] * ck_ref[...] + p[:, base + LANE:base + 2 * LANE] * sk_ref[...]

    q_plain = jnp.dot(cq, wq_ref[...], preferred_element_type=F32)
    q_swap = jnp.dot(cq, wqs_ref[...], preferred_element_type=F32)
    cq_t, sq_t = cq_ref[...], sq_ref[...]
    for h in range(C_HEADS):
        cols = slice(h * HEAD_PAD, (h + 1) * HEAD_PAD)
        q_ref[:, cols] = ((q_plain[:, cols] * cq_t + q_swap[:, cols] * sq_t) * (C_SCALE * LOG2_E)).astype(q_ref.dtype)
    k_cat = jnp.dot(ckv, wk_ref[...], preferred_element_type=F32) + _mm(kr, place_ref[...])
    k_ref[...] = k_cat.astype(k_ref.dtype)
    v_ref[...] = (jnp.dot(ckv, wv_ref[...], preferred_element_type=F32) + vone_ref[...]).astype(v_ref.dtype)


def _odd_in(h, mod, ng, wts, rope):
    tm = TM
    n_tiles = N_ROWS // tm
    consts = [ng, wts["w_in"], wts["q_norm_g"], wts["kv_norm_g"], wts["w_q"], wts["w_q_swap"],
              wts["w_k"], wts["w_v"], wts["place"], wts["v_one"]]
    ctx_tiles = NC_ROWS // tm
    lat_tiles = SEQ // tm

    def rope_spec():
        return pl.BlockSpec((tm, LANE), lambda i: (jnp.where(i < ctx_tiles, lat_tiles, (i - ctx_tiles) % lat_tiles), 0))

    return pl.pallas_call(
        _odd_in_kernel,
        out_shape=(jax.ShapeDtypeStruct((N_ROWS, C_QK_PAD), BF16),) * 3,
        grid=(n_tiles,),
        in_specs=[_rows(tm, D_MODEL), _mod_spec(tm)] + [_const(a.shape) for a in consts] + [rope_spec()] * 4,
        out_specs=(_rows(tm, C_QK_PAD),) * 3,
        compiler_params=_params("arbitrary"),
        name="odd_in",
    )(h, mod, *consts, *rope)


def _attn_kernel(*refs, n_pieces):
    q_ref = refs[0]
    k_refs = refs[1:1 + n_pieces]
    v_refs = refs[1 + n_pieces:1 + 2 * n_pieces]
    o_ref = refs[-1]
    for pair in range(ATT_HG // 2):
        acc = None
        for hh in range(2):
            cols = slice((2 * pair + hh) * HEAD_PAD, (2 * pair + hh + 1) * HEAD_PAD)
            q = q_ref[:, cols]
            s = [lax.dot_general(q, k[:, cols], (((1,), (1,)), ((), ())), preferred_element_type=F32)
                 for k in k_refs]
            m = functools.reduce(jnp.maximum, [jnp.max(x, axis=-1, keepdims=True) for x in s])
            e = [jnp.exp2(x - m) for x in s]
            o = functools.reduce(jnp.add, [jnp.dot(x.astype(BF16), v[:, cols], preferred_element_type=F32)
                                           for x, v in zip(e, v_refs)])
            ones_lane = C_VDIM if hh == 0 else 0
            l = o[:, ones_lane:ones_lane + 1]
            lane = lax.broadcasted_iota(jnp.int32, o.shape, 1)
            own = (lane < C_VDIM) if hh == 0 else (lane >= C_VDIM)
            o = jnp.where(own, o * (1.0 / l), 0.0)
            acc = o if acc is None else acc + o
        o_ref[:, pair * LANE:(pair + 1) * LANE] = acc.astype(o_ref.dtype)


def _attention(q, k, v, latent, prev=None):
    width = ATT_HG * HEAD_PAD
    n_hg = C_HEADS // ATT_HG
    if latent:
        tq = ATT_TQ
        nq = SEQ // tq
        q_spec = pl.BlockSpec((tq, width), lambda b, g, t: ((NC_ROWS + b * SEQ) // tq + t, g))
        kv_specs = [pl.BlockSpec((CTX_LEN, width), lambda b, g, t: (b, g)),
                    pl.BlockSpec((SEQ, width), lambda b, g, t: (NC_ROWS // SEQ + b, g))]
    else:
        tq = CTX_LEN
        nq = 1
        q_spec = pl.BlockSpec((tq, width), lambda b, g, t: (b, g))
        kv_specs = [pl.BlockSpec((CTX_LEN, width), lambda b, g, t: (b, g))]
    n_pieces = len(kv_specs)
    o_spec = pl.BlockSpec((tq, ATT_HG * C_VDIM), q_spec.index_map)
    ins = [q] + [k] * n_pieces + [v] * n_pieces
    in_specs = [q_spec] + kv_specs + kv_specs
    aliases = {}
    if prev is not None:
        aliases = {len(ins): 0}
        ins.append(prev)
        in_specs.append(pl.BlockSpec(memory_space=pl.ANY))
    return pl.pallas_call(
        functools.partial(_attn_kernel, n_pieces=n_pieces),
        out_shape=jax.ShapeDtypeStruct((N_ROWS, C_HEADS * C_VDIM), BF16),
        grid=(BATCH, n_hg, nq),
        in_specs=in_specs,
        out_specs=o_spec,
        input_output_aliases=aliases,
        compiler_params=_params("arbitrary", "arbitrary", "arbitrary"),
        name="attn_latent" if latent else "attn_context",
    )(*ins)


def _odd_out_kernel(h_ref, mod_ref, o_ref, w_ref, out_ref):
    y = jnp.dot(o_ref[...], w_ref[...], preferred_element_type=F32)
    out_ref[...] = h_ref[...] + mod_ref[2] * y


def _odd_out(h, mod, o, w_out, tile0, n_tiles):
    tm = TM
    return pl.pallas_call(
        _odd_out_kernel,
        out_shape=jax.ShapeDtypeStruct((N_ROWS, D_MODEL), F32),
        grid=(n_tiles,),
        in_specs=[_rows(tm, D_MODEL, tile0), _mod_spec(tm, tile0), _rows(tm, C_HEADS * C_VDIM, tile0),
                  _const(w_out.shape)],
        out_specs=_rows(tm, D_MODEL, tile0),
        input_output_aliases={0: 0},
        compiler_params=_params("arbitrary"),
        name="odd_out",
    )(h, mod, o, w_out)


def _route(m, router):
    m_hi, m_lo = _split_bf16(m)
    r_hi, r_lo = _split_bf16(router)
    dot = functools.partial(jnp.dot, preferred_element_type=F32)
    logits = dot(m_hi, r_hi) + (dot(m_hi, r_lo) + dot(m_lo, r_hi))
    lane = lax.broadcasted_iota(jnp.int32, logits.shape, 1)
    neg = -jnp.inf
    l1 = jnp.where(lane < N_EXPERTS, logits, neg)
    m1 = jnp.max(l1, axis=-1, keepdims=True)
    i1 = jnp.min(jnp.where(l1 == m1, lane, LANE), axis=-1, keepdims=True)
    l2 = jnp.where(lane == i1, neg, l1)
    m2 = jnp.max(l2, axis=-1, keepdims=True)
    i2 = jnp.min(jnp.where(l2 == m2, lane, LANE), axis=-1, keepdims=True)
    e2 = jnp.exp(m2 - m1)
    w1 = 1.0 / (1.0 + e2)
    w2 = e2 / (1.0 + e2)
    return i1, i2, w1, w2


R_E0, R_E1, R_RANK0, R_RANK1, R_W0, R_W1 = range(6)
R_POS0, R_POS1 = R_RANK0, R_RANK1
R_ROWS = 8
F_VALID, F_FIRST, F_LAST = 1, 2, 4


def _moe_route_kernel(h_ref, mod_ref, ng_ref, rt_ref, m_ref, rcol_ref, rrow_ref, cnt_ref, carry_s):
    @pl.when(pl.program_id(0) == 0)
    def _():
        carry_s[...] = jnp.zeros_like(carry_s)

    m = _rms(h_ref[...], ng_ref[...]) * (1.0 + mod_ref[4]) + mod_ref[3]
    m_ref[...] = m.astype(m_ref.dtype)
    i1, i2, w1, w2 = _route(m, rt_ref[...])
    tm = m.shape[0]
    lane = lax.broadcasted_iota(jnp.int32, (tm, LANE), 1)
    picked = jnp.where(lane == i1, 1.0, jnp.where(lane == i2, 1.0, 0.0))
    ri = lax.broadcasted_iota(jnp.int32, (tm, tm), 0)
    ci = lax.broadcasted_iota(jnp.int32, (tm, tm), 1)
    earlier = jnp.where(ci < ri, 1.0, 0.0).astype(BF16)
    before = carry_s[0:1, :]
    rank_all = jnp.dot(earlier, picked.astype(BF16), preferred_element_type=F32) + before
    rank0 = jnp.sum(jnp.where(lane == i1, rank_all, 0.0), axis=-1, keepdims=True)
    rank1 = jnp.sum(jnp.where(lane == i2, rank_all, 0.0), axis=-1, keepdims=True)
    after = before + jnp.sum(picked, axis=0, keepdims=True)
    carry_s[...] = jnp.broadcast_to(after, carry_s.shape)
    sub = lax.broadcasted_iota(jnp.int32, cnt_ref.shape, 0)
    cnt_ref[...] = jnp.where(sub == 0, before, jnp.where(sub == 1, after, 0.0))

    fields = {R_E0: i1.astype(F32), R_E1: i2.astype(F32), R_RANK0: rank0, R_RANK1: rank1, R_W0: w1, R_W1: w2}
    rcol = jnp.zeros((tm, LANE), F32)
    for k, val in fields.items():
        rcol = jnp.where(lane == k, val, rcol)
    rcol_ref[...] = rcol
    rrow_ref[...] = jnp.transpose(rcol)[:R_ROWS, :]


def _moe_route(h, mod, ng, router, tile0, n_chunks):
    tm = MOE_TC
    rows = n_chunks * tm
    return pl.pallas_call(
        _moe_route_kernel,
        out_shape=(jax.ShapeDtypeStruct((rows, D_MODEL), BF16),
                   jax.ShapeDtypeStruct((rows, LANE), F32),
                   jax.ShapeDtypeStruct((R_ROWS, rows), F32),
                   jax.ShapeDtypeStruct((n_chunks, R_ROWS, LANE), F32)),
        grid=(n_chunks,),
        in_specs=[_rows(tm, D_MODEL, tile0), _mod_spec(tm, tile0), _const(ng.shape), _const(router.shape)],
        out_specs=(_rows(tm, D_MODEL), _rows(tm, LANE),
                   pl.BlockSpec((R_ROWS, tm), lambda i: (0, i)),
                   pl.BlockSpec((None, R_ROWS, LANE), lambda i: (i, 0, 0))),
        scratch_shapes=[pltpu.VMEM((R_ROWS, LANE), F32)],
        compiler_params=_params("arbitrary"),
        name="moe_route",
    )(h, mod, ng, router)


def _moe_plan(cnt, n_chunks, n_blk):
    i32 = jnp.int32
    before = cnt[:, 0, :N_EXPERTS].astype(i32)
    after = cnt[:, 1, :N_EXPERTS].astype(i32)
    padded = (after[-1] + MOE_XT - 1) // MOE_XT * MOE_XT
    ends = jnp.cumsum(padded).astype(i32)
    base = ends - padded
    used_rows = ends[-1]
    start = base[None, :] + before
    stop = base[None, :] + after
    lo = jnp.arange(n_blk, dtype=i32) * MOE_BLK
    overlap = (start[:, :, None] < lo + MOE_BLK) & (stop[:, :, None] > lo)
    inc = jnp.any(overlap, axis=1)
    fill = (lo < used_rows) & ~jnp.any(inc, axis=0)
    inc_fill = inc.at[0].set(inc[0] | fill)
    n_items = n_chunks * N_EXPERTS + n_blk

    def items(mat):
        idx = jnp.nonzero(mat.reshape(-1), size=n_items, fill_value=-1)[0].astype(i32)
        valid = idx >= 0
        n_valid = jnp.sum(valid).astype(i32)
        idx = jnp.where(valid, idx, idx[jnp.maximum(n_valid - 1, 0)])
        outer = idx // mat.shape[1]
        inner = idx % mat.shape[1]
        prev = jnp.concatenate([jnp.full((1,), -1, i32), outer[:-1]])
        nxt = jnp.concatenate([outer[1:], jnp.full((1,), -1, i32)])
        pos = jnp.arange(n_items, dtype=i32)
        first = valid & (outer != prev)
        last = valid & ((outer != nxt) | (pos == n_valid - 1))
        flags = valid.astype(i32) * F_VALID + first.astype(i32) * F_FIRST + last.astype(i32) * F_LAST
        return outer, inner, flags

    disp_blk, disp_chunk, disp_flags = items(inc_fill.T)
    comb_chunk, comb_blk, comb_flags = items(inc)
    n_xt = n_blk * MOE_BLK // MOE_XT
    t0 = jnp.arange(n_xt, dtype=i32) * MOE_XT
    tile_used = (t0 < used_rows).astype(i32)
    tile_src = jnp.minimum(jnp.arange(n_xt, dtype=i32), used_rows // MOE_XT - 1)
    tile_e = jnp.minimum(jnp.sum(t0[:, None] >= ends[None, :], axis=1), N_EXPERTS - 1).astype(i32)[tile_src]
    return {"base": base, "disp": (disp_chunk, disp_blk, disp_flags), "comb": (comb_chunk, comb_blk, comb_flags),
            "tiles": (tile_e, tile_src, tile_used)}


def _sorted_rows(rcol, rrow, base):
    base_f = base.astype(F32)
    b0 = base_f[rcol[:, R_E0].astype(jnp.int32)]
    b1 = base_f[rcol[:, R_E1].astype(jnp.int32)]
    lane = jnp.arange(LANE)
    rcol = rcol + jnp.where(lane == R_POS0, b0[:, None], 0.0) + jnp.where(lane == R_POS1, b1[:, None], 0.0)
    sub = jnp.arange(R_ROWS)[:, None]
    rrow = rrow + jnp.where(sub == R_POS0, b0[None, :], 0.0) + jnp.where(sub == R_POS1, b1[None, :], 0.0)
    return rcol, rrow


def _moe_dispatch_kernel(c_ref, d_ref, f_ref, m_ref, rrow_ref, x_ref):
    n = pl.program_id(0)
    flag = f_ref[n]

    @pl.when((flag & F_VALID) != 0)
    def _():
        rr = rrow_ref[...]
        row0 = (d_ref[n] * MOE_BLK).astype(F32)
        rel0 = rr[R_POS0:R_POS0 + 1] - row0
        rel1 = rr[R_POS1:R_POS1 + 1] - row0
        ri = lax.broadcasted_iota(jnp.int32, (MOE_BLK, MOE_TC), 0).astype(F32)
        onehot = jnp.where(rel0 == ri, 1.0, jnp.where(rel1 == ri, 1.0, 0.0)).astype(BF16)
        rows = jnp.dot(onehot, m_ref[...], preferred_element_type=F32).astype(x_ref.dtype)

        @pl.when((flag & F_FIRST) != 0)
        def _():
            x_ref[...] = rows

        @pl.when((flag & F_FIRST) == 0)
        def _():
            x_ref[...] = x_ref[...] + rows


def _moe_dispatch(m, rrow, plan, n_blk):
    chunk, blk, flags = plan["disp"]
    n_items = chunk.shape[0]
    grid_spec = pltpu.PrefetchScalarGridSpec(
        num_scalar_prefetch=3,
        grid=(n_items,),
        in_specs=[pl.BlockSpec((MOE_TC, D_MODEL), lambda n, c, d, f: (c[n], 0)),
                  pl.BlockSpec((R_ROWS, MOE_TC), lambda n, c, d, f: (0, c[n]))],
        out_specs=pl.BlockSpec((MOE_BLK, D_MODEL), lambda n, c, d, f: (d[n], 0)),
    )
    return pl.pallas_call(
        _moe_dispatch_kernel,
        out_shape=jax.ShapeDtypeStruct((n_blk * MOE_BLK, D_MODEL), BF16),
        grid_spec=grid_spec,
        compiler_params=_params("arbitrary"),
        name="moe_dispatch",
    )(chunk, blk, flags, m, rrow)


def _moe_experts_kernel(e_ref, src_ref, used_ref, x_ref, wg_ref, wu_ref, wd_ref, y_ref, acc_s):
    t = pl.program_id(0)
    j = pl.program_id(1)
    last = j == pl.num_programs(1) - 1
    used = used_ref[t] != 0

    @pl.when(used)
    def _():
        x = x_ref[...]
        gate = jnp.dot(x, wg_ref[...].astype(BF16), preferred_element_type=F32)
        up = jnp.dot(x, wu_ref[...].astype(BF16), preferred_element_type=F32)
        act = (_silu(gate) * up).astype(BF16)
        part = jnp.dot(act, wd_ref[...].astype(BF16), preferred_element_type=F32)

        @pl.when(j == 0)
        def _():
            acc_s[...] = part

        @pl.when(j != 0)
        def _():
            acc_s[...] += part

        @pl.when(last)
        def _():
            y_ref[...] = acc_s[...].astype(y_ref.dtype)

    @pl.when(jnp.logical_not(used) & last)
    def _():
        y_ref[...] = jnp.zeros_like(y_ref)


def _moe_experts(x, plan, wts):
    tile_e, tile_src, tile_used = plan["tiles"]
    n_xt = tile_e.shape[0]
    li = wts["moe_layer"]
    th = MOE_TH
    nj = MOE_HIDDEN // th

    def hidden(t, j, used):
        return jnp.where(used[t] != 0, j, nj - 1)

    grid_spec = pltpu.PrefetchScalarGridSpec(
        num_scalar_prefetch=3,
        grid=(n_xt, nj),
        in_specs=[pl.BlockSpec((MOE_XT, D_MODEL), lambda t, j, e, s, u: (s[t], 0)),
                  pl.BlockSpec((None, None, D_MODEL, th), lambda t, j, e, s, u: (li, e[t], 0, hidden(t, j, u))),
                  pl.BlockSpec((None, None, D_MODEL, th), lambda t, j, e, s, u: (li, e[t], 0, hidden(t, j, u))),
                  pl.BlockSpec((None, None, th, D_MODEL), lambda t, j, e, s, u: (li, e[t], hidden(t, j, u), 0))],
        out_specs=pl.BlockSpec((MOE_XT, D_MODEL), lambda t, j, e, s, u: (t, 0)),
        scratch_shapes=[pltpu.VMEM((MOE_XT, D_MODEL), F32)],
    )
    return pl.pallas_call(
        _moe_experts_kernel,
        out_shape=jax.ShapeDtypeStruct((n_xt * MOE_XT, D_MODEL), BF16),
        grid_spec=grid_spec,
        compiler_params=_params("arbitrary", "arbitrary"),
        name="moe_experts",
    )(tile_e, tile_src, tile_used, x, wts["moe_g"], wts["moe_u"], wts["moe_d"])


def _moe_combine_kernel(c_ref, d_ref, f_ref, rcol_ref, y_ref, h_ref, mod_ref, out_ref, acc_s):
    n = pl.program_id(0)
    flag = f_ref[n]

    @pl.when((flag & F_VALID) != 0)
    def _():
        rc = rcol_ref[...]
        row0 = (d_ref[n] * MOE_BLK).astype(F32)
        rel0 = rc[:, R_POS0:R_POS0 + 1] - row0
        rel1 = rc[:, R_POS1:R_POS1 + 1] - row0
        li = lax.broadcasted_iota(jnp.int32, (MOE_TC, MOE_BLK), 1).astype(F32)
        gates = jnp.where(rel0 == li, rc[:, R_W0:R_W0 + 1], jnp.where(rel1 == li, rc[:, R_W1:R_W1 + 1], 0.0))
        part = jnp.dot(gates.astype(BF16), y_ref[...], preferred_element_type=F32)

        @pl.when((flag & F_FIRST) != 0)
        def _():
            acc_s[...] = part

        @pl.when((flag & F_FIRST) == 0)
        def _():
            acc_s[...] += part

        @pl.when((flag & F_LAST) != 0)
        def _():
            out_ref[...] = h_ref[...] + mod_ref[5] * acc_s[...]


def _moe_combine(h, mod, rcol, y, plan, tile0):
    chunk, blk, flags = plan["comb"]
    n_items = chunk.shape[0]
    grid_spec = pltpu.PrefetchScalarGridSpec(
        num_scalar_prefetch=3,
        grid=(n_items,),
        in_specs=[pl.BlockSpec((MOE_TC, LANE), lambda n, c, d, f: (c[n], 0)),
                  pl.BlockSpec((MOE_BLK, D_MODEL), lambda n, c, d, f: (d[n], 0)),
                  pl.BlockSpec((MOE_TC, D_MODEL), lambda n, c, d, f: (c[n] + tile0, 0)),
                  pl.BlockSpec((N_MOD, None, 1, D_MODEL),
                               lambda n, c, d, f: (0, _mod_row(c[n] + tile0, MOE_TC), 0, 0))],
        out_specs=pl.BlockSpec((MOE_TC, D_MODEL), lambda n, c, d, f: (c[n] + tile0, 0)),
        scratch_shapes=[pltpu.VMEM((MOE_TC, D_MODEL), F32)],
    )
    return pl.pallas_call(
        _moe_combine_kernel,
        out_shape=jax.ShapeDtypeStruct((N_ROWS, D_MODEL), F32),
        grid_spec=grid_spec,
        input_output_aliases={5: 0},
        compiler_params=_params("arbitrary"),
        name="moe_combine",
    )(chunk, blk, flags, rcol, y, h, mod)


def _moe(h, mod, ng, wts, first_row):
    tile0 = first_row // MOE_TC
    n_chunks = (N_ROWS - first_row) // MOE_TC
    n_blk = (2 * n_chunks * MOE_TC + N_EXPERTS * MOE_XT) // MOE_BLK
    m, rcol, rrow, cnt = _moe_route(h, mod, ng, wts["router"], tile0, n_chunks)
    plan = _moe_plan(cnt, n_chunks, n_blk)
    rcol, rrow = _sorted_rows(rcol, rrow, plan["base"])
    x = _moe_dispatch(m, rrow, plan, n_blk)
    y = _moe_experts(x, plan, wts)
    return _moe_combine(h, mod, rcol, y, plan, tile0)


def _final_kernel(h_ref, g_ref, o_ref):
    o_ref[...] = _rms(h_ref[...], g_ref[...])


def _final_norm(h, g):
    tm = TM
    return pl.pallas_call(
        _final_kernel,
        out_shape=jax.ShapeDtypeStruct((NL_ROWS, D_MODEL), F32),
        grid=(NL_ROWS // tm,),
        in_specs=[_rows(tm, D_MODEL, NC_ROWS // tm), _const(g.shape)],
        out_specs=_rows(tm, D_MODEL),
        compiler_params=_params("arbitrary"),
        name="final_norm",
    )(h, g)


def _row_vec(x):
    return x.reshape(1, -1).astype(F32)


def _even_weights(i, ev_w_in, a_ln_g, a_ln_b, a_ws, a_bs, b_gate_w, b_gate_b, b_norm_g, ev_w_out,
                  ff_w_gate, ff_w_up, ff_w_down):
    w_in = ev_w_in[i]
    qkvr = 2 * B_QK + 2 * B_V
    gate_w = jnp.zeros((2 * B_GATE_RANK, 2 * B_QK), F32)
    gate_w = gate_w.at[:B_GATE_RANK, :B_QK].set(b_gate_w[i, 0]).at[B_GATE_RANK:, B_QK:].set(b_gate_w[i, 1])
    nch = FF_HIDDEN // FF_TH
    return {
        "w_a": w_in[:, :2 * A_WIDTH].astype(BF16),
        "w_q": w_in[:, 2 * A_WIDTH:2 * A_WIDTH + qkvr].astype(BF16),
        "w_gf": w_in[:, 2 * A_WIDTH + qkvr:].astype(BF16),
        "gate_w": gate_w.astype(BF16),
        "gate_b": b_gate_b[i].reshape(1, 2 * B_QK).astype(F32),
        "ln_g": _row_vec(a_ln_g[i]),
        "ln_b": _row_vec(a_ln_b[i]),
        "ws": a_ws[i].astype(BF16),
        "bs": jnp.broadcast_to(a_bs[i][:, :, None], (A_HEADS, A_CHUNK, A_HEAD_DIM)).astype(F32),
        "b_norm_g": _row_vec(b_norm_g[i]),
        "w_out_a": ev_w_out[i, :A_WIDTH].astype(BF16),
        "w_out_b": ev_w_out[i, A_WIDTH:].astype(BF16),
        "ff_g": jnp.transpose(ff_w_gate[i].reshape(D_MODEL, nch, FF_TH), (1, 0, 2)).astype(BF16),
        "ff_u": jnp.transpose(ff_w_up[i].reshape(D_MODEL, nch, FF_TH), (1, 0, 2)).astype(BF16),
        "ff_d": ff_w_down[i].reshape(nch, FF_TH, D_MODEL).astype(BF16),
    }


def _rope_swap_index():
    quarter = C_ROPE // 4
    idx = np.arange(C_ROPE)
    return np.concatenate([idx[quarter:2 * quarter], idx[:quarter], idx[3 * quarter:], idx[2 * quarter:3 * quarter]])


def _odd_weights(i, od_w_in, c_q_norm_g, c_w_uq, c_kv_norm_g, c_w_ukv, od_w_out,
                 moe_router, moe_w_gate, moe_w_up, moe_w_down):
    swap = _rope_swap_index()
    w_in = od_w_in[i]
    low = C_Q_RANK + C_KV_RANK
    k_rope_w = w_in[:, low:]
    pad = jnp.zeros((D_MODEL, LANE - C_ROPE), F32)
    w_in_x = jnp.concatenate([w_in[:, :low], k_rope_w, pad, k_rope_w[:, swap], pad], axis=1)

    uq = c_w_uq[i].reshape(C_Q_RANK, C_HEADS, C_NOPE + C_ROPE)
    zq = jnp.zeros((C_Q_RANK, C_HEADS, HEAD_PAD - C_NOPE - C_ROPE), F32)
    w_q = jnp.concatenate([uq, zq], axis=-1)
    w_q_swap = jnp.concatenate([jnp.zeros((C_Q_RANK, C_HEADS, C_NOPE), F32), uq[:, :, C_NOPE:][:, :, swap], zq], axis=-1)
    ukv = c_w_ukv[i].reshape(C_KV_RANK, C_HEADS, C_NOPE + C_VDIM)
    w_k = jnp.concatenate([ukv[:, :, :C_NOPE], jnp.zeros((C_KV_RANK, C_HEADS, HEAD_PAD - C_NOPE), F32)], axis=-1)
    zv = jnp.zeros((C_KV_RANK, C_HEADS // 2, C_VDIM), F32)
    uv = ukv[:, :, C_NOPE:].reshape(C_KV_RANK, C_HEADS // 2, 2, C_VDIM)
    w_v = jnp.stack([jnp.concatenate([uv[:, :, 0], zv], axis=-1),
                     jnp.concatenate([zv, uv[:, :, 1]], axis=-1)], axis=2)
    v_one = np.zeros((C_HEADS // 2, 2, HEAD_PAD), np.float32)
    v_one[:, 0, C_VDIM] = 1.0
    v_one[:, 1, 0] = 1.0
    place = np.zeros((LANE, C_HEADS, HEAD_PAD), np.float32)
    for r in range(C_ROPE):
        place[r, :, C_NOPE + r] = 1.0
    router = jnp.concatenate([moe_router[i], jnp.zeros((D_MODEL, LANE - N_EXPERTS), F32)], axis=1)
    return {
        "w_in": w_in_x.astype(BF16),
        "q_norm_g": _row_vec(c_q_norm_g[i]),
        "kv_norm_g": _row_vec(c_kv_norm_g[i]),
        "w_q": w_q.reshape(C_Q_RANK, C_QK_PAD).astype(BF16),
        "w_q_swap": w_q_swap.reshape(C_Q_RANK, C_QK_PAD).astype(BF16),
        "w_k": w_k.reshape(C_KV_RANK, C_QK_PAD).astype(BF16),
        "w_v": w_v.reshape(C_KV_RANK, C_QK_PAD).astype(BF16),
        "place": jnp.asarray(place.reshape(LANE, C_QK_PAD), BF16),
        "v_one": jnp.asarray(v_one.reshape(1, C_QK_PAD), F32),
        "w_out": od_w_out[i].astype(BF16),
        "router": router,
        "moe_layer": i,
        "moe_g": moe_w_gate,
        "moe_u": moe_w_up,
        "moe_d": moe_w_down,
    }


def _rope_tables(tm):
    quarter = C_ROPE // 4
    half = C_ROPE // 2
    inv_freq = ROPE_BASE ** (-jnp.arange(0, half, 2, dtype=F32) / half)
    pos = jnp.arange(SEQ, dtype=jnp.int32)
    ang_r = (pos // GRID_W).astype(F32)[:, None] * inv_freq
    ang_c = (pos % GRID_W).astype(F32)[:, None] * inv_freq
    cos = jnp.concatenate([jnp.cos(ang_r)] * 2 + [jnp.cos(ang_c)] * 2, axis=1)
    sin = jnp.concatenate([-jnp.sin(ang_r), jnp.sin(ang_r), -jnp.sin(ang_c), jnp.sin(ang_c)], axis=1)
    cos = jnp.concatenate([cos, jnp.ones((tm, C_ROPE), F32)], axis=0)
    sin = jnp.concatenate([sin, jnp.zeros((tm, C_ROPE), F32)], axis=0)
    rows = SEQ + tm
    ones = jnp.ones((rows, C_NOPE), F32)
    z = lambda w: jnp.zeros((rows, w), F32)
    cq = jnp.concatenate([ones, cos, z(HEAD_PAD - C_NOPE - C_ROPE)], axis=1)
    sq = jnp.concatenate([z(C_NOPE), sin, z(HEAD_PAD - C_NOPE - C_ROPE)], axis=1)
    ck = jnp.concatenate([cos, z(LANE - C_ROPE)], axis=1)
    sk = jnp.concatenate([sin, z(LANE - C_ROPE)], axis=1)
    return cq, sq, ck, sk


def kernel(x, c, ctx, c_ctx, mod_w, mod_b, norm_mix_g, norm_ffn_g, final_g, ev_w_in, a_ln_g, a_ln_b, a_ws, a_bs, b_gate_w, b_gate_b, b_norm_g, ev_w_out, od_w_in, c_q_norm_g, c_w_uq, c_kv_norm_g, c_w_ukv, od_w_out, ff_w_gate, ff_w_up, ff_w_down, moe_router, moe_w_gate, moe_w_up, moe_w_down):
    h = jnp.concatenate([ctx.reshape(NC_ROWS, D_MODEL), x.reshape(NL_ROWS, D_MODEL)], axis=0)
    cond = jnp.concatenate([c, c_ctx[None, :], jnp.zeros((MOD_PAD - MOD_ROWS, D_MODEL), F32)], axis=0)
    mod = _modulation(cond, mod_w, mod_b)
    rope = _rope_tables(TM)

    for layer in range(DEPTH):
        i = layer // 2
        need_ctx = layer < DEPTH - 1
        ng_mix = _row_vec(norm_mix_g[layer])
        ng_ffn = _row_vec(norm_ffn_g[layer])
        if layer % 2 == 0:
            wts = _even_weights(i, ev_w_in, a_ln_g, a_ln_b, a_ws, a_bs, b_gate_w, b_gate_b, b_norm_g,
                                ev_w_out, ff_w_gate, ff_w_up, ff_w_down)
            ya, qkv, r, g = _even_in(h, mod[layer], ng_mix, wts)
            o_f, o_b = _gla(qkv, g)
            h = _even_out(h, mod[layer], o_f, o_b, r, ya, wts)
            h = _ffn(h, mod[layer], ng_ffn, wts)
        else:
            wts = _odd_weights(i, od_w_in, c_q_norm_g, c_w_uq, c_kv_norm_g, c_w_ukv, od_w_out,
                               moe_router, moe_w_gate, moe_w_up, moe_w_down)
            q, k, v = _odd_in(h, mod[layer], ng_mix, wts, rope)
            o = _attention(q, k, v, latent=True)
            if need_ctx:
                o = _attention(q, k, v, latent=False, prev=o)
            first = 0 if need_ctx else NC_ROWS
            h = _odd_out(h, mod[layer], o, wts["w_out"], first // TM, (N_ROWS - first) // TM)
            h = _moe(h, mod[layer], ng_ffn, wts, first)
    out = _final_norm(h, _row_vec(final_g))
    return out.reshape(BATCH, SEQ, D_MODEL)
```

```python
import functools

import numpy as np
import jax
import jax.numpy as jnp
from jax import lax
from jax.experimental import pallas as pl
from jax.experimental.pallas import tpu as pltpu

D_MODEL = 1024
BATCH = 8
SEQ = 2048
DEPTH = 4
GRID_W = 64
CTX_LEN = 256
EPS = 1e-6
N_MOD = 6

A_HEADS = 4
A_HEAD_DIM = 128
A_CHUNK = 128
A_WIDTH = A_HEADS * A_HEAD_DIM

B_HEADS = 4
B_DK = 64
B_DV = 128
B_QK = B_HEADS * B_DK
B_V = B_HEADS * B_DV
B_GATE_RANK = 16
B_GATE_TEMP = 16.0
B_CHUNK = 64

C_HEADS = 16
C_NOPE = 64
C_ROPE = 32
C_VDIM = 64
C_Q_RANK = 256
C_KV_RANK = 256
C_SCALE = (C_NOPE + C_ROPE) ** -0.5
LOG2_E = 1.4426950408889634
ROPE_BASE = 10000.0

FF_HIDDEN = 2816
N_EXPERTS = 8
MOE_HIDDEN = 3584

NC_ROWS = BATCH * CTX_LEN
NL_ROWS = BATCH * SEQ
N_ROWS = NC_ROWS + NL_ROWS
MOD_ROWS = BATCH + 1
MOD_PAD = 16

LANE = 128
HEAD_PAD = 128
C_QK_PAD = C_HEADS * HEAD_PAD

TM = 512
GLA_BLK = 256
ATT_TQ = 1024
ATT_HG = 4
FF_TH = 256
MOE_TC = 512
MOE_BLK = 256
MOE_XT = 1024
MOE_TH = 512
VMEM_LIMIT = 56 * 1024 * 1024

F32 = jnp.float32
BF16 = jnp.bfloat16


def _mm(a, b):
    return jnp.dot(a.astype(BF16), b.astype(BF16), preferred_element_type=F32)


def _mm_nt(a, b):
    return lax.dot_general(a.astype(BF16), b.astype(BF16), (((1,), (1,)), ((), ())),
                           preferred_element_type=F32)


def _mm_tn(a, b):
    return lax.dot_general(a.astype(BF16), b.astype(BF16), (((0,), (0,)), ((), ())),
                           preferred_element_type=F32)


def _sigmoid(x):
    return 1.0 / (1.0 + jnp.exp(-x))


def _silu(x):
    return x * _sigmoid(x)


def _gelu(x):
    return 0.5 * x * (1.0 + jnp.tanh(np.sqrt(2.0 / np.pi) * (x + 0.044715 * (x * x * x))))


def _rms(x, g):
    return x * lax.rsqrt(jnp.mean(x * x, axis=-1, keepdims=True) + EPS) * g


def _split_bf16(x):
    hi = x.astype(BF16)
    lo = (x - hi.astype(F32)).astype(BF16)
    return hi, lo


def _rows(tm, width, tile0=0, col=0):
    return pl.BlockSpec((tm, width), lambda i: (i + tile0, col))


def _const(shape):
    zeros = (0,) * len(shape)
    return pl.BlockSpec(shape, lambda *_: zeros, pipeline_mode=pl.Buffered(1))


def _mod_row(tile, tm):
    return jnp.where(tile < NC_ROWS // tm, BATCH, (tile * tm - NC_ROWS) // SEQ)


def _mod_spec(tm, tile0=0):
    return pl.BlockSpec((N_MOD, None, 1, D_MODEL), lambda i: (0, _mod_row(i + tile0, tm), 0, 0))


def _params(*sem):
    return pltpu.CompilerParams(dimension_semantics=sem, vmem_limit_bytes=VMEM_LIMIT)


def _mod_kernel(c_ref, w_ref, b_ref, o_ref):
    o_ref[...] = _mm(_silu(c_ref[...]), w_ref[...]) + b_ref[...]


def _modulation(cond, mod_w, mod_b):
    tn = 1536
    out = pl.pallas_call(
        _mod_kernel,
        out_shape=jax.ShapeDtypeStruct((DEPTH, MOD_PAD, N_MOD * D_MODEL), F32),
        grid=(DEPTH, N_MOD * D_MODEL // tn),
        in_specs=[
            pl.BlockSpec((MOD_PAD, D_MODEL), lambda l, j: (0, 0)),
            pl.BlockSpec((None, D_MODEL, tn), lambda l, j: (l, 0, j)),
            pl.BlockSpec((None, 1, tn), lambda l, j: (l, 0, j)),
        ],
        out_specs=pl.BlockSpec((None, MOD_PAD, tn), lambda l, j: (l, 0, j)),
        compiler_params=_params("arbitrary", "arbitrary"),
        name="modulation",
    )(cond, mod_w, mod_b.reshape(DEPTH, 1, N_MOD * D_MODEL))
    out = out[:, :MOD_ROWS].reshape(DEPTH, MOD_ROWS, N_MOD, 1, D_MODEL)
    return jnp.transpose(out, (0, 2, 1, 3, 4))


def _even_in_kernel(h_ref, mod_ref, ng_ref, wa_ref, wq_ref, wgf_ref, gw_ref, gb_ref,
                    lng_ref, lnb_ref, ws_ref, bs_ref, ya_ref, qkv_ref, r_ref, g_ref):
    tm = h_ref.shape[0]
    n = _rms(h_ref[...], ng_ref[...]) * (1.0 + mod_ref[1]) + mod_ref[0]
    nb = n.astype(BF16)

    ua_va = jnp.dot(nb, wa_ref[...], preferred_element_type=F32)
    u = _gelu(ua_va[:, :A_WIDTH])
    v = _gelu(ua_va[:, A_WIDTH:])
    mu = jnp.mean(v, axis=-1, keepdims=True)
    vc = v - mu
    var = jnp.mean(vc * vc, axis=-1, keepdims=True)
    v = (vc * lax.rsqrt(var + EPS) * lng_ref[...] + lnb_ref[...]).astype(BF16)
    for c in range(tm // A_CHUNK):
        rows = slice(c * A_CHUNK, (c + 1) * A_CHUNK)
        for h in range(A_HEADS):
            cols = slice(h * A_HEAD_DIM, (h + 1) * A_HEAD_DIM)
            mixed = jnp.dot(ws_ref[h], v[rows, cols], preferred_element_type=F32) + bs_ref[h]
            ya_ref[rows, cols] = (u[rows, cols] * mixed).astype(ya_ref.dtype)

    p = jnp.dot(nb, wq_ref[...], preferred_element_type=F32)
    qkv_ref[...] = p[:, :2 * B_QK + B_V].astype(qkv_ref.dtype)
    r_ref[...] = p[:, 2 * B_QK + B_V:].astype(r_ref.dtype)
    gf = jnp.dot(nb, wgf_ref[...], preferred_element_type=F32)
    logit = _mm(gf, gw_ref[...]) + gb_ref[...]
    log_sig = jnp.minimum(logit, 0.0) - jnp.log(1.0 + jnp.exp(-jnp.abs(logit)))
    g_ref[...] = log_sig / B_GATE_TEMP


def _even_in(h, mod, ng, wts):
    tm = TM
    n_tiles = N_ROWS // tm
    qkv_w = 2 * B_QK + B_V
    consts = [ng, wts["w_a"], wts["w_q"], wts["w_gf"], wts["gate_w"], wts["gate_b"],
              wts["ln_g"], wts["ln_b"], wts["ws"], wts["bs"]]
    return pl.pallas_call(
        _even_in_kernel,
        out_shape=(jax.ShapeDtypeStruct((N_ROWS, A_WIDTH), BF16),
                   jax.ShapeDtypeStruct((N_ROWS, qkv_w), BF16),
                   jax.ShapeDtypeStruct((N_ROWS, B_V), BF16),
                   jax.ShapeDtypeStruct((N_ROWS, 2 * B_QK), F32)),
        grid=(n_tiles,),
        in_specs=[_rows(tm, D_MODEL), _mod_spec(tm)] + [_const(a.shape) for a in consts],
        out_specs=(_rows(tm, A_WIDTH), _rows(tm, qkv_w), _rows(tm, B_V), _rows(tm, 2 * B_QK)),
        compiler_params=_params("arbitrary"),
        name="even_in",
    )(h, mod, *consts)


def _gla_direction(q, k, v, g, st_ref, reverse):
    n = GLA_BLK
    nch = n // B_CHUNK
    ri = lax.broadcasted_iota(jnp.int32, (n, n), 0)
    ci = lax.broadcasted_iota(jnp.int32, (n, n), 1)
    same = (ri // B_CHUNK) == (ci // B_CHUNK)
    tri = same & ((ci >= ri) if reverse else (ci <= ri))
    tri_b = jnp.where(tri, 1.0, 0.0).astype(BF16)
    same_b = jnp.where(same, 1.0, 0.0).astype(BF16)

    g_hi, g_lo = _split_bf16(g)
    dot = functools.partial(jnp.dot, preferred_element_type=F32)
    b = dot(tri_b, g_hi) + dot(tri_b, g_lo)
    b_tot = dot(same_b, g_hi) + dot(same_b, g_lo)
    q_dec = (q * (B_DK ** -0.5)) * jnp.exp(b)
    k_inv = k * jnp.exp(-b)
    k_dec = k * jnp.exp(b_tot - b)
    decay = jnp.exp(b_tot)

    lane = lax.broadcasted_iota(jnp.int32, (1, LANE), 1)
    even_lane = lane < B_DK

    outs = [None] * B_HEADS
    order = list(range(nch))[::-1] if reverse else list(range(nch))
    for p in range(B_HEADS // 2):
        lanes = slice(p * LANE, (p + 1) * LANE)
        qp = q_dec[:, lanes]
        q_heads = (jnp.where(even_lane, qp, 0.0).astype(BF16), jnp.where(even_lane, 0.0, qp).astype(BF16))
        k_inv_p = k_inv[:, lanes].astype(BF16)
        k_dec_p = k_dec[:, lanes].astype(BF16)
        intra = []
        for hh in range(2):
            h = 2 * p + hh
            scores = _mm_nt(q_heads[hh], k_inv_p)
            scores = jnp.where(tri, scores, 0.0)
            intra.append(_mm(scores, v[:, h * B_DV:(h + 1) * B_DV]))
        st = st_ref[p]
        inter = [[None] * nch for _ in range(2)]
        for c in order:
            rows = slice(c * B_CHUNK, (c + 1) * B_CHUNK)
            st_b = st.astype(BF16)
            for hh in range(2):
                inter[hh][c] = _mm_nt(q_heads[hh][rows], st_b)
            upd_e = _mm_tn(v[rows, (2 * p) * B_DV:(2 * p + 1) * B_DV], k_dec_p[rows])
            upd_o = _mm_tn(v[rows, (2 * p + 1) * B_DV:(2 * p + 2) * B_DV], k_dec_p[rows])
            st = st * decay[c * B_CHUNK:c * B_CHUNK + 1, lanes] + jnp.where(even_lane, upd_e, upd_o)
        st_ref[p] = st
        for hh in range(2):
            outs[2 * p + hh] = intra[hh] + jnp.concatenate(inter[hh], axis=0)
    return outs


def _gla_kernel(qf_ref, kf_ref, vf_ref, gf_ref, qb_ref, kb_ref, vb_ref, gb_ref,
                of_ref, ob_ref, stf_ref, stb_ref):
    @pl.when(pl.program_id(1) == 0)
    def _():
        stf_ref[...] = jnp.zeros_like(stf_ref)
        stb_ref[...] = jnp.zeros_like(stb_ref)

    outs = _gla_direction(qf_ref[...].astype(F32), kf_ref[...].astype(F32), vf_ref[...], gf_ref[...], stf_ref, False)
    for h in range(B_HEADS):
        of_ref[:, h * B_DV:(h + 1) * B_DV] = outs[h]
    outs = _gla_direction(qb_ref[...].astype(F32), kb_ref[...].astype(F32), vb_ref[...], gb_ref[...], stb_ref, True)
    for h in range(B_HEADS):
        ob_ref[:, h * B_DV:(h + 1) * B_DV] = outs[h]


def _gla(qkv, g):
    nb_ctx = CTX_LEN // GLA_BLK
    nb_lat = SEQ // GLA_BLK
    nblk = nb_ctx + nb_lat
    ctx_blocks = NC_ROWS // GLA_BLK

    def fwd_row(b, j):
        return jnp.where(j < nb_ctx, b * nb_ctx + j, ctx_blocks + b * nb_lat + (j - nb_ctx))

    def bwd_row(b, j):
        return jnp.where(j < nb_ctx, b * nb_ctx + (nb_ctx - 1 - j),
                         ctx_blocks + b * nb_lat + (nblk - 1 - j))

    def spec(width, col, row):
        return pl.BlockSpec((GLA_BLK, width), lambda b, j: (row(b, j), col))

    ins, in_specs = [], []
    for d, row in enumerate((fwd_row, bwd_row)):
        ins += [qkv, qkv, qkv, g]
        in_specs += [spec(B_QK, 0, row), spec(B_QK, 1, row), spec(B_V, 1, row), spec(B_QK, d, row)]
    return pl.pallas_call(
        _gla_kernel,
        out_shape=(jax.ShapeDtypeStruct((N_ROWS, B_V), F32),) * 2,
        grid=(BATCH, nblk),
        in_specs=in_specs,
        out_specs=(spec(B_V, 0, fwd_row), spec(B_V, 0, bwd_row)),
        scratch_shapes=[pltpu.VMEM((B_HEADS // 2, B_DV, LANE), F32)] * 2,
        compiler_params=_params("arbitrary", "arbitrary"),
        name="gla",
    )(*ins)


def _even_out_kernel(h_ref, mod_ref, of_ref, ob_ref, r_ref, ya_ref, bng_ref, woa_ref, wob_ref, out_ref):
    o = of_ref[...] + ob_ref[...]
    parts = [_rms(o[:, h * B_DV:(h + 1) * B_DV], bng_ref[...]) for h in range(B_HEADS)]
    yb = _silu(r_ref[...].astype(F32)) * jnp.concatenate(parts, axis=-1)
    y = jnp.dot(ya_ref[...], woa_ref[...], preferred_element_type=F32) + _mm(yb, wob_ref[...])
    out_ref[...] = h_ref[...] + mod_ref[2] * y


def _even_out(h, mod, o_f, o_b, r, ya, wts):
    tm = TM
    consts = [wts["b_norm_g"], wts["w_out_a"], wts["w_out_b"]]
    return pl.pallas_call(
        _even_out_kernel,
        out_shape=jax.ShapeDtypeStruct((N_ROWS, D_MODEL), F32),
        grid=(N_ROWS // tm,),
        in_specs=[_rows(tm, D_MODEL), _mod_spec(tm), _rows(tm, B_V), _rows(tm, B_V), _rows(tm, B_V),
                  _rows(tm, A_WIDTH)] + [_const(a.shape) for a in consts],
        out_specs=_rows(tm, D_MODEL),
        input_output_aliases={0: 0},
        compiler_params=_params("arbitrary"),
        name="even_out",
    )(h, mod, o_f, o_b, r, ya, *consts)


def _ffn_kernel(h_ref, mod_ref, ng_ref, wg_ref, wu_ref, wd_ref, out_ref):
    x = h_ref[...]
    m = (_rms(x, ng_ref[...]) * (1.0 + mod_ref[4]) + mod_ref[3]).astype(BF16)
    acc = jnp.zeros(x.shape, F32)
    for j in range(wg_ref.shape[0]):
        gate = jnp.dot(m, wg_ref[j], preferred_element_type=F32)
        up = jnp.dot(m, wu_ref[j], preferred_element_type=F32)
        act = (_silu(gate) * up).astype(BF16)
        acc = acc + jnp.dot(act, wd_ref[j], preferred_element_type=F32)
    out_ref[...] = x + mod_ref[5] * acc


def _ffn(h, mod, ng, wts):
    tm = TM
    consts = [ng, wts["ff_g"], wts["ff_u"], wts["ff_d"]]
    return pl.pallas_call(
        _ffn_kernel,
        out_shape=jax.ShapeDtypeStruct((N_ROWS, D_MODEL), F32),
        grid=(N_ROWS // tm,),
        in_specs=[_rows(tm, D_MODEL), _mod_spec(tm)] + [_const(a.shape) for a in consts],
        out_specs=_rows(tm, D_MODEL),
        input_output_aliases={0: 0},
        compiler_params=_params("arbitrary"),
        name="ffn",
    )(h, mod, *consts)


def _odd_in_kernel(h_ref, mod_ref, ng_ref, win_ref, qg_ref, kvg_ref, wq_ref, wqs_ref, wk_ref, wv_ref,
                   place_ref, vone_ref, cq_ref, sq_ref, ck_ref, sk_ref, q_ref, k_ref, v_ref):
    n = _rms(h_ref[...], ng_ref[...]) * (1.0 + mod_ref[1]) + mod_ref[0]
    p = jnp.dot(n.astype(BF16), win_ref[...], preferred_element_type=F32)
    cq = _rms(p[:, :C_Q_RANK], qg_ref[...]).astype(BF16)
    ckv = _rms(p[:, C_Q_RANK:C_Q_RANK + C_KV_RANK], kvg_ref[...]).astype(BF16)
    base = C_Q_RANK + C_KV_RANK
    kr = p[:, base:base + LANE] * ck_ref[...] + p[:, base + LANE:base + 2 * LANE] * sk_ref[...]

    q_plain = jnp.dot(cq, wq_ref[...], preferred_element_type=F32)
    q_swap = jnp.dot(cq, wqs_ref[...], preferred_element_type=F32)
    cq_t, sq_t = cq_ref[...], sq_ref[...]
    for h in range(C_HEADS):
        cols = slice(h * HEAD_PAD, (h + 1) * HEAD_PAD)
        q_ref[:, cols] = ((q_plain[:, cols] * cq_t + q_swap[:, cols] * sq_t) * (C_SCALE * LOG2_E)).astype(q_ref.dtype)
    k_cat = jnp.dot(ckv, wk_ref[...], preferred_element_type=F32) + _mm(kr, place_ref[...])
    k_ref[...] = k_cat.astype(k_ref.dtype)
    v_ref[...] = (jnp.dot(ckv, wv_ref[...], preferred_element_type=F32) + vone_ref[...]).astype(v_ref.dtype)


def _odd_in(h, mod, ng, wts, rope):
    tm = TM
    n_tiles = N_ROWS // tm
    consts = [ng, wts["w_in"], wts["q_norm_g"], wts["kv_norm_g"], wts["w_q"], wts["w_q_swap"],
              wts["w_k"], wts["w_v"], wts["place"], wts["v_one"]]
    ctx_tiles = NC_ROWS // tm
    lat_tiles = SEQ // tm

    def rope_spec():
        return pl.BlockSpec((tm, LANE), lambda i: (jnp.where(i < ctx_tiles, lat_tiles, (i - ctx_tiles) % lat_tiles), 0))

    return pl.pallas_call(
        _odd_in_kernel,
        out_shape=(jax.ShapeDtypeStruct((N_ROWS, C_QK_PAD), BF16),) * 3,
        grid=(n_tiles,),
        in_specs=[_rows(tm, D_MODEL), _mod_spec(tm)] + [_const(a.shape) for a in consts] + [rope_spec()] * 4,
        out_specs=(_rows(tm, C_QK_PAD),) * 3,
        compiler_params=_params("arbitrary"),
        name="odd_in",
    )(h, mod, *consts, *rope)


def _attn_kernel(*refs, n_pieces):
    q_ref = refs[0]
    k_refs = refs[1:1 + n_pieces]
    v_refs = refs[1 + n_pieces:1 + 2 * n_pieces]
    o_ref = refs[-1]
    for pair in range(ATT_HG // 2):
        acc = None
        for hh in range(2):
            cols = slice((2 * pair + hh) * HEAD_PAD, (2 * pair + hh + 1) * HEAD_PAD)
            q = q_ref[:, cols]
            s = [lax.dot_general(q, k[:, cols], (((1,), (1,)), ((), ())), preferred_element_type=F32)
                 for k in k_refs]
            m = functools.reduce(jnp.maximum, [jnp.max(x, axis=-1, keepdims=True) for x in s])
            e = [jnp.exp2(x - m) for x in s]
            o = functools.reduce(jnp.add, [jnp.dot(x.astype(BF16), v[:, cols], preferred_element_type=F32)
                                           for x, v in zip(e, v_refs)])
            ones_lane = C_VDIM if hh == 0 else 0
            l = o[:, ones_lane:ones_lane + 1]
            lane = lax.broadcasted_iota(jnp.int32, o.shape, 1)
            own = (lane < C_VDIM) if hh == 0 else (lane >= C_VDIM)
            o = jnp.where(own, o * (1.0 / l), 0.0)
            acc = o if acc is None else acc + o
        o_ref[:, pair * LANE:(pair + 1) * LANE] = acc.astype(o_ref.dtype)


def _attention(q, k, v, latent, prev=None):
    width = ATT_HG * HEAD_PAD
    n_hg = C_HEADS // ATT_HG
    if latent:
        tq = ATT_TQ
        nq = SEQ // tq
        q_spec = pl.BlockSpec((tq, width), lambda b, g, t: ((NC_ROWS + b * SEQ) // tq + t, g))
        kv_specs = [pl.BlockSpec((CTX_LEN, width), lambda b, g, t: (b, g)),
                    pl.BlockSpec((SEQ, width), lambda b, g, t: (NC_ROWS // SEQ + b, g))]
    else:
        tq = CTX_LEN
        nq = 1
        q_spec = pl.BlockSpec((tq, width), lambda b, g, t: (b, g))
        kv_specs = [pl.BlockSpec((CTX_LEN, width), lambda b, g, t: (b, g))]
    n_pieces = len(kv_specs)
    o_spec = pl.BlockSpec((tq, ATT_HG * C_VDIM), q_spec.index_map)
    ins = [q] + [k] * n_pieces + [v] * n_pieces
    in_specs = [q_spec] + kv_specs + kv_specs
    aliases = {}
    if prev is not None:
        aliases = {len(ins): 0}
        ins.append(prev)
        in_specs.append(pl.BlockSpec(memory_space=pl.ANY))
    return pl.pallas_call(
        functools.partial(_attn_kernel, n_pieces=n_pieces),
        out_shape=jax.ShapeDtypeStruct((N_ROWS, C_HEADS * C_VDIM), BF16),
        grid=(BATCH, n_hg, nq),
        in_specs=in_specs,
        out_specs=o_spec,
        input_output_aliases=aliases,
        compiler_params=_params("arbitrary", "arbitrary", "arbitrary"),
        name="attn_latent" if latent else "attn_context",
    )(*ins)


def _odd_out_kernel(h_ref, mod_ref, o_ref, w_ref, out_ref):
    y = jnp.dot(o_ref[...], w_ref[...], preferred_element_type=F32)
    out_ref[...] = h_ref[...] + mod_ref[2] * y


def _odd_out(h, mod, o, w_out, tile0, n_tiles):
    tm = TM
    return pl.pallas_call(
        _odd_out_kernel,
        out_shape=jax.ShapeDtypeStruct((N_ROWS, D_MODEL), F32),
        grid=(n_tiles,),
        in_specs=[_rows(tm, D_MODEL, tile0), _mod_spec(tm, tile0), _rows(tm, C_HEADS * C_VDIM, tile0),
                  _const(w_out.shape)],
        out_specs=_rows(tm, D_MODEL, tile0),
        input_output_aliases={0: 0},
        compiler_params=_params("arbitrary"),
        name="odd_out",
    )(h, mod, o, w_out)


def _route(m, router):
    m_hi, m_lo = _split_bf16(m)
    r_hi, r_lo = _split_bf16(router)
    dot = functools.partial(jnp.dot, preferred_element_type=F32)
    logits = dot(m_hi, r_hi) + (dot(m_hi, r_lo) + dot(m_lo, r_hi))
    lane = lax.broadcasted_iota(jnp.int32, logits.shape, 1)
    neg = -jnp.inf
    l1 = jnp.where(lane < N_EXPERTS, logits, neg)
    m1 = jnp.max(l1, axis=-1, keepdims=True)
    i1 = jnp.min(jnp.where(l1 == m1, lane, LANE), axis=-1, keepdims=True)
    l2 = jnp.where(lane == i1, neg, l1)
    m2 = jnp.max(l2, axis=-1, keepdims=True)
    i2 = jnp.min(jnp.where(l2 == m2, lane, LANE), axis=-1, keepdims=True)
    e2 = jnp.exp(m2 - m1)
    w1 = 1.0 / (1.0 + e2)
    w2 = e2 / (1.0 + e2)
    return i1, i2, w1, w2


R_E0, R_E1, R_RANK0, R_RANK1, R_W0, R_W1 = range(6)
R_POS0, R_POS1 = R_RANK0, R_RANK1
R_ROWS = 8
F_VALID, F_FIRST, F_LAST = 1, 2, 4


def _moe_route_kernel(h_ref, mod_ref, ng_ref, rt_ref, m_ref, rcol_ref, rrow_ref, cnt_ref, carry_s):
    @pl.when(pl.program_id(0) == 0)
    def _():
        carry_s[...] = jnp.zeros_like(carry_s)

    m = _rms(h_ref[...], ng_ref[...]) * (1.0 + mod_ref[4]) + mod_ref[3]
    m_ref[...] = m.astype(m_ref.dtype)
    i1, i2, w1, w2 = _route(m, rt_ref[...])
    tm = m.shape[0]
    lane = lax.broadcasted_iota(jnp.int32, (tm, LANE), 1)
    picked = jnp.where(lane == i1, 1.0, jnp.where(lane == i2, 1.0, 0.0))
    ri = lax.broadcasted_iota(jnp.int32, (tm, tm), 0)
    ci = lax.broadcasted_iota(jnp.int32, (tm, tm), 1)
    earlier = jnp.where(ci < ri, 1.0, 0.0).astype(BF16)
    before = carry_s[0:1, :]
    rank_all = jnp.dot(earlier, picked.astype(BF16), preferred_element_type=F32) + before
    rank0 = jnp.sum(jnp.where(lane == i1, rank_all, 0.0), axis=-1, keepdims=True)
    rank1 = jnp.sum(jnp.where(lane == i2, rank_all, 0.0), axis=-1, keepdims=True)
    after = before + jnp.sum(picked, axis=0, keepdims=True)
    carry_s[...] = jnp.broadcast_to(after, carry_s.shape)
    sub = lax.broadcasted_iota(jnp.int32, cnt_ref.shape, 0)
    cnt_ref[...] = jnp.where(sub == 0, before, jnp.where(sub == 1, after, 0.0))

    fields = {R_E0: i1.astype(F32), R_E1: i2.astype(F32), R_RANK0: rank0, R_RANK1: rank1, R_W0: w1, R_W1: w2}
    rcol = jnp.zeros((tm, LANE), F32)
    for k, val in fields.items():
        rcol = jnp.where(lane == k, val, rcol)
    rcol_ref[...] = rcol
    rrow_ref[...] = jnp.transpose(rcol)[:R_ROWS, :]


def _moe_route(h, mod, ng, router, tile0, n_chunks):
    tm = MOE_TC
    rows = n_chunks * tm
    return pl.pallas_call(
        _moe_route_kernel,
        out_shape=(jax.ShapeDtypeStruct((rows, D_MODEL), BF16),
                   jax.ShapeDtypeStruct((rows, LANE), F32),
                   jax.ShapeDtypeStruct((R_ROWS, rows), F32),
                   jax.ShapeDtypeStruct((n_chunks, R_ROWS, LANE), F32)),
        grid=(n_chunks,),
        in_specs=[_rows(tm, D_MODEL, tile0), _mod_spec(tm, tile0), _const(ng.shape), _const(router.shape)],
        out_specs=(_rows(tm, D_MODEL), _rows(tm, LANE),
                   pl.BlockSpec((R_ROWS, tm), lambda i: (0, i)),
                   pl.BlockSpec((None, R_ROWS, LANE), lambda i: (i, 0, 0))),
        scratch_shapes=[pltpu.VMEM((R_ROWS, LANE), F32)],
        compiler_params=_params("arbitrary"),
        name="moe_route",
    )(h, mod, ng, router)


def _moe_plan(cnt, n_chunks, n_blk):
    i32 = jnp.int32
    before = cnt[:, 0, :N_EXPERTS].astype(i32)
    after = cnt[:, 1, :N_EXPERTS].astype(i32)
    padded = (after[-1] + MOE_XT - 1) // MOE_XT * MOE_XT
    ends = jnp.cumsum(padded).astype(i32)
    base = ends - padded
    used_rows = ends[-1]
    start = base[None, :] + before
    stop = base[None, :] + after
    lo = jnp.arange(n_blk, dtype=i32) * MOE_BLK
    overlap = (start[:, :, None] < lo + MOE_BLK) & (stop[:, :, None] > lo)
    inc = jnp.any(overlap, axis=1)
    fill = (lo < used_rows) & ~jnp.any(inc, axis=0)
    inc_fill = inc.at[0].set(inc[0] | fill)
    n_items = n_chunks * N_EXPERTS + n_blk

    def items(mat):
        idx = jnp.nonzero(mat.reshape(-1), size=n_items, fill_value=-1)[0].astype(i32)
        valid = idx >= 0
        n_valid = jnp.sum(valid).astype(i32)
        idx = jnp.where(valid, idx, idx[jnp.maximum(n_valid - 1, 0)])
        outer = idx // mat.shape[1]
        inner = idx % mat.shape[1]
        prev = jnp.concatenate([jnp.full((1,), -1, i32), outer[:-1]])
        nxt = jnp.concatenate([outer[1:], jnp.full((1,), -1, i32)])
        pos = jnp.arange(n_items, dtype=i32)
        first = valid & (outer != prev)
        last = valid & ((outer != nxt) | (pos == n_valid - 1))
        flags = valid.astype(i32) * F_VALID + first.astype(i32) * F_FIRST + last.astype(i32) * F_LAST
        return outer, inner, flags

    disp_blk, disp_chunk, disp_flags = items(inc_fill.T)
    comb_chunk, comb_blk, comb_flags = items(inc)
    n_xt = n_blk * MOE_BLK // MOE_XT
    t0 = jnp.arange(n_xt, dtype=i32) * MOE_XT
    tile_used = (t0 < used_rows).astype(i32)
    tile_src = jnp.minimum(jnp.arange(n_xt, dtype=i32), used_rows // MOE_XT - 1)
    tile_e = jnp.minimum(jnp.sum(t0[:, None] >= ends[None, :], axis=1), N_EXPERTS - 1).astype(i32)[tile_src]
    return {"base": base, "disp": (disp_chunk, disp_blk, disp_flags), "comb": (comb_chunk, comb_blk, comb_flags),
            "tiles": (tile_e, tile_src, tile_used)}


def _sorted_rows(rcol, rrow, base):
    base_f = base.astype(F32)
    b0 = base_f[rcol[:, R_E0].astype(jnp.int32)]
    b1 = base_f[rcol[:, R_E1].astype(jnp.int32)]
    lane = jnp.arange(LANE)
    rcol = rcol + jnp.where(lane == R_POS0, b0[:, None], 0.0) + jnp.where(lane == R_POS1, b1[:, None], 0.0)
    sub = jnp.arange(R_ROWS)[:, None]
    rrow = rrow + jnp.where(sub == R_POS0, b0[None, :], 0.0) + jnp.where(sub == R_POS1, b1[None, :], 0.0)
    return rcol, rrow


def _moe_dispatch_kernel(c_ref, d_ref, f_ref, m_ref, rrow_ref, x_ref):
    n = pl.program_id(0)
    flag = f_ref[n]

    @pl.when((flag & F_VALID) != 0)
    def _():
        rr = rrow_ref[...]
        row0 = (d_ref[n] * MOE_BLK).astype(F32)
        rel0 = rr[R_POS0:R_POS0 + 1] - row0
        rel1 = rr[R_POS1:R_POS1 + 1] - row0
        ri = lax.broadcasted_iota(jnp.int32, (MOE_BLK, MOE_TC), 0).astype(F32)
        onehot = jnp.where(rel0 == ri, 1.0, jnp.where(rel1 == ri, 1.0, 0.0)).astype(BF16)
        rows = jnp.dot(onehot, m_ref[...], preferred_element_type=F32).astype(x_ref.dtype)

        @pl.when((flag & F_FIRST) != 0)
        def _():
            x_ref[...] = rows

        @pl.when((flag & F_FIRST) == 0)
        def _():
            x_ref[...] = x_ref[...] + rows


def _moe_dispatch(m, rrow, plan, n_blk):
    chunk, blk, flags = plan["disp"]
    n_items = chunk.shape[0]
    grid_spec = pltpu.PrefetchScalarGridSpec(
        num_scalar_prefetch=3,
        grid=(n_items,),
        in_specs=[pl.BlockSpec((MOE_TC, D_MODEL), lambda n, c, d, f: (c[n], 0)),
                  pl.BlockSpec((R_ROWS, MOE_TC), lambda n, c, d, f: (0, c[n]))],
        out_specs=pl.BlockSpec((MOE_BLK, D_MODEL), lambda n, c, d, f: (d[n], 0)),
    )
    return pl.pallas_call(
        _moe_dispatch_kernel,
        out_shape=jax.ShapeDtypeStruct((n_blk * MOE_BLK, D_MODEL), BF16),
        grid_spec=grid_spec,
        compiler_params=_params("arbitrary"),
        name="moe_dispatch",
    )(chunk, blk, flags, m, rrow)


def _moe_experts_kernel(e_ref, src_ref, used_ref, x_ref, wg_ref, wu_ref, wd_ref, y_ref, acc_s):
    t = pl.program_id(0)
    j = pl.program_id(1)
    last = j == pl.num_programs(1) - 1
    used = used_ref[t] != 0

    @pl.when(used)
    def _():
        x = x_ref[...]
        gate = jnp.dot(x, wg_ref[...].astype(BF16), preferred_element_type=F32)
        up = jnp.dot(x, wu_ref[...].astype(BF16), preferred_element_type=F32)
        act = (_silu(gate) * up).astype(BF16)
        part = jnp.dot(act, wd_ref[...].astype(BF16), preferred_element_type=F32)

        @pl.when(j == 0)
        def _():
            acc_s[...] = part

        @pl.when(j != 0)
        def _():
            acc_s[...] += part

        @pl.when(last)
        def _():
            y_ref[...] = acc_s[...].astype(y_ref.dtype)

    @pl.when(jnp.logical_not(used) & last)
    def _():
        y_ref[...] = jnp.zeros_like(y_ref)


def _moe_experts(x, plan, wts):
    tile_e, tile_src, tile_used = plan["tiles"]
    n_xt = tile_e.shape[0]
    li = wts["moe_layer"]
    th = MOE_TH
    nj = MOE_HIDDEN // th

    def hidden(t, j, used):
        return jnp.where(used[t] != 0, j, nj - 1)

    grid_spec = pltpu.PrefetchScalarGridSpec(
        num_scalar_prefetch=3,
        grid=(n_xt, nj),
        in_specs=[pl.BlockSpec((MOE_XT, D_MODEL), lambda t, j, e, s, u: (s[t], 0)),
                  pl.BlockSpec((None, None, D_MODEL, th), lambda t, j, e, s, u: (li, e[t], 0, hidden(t, j, u))),
                  pl.BlockSpec((None, None, D_MODEL, th), lambda t, j, e, s, u: (li, e[t], 0, hidden(t, j, u))),
                  pl.BlockSpec((None, None, th, D_MODEL), lambda t, j, e, s, u: (li, e[t], hidden(t, j, u), 0))],
        out_specs=pl.BlockSpec((MOE_XT, D_MODEL), lambda t, j, e, s, u: (t, 0)),
        scratch_shapes=[pltpu.VMEM((MOE_XT, D_MODEL), F32)],
    )
    return pl.pallas_call(
        _moe_experts_kernel,
        out_shape=jax.ShapeDtypeStruct((n_xt * MOE_XT, D_MODEL), BF16),
        grid_spec=grid_spec,
        compiler_params=_params("arbitrary", "arbitrary"),
        name="moe_experts",
    )(tile_e, tile_src, tile_used, x, wts["moe_g"], wts["moe_u"], wts["moe_d"])


def _moe_combine_kernel(c_ref, d_ref, f_ref, rcol_ref, y_ref, h_ref, mod_ref, *rest, final):
    fg_ref = rest[0] if final else None
    out_ref, acc_s = rest[-2:]
    n = pl.program_id(0)
    flag = f_ref[n]

    @pl.when((flag & F_VALID) != 0)
    def _():
        rc = rcol_ref[...]
        row0 = (d_ref[n] * MOE_BLK).astype(F32)
        rel0 = rc[:, R_POS0:R_POS0 + 1] - row0
        rel1 = rc[:, R_POS1:R_POS1 + 1] - row0
        li = lax.broadcasted_iota(jnp.int32, (MOE_TC, MOE_BLK), 1).astype(F32)
        gates = jnp.where(rel0 == li, rc[:, R_W0:R_W0 + 1], jnp.where(rel1 == li, rc[:, R_W1:R_W1 + 1], 0.0))
        part = jnp.dot(gates.astype(BF16), y_ref[...], preferred_element_type=F32)

        @pl.when((flag & F_FIRST) != 0)
        def _():
            acc_s[...] = part

        @pl.when((flag & F_FIRST) == 0)
        def _():
            acc_s[...] += part

        @pl.when((flag & F_LAST) != 0)
        def _():
            new_h = h_ref[...] + mod_ref[5] * acc_s[...]
            out_ref[...] = _rms(new_h, fg_ref[...]) if final else new_h


def _moe_combine(h, mod, rcol, y, plan, tile0, final_g=None):
    chunk, blk, flags = plan["comb"]
    n_items = chunk.shape[0]
    final = final_g is not None
    extra_specs = [pl.BlockSpec(final_g.shape, lambda n, c, d, f: (0, 0))] if final else []
    extra_ins = [final_g] if final else []
    out_tile0 = 0 if final else tile0
    out_rows = N_ROWS - tile0 * MOE_TC if final else N_ROWS
    grid_spec = pltpu.PrefetchScalarGridSpec(
        num_scalar_prefetch=3,
        grid=(n_items,),
        in_specs=[pl.BlockSpec((MOE_TC, LANE), lambda n, c, d, f: (c[n], 0)),
                  pl.BlockSpec((MOE_BLK, D_MODEL), lambda n, c, d, f: (d[n], 0)),
                  pl.BlockSpec((MOE_TC, D_MODEL), lambda n, c, d, f: (c[n] + tile0, 0)),
                  pl.BlockSpec((N_MOD, None, 1, D_MODEL),
                               lambda n, c, d, f: (0, _mod_row(c[n] + tile0, MOE_TC), 0, 0))] + extra_specs,
        out_specs=pl.BlockSpec((MOE_TC, D_MODEL), lambda n, c, d, f: (c[n] + out_tile0, 0)),
        scratch_shapes=[pltpu.VMEM((MOE_TC, D_MODEL), F32)],
    )
    return pl.pallas_call(
        functools.partial(_moe_combine_kernel, final=final),
        out_shape=jax.ShapeDtypeStruct((out_rows, D_MODEL), F32),
        grid_spec=grid_spec,
        input_output_aliases={} if final else {5: 0},
        compiler_params=_params("arbitrary"),
        name="moe_combine_final" if final else "moe_combine",
    )(chunk, blk, flags, rcol, y, h, mod, *extra_ins)


def _moe(h, mod, ng, wts, first_row, final_g=None):
    tile0 = first_row // MOE_TC
    n_chunks = (N_ROWS - first_row) // MOE_TC
    n_blk = (2 * n_chunks * MOE_TC + N_EXPERTS * MOE_XT) // MOE_BLK
    m, rcol, rrow, cnt = _moe_route(h, mod, ng, wts["router"], tile0, n_chunks)
    plan = _moe_plan(cnt, n_chunks, n_blk)
    rcol, rrow = _sorted_rows(rcol, rrow, plan["base"])
    x = _moe_dispatch(m, rrow, plan, n_blk)
    y = _moe_experts(x, plan, wts)
    return _moe_combine(h, mod, rcol, y, plan, tile0, final_g)


assert DEPTH % 2 == 0, "the final norm is fused into the last (odd, MoE) layer's combine kernel"


def _row_vec(x):
    return x.reshape(1, -1).astype(F32)


def _even_weights(i, ev_w_in, a_ln_g, a_ln_b, a_ws, a_bs, b_gate_w, b_gate_b, b_norm_g, ev_w_out,
                  ff_w_gate, ff_w_up, ff_w_down):
    w_in = ev_w_in[i]
    qkvr = 2 * B_QK + 2 * B_V
    gate_w = jnp.zeros((2 * B_GATE_RANK, 2 * B_QK), F32)
    gate_w = gate_w.at[:B_GATE_RANK, :B_QK].set(b_gate_w[i, 0]).at[B_GATE_RANK:, B_QK:].set(b_gate_w[i, 1])
    nch = FF_HIDDEN // FF_TH
    return {
        "w_a": w_in[:, :2 * A_WIDTH].astype(BF16),
        "w_q": w_in[:, 2 * A_WIDTH:2 * A_WIDTH + qkvr].astype(BF16),
        "w_gf": w_in[:, 2 * A_WIDTH + qkvr:].astype(BF16),
        "gate_w": gate_w.astype(BF16),
        "gate_b": b_gate_b[i].reshape(1, 2 * B_QK).astype(F32),
        "ln_g": _row_vec(a_ln_g[i]),
        "ln_b": _row_vec(a_ln_b[i]),
        "ws": a_ws[i].astype(BF16),
        "bs": jnp.broadcast_to(a_bs[i][:, :, None], (A_HEADS, A_CHUNK, A_HEAD_DIM)).astype(F32),
        "b_norm_g": _row_vec(b_norm_g[i]),
        "w_out_a": ev_w_out[i, :A_WIDTH].astype(BF16),
        "w_out_b": ev_w_out[i, A_WIDTH:].astype(BF16),
        "ff_g": jnp.transpose(ff_w_gate[i].reshape(D_MODEL, nch, FF_TH), (1, 0, 2)).astype(BF16),
        "ff_u": jnp.transpose(ff_w_up[i].reshape(D_MODEL, nch, FF_TH), (1, 0, 2)).astype(BF16),
        "ff_d": ff_w_down[i].reshape(nch, FF_TH, D_MODEL).astype(BF16),
    }


def _rope_swap_index():
    quarter = C_ROPE // 4
    idx = np.arange(C_ROPE)
    return np.concatenate([idx[quarter:2 * quarter], idx[:quarter], idx[3 * quarter:], idx[2 * quarter:3 * quarter]])


def _odd_weights(i, od_w_in, c_q_norm_g, c_w_uq, c_kv_norm_g, c_w_ukv, od_w_out,
                 moe_router, moe_w_gate, moe_w_up, moe_w_down):
    swap = _rope_swap_index()
    w_in = od_w_in[i]
    low = C_Q_RANK + C_KV_RANK
    k_rope_w = w_in[:, low:]
    pad = jnp.zeros((D_MODEL, LANE - C_ROPE), F32)
    w_in_x = jnp.concatenate([w_in[:, :low], k_rope_w, pad, k_rope_w[:, swap], pad], axis=1)

    uq = c_w_uq[i].reshape(C_Q_RANK, C_HEADS, C_NOPE + C_ROPE)
    zq = jnp.zeros((C_Q_RANK, C_HEADS, HEAD_PAD - C_NOPE - C_ROPE), F32)
    w_q = jnp.concatenate([uq, zq], axis=-1)
    w_q_swap = jnp.concatenate([jnp.zeros((C_Q_RANK, C_HEADS, C_NOPE), F32), uq[:, :, C_NOPE:][:, :, swap], zq], axis=-1)
    ukv = c_w_ukv[i].reshape(C_KV_RANK, C_HEADS, C_NOPE + C_VDIM)
    w_k = jnp.concatenate([ukv[:, :, :C_NOPE], jnp.zeros((C_KV_RANK, C_HEADS, HEAD_PAD - C_NOPE), F32)], axis=-1)
    zv = jnp.zeros((C_KV_RANK, C_HEADS // 2, C_VDIM), F32)
    uv = ukv[:, :, C_NOPE:].reshape(C_KV_RANK, C_HEADS // 2, 2, C_VDIM)
    w_v = jnp.stack([jnp.concatenate([uv[:, :, 0], zv], axis=-1),
                     jnp.concatenate([zv, uv[:, :, 1]], axis=-1)], axis=2)
    v_one = np.zeros((C_HEADS // 2, 2, HEAD_PAD), np.float32)
    v_one[:, 0, C_VDIM] = 1.0
    v_one[:, 1, 0] = 1.0
    place = np.zeros((LANE, C_HEADS, HEAD_PAD), np.float32)
    for r in range(C_ROPE):
        place[r, :, C_NOPE + r] = 1.0
    router = jnp.concatenate([moe_router[i], jnp.zeros((D_MODEL, LANE - N_EXPERTS), F32)], axis=1)
    return {
        "w_in": w_in_x.astype(BF16),
        "q_norm_g": _row_vec(c_q_norm_g[i]),
        "kv_norm_g": _row_vec(c_kv_norm_g[i]),
        "w_q": w_q.reshape(C_Q_RANK, C_QK_PAD).astype(BF16),
        "w_q_swap": w_q_swap.reshape(C_Q_RANK, C_QK_PAD).astype(BF16),
        "w_k": w_k.reshape(C_KV_RANK, C_QK_PAD).astype(BF16),
        "w_v": w_v.reshape(C_KV_RANK, C_QK_PAD).astype(BF16),
        "place": jnp.asarray(place.reshape(LANE, C_QK_PAD), BF16),
        "v_one": jnp.asarray(v_one.reshape(1, C_QK_PAD), F32),
        "w_out": od_w_out[i].astype(BF16),
        "router": router,
        "moe_layer": i,
        "moe_g": moe_w_gate,
        "moe_u": moe_w_up,
        "moe_d": moe_w_down,
    }


def _rope_tables(tm):
    quarter = C_ROPE // 4
    half = C_ROPE // 2
    inv_freq = ROPE_BASE ** (-jnp.arange(0, half, 2, dtype=F32) / half)
    pos = jnp.arange(SEQ, dtype=jnp.int32)
    ang_r = (pos // GRID_W).astype(F32)[:, None] * inv_freq
    ang_c = (pos % GRID_W).astype(F32)[:, None] * inv_freq
    cos = jnp.concatenate([jnp.cos(ang_r)] * 2 + [jnp.cos(ang_c)] * 2, axis=1)
    sin = jnp.concatenate([-jnp.sin(ang_r), jnp.sin(ang_r), -jnp.sin(ang_c), jnp.sin(ang_c)], axis=1)
    cos = jnp.concatenate([cos, jnp.ones((tm, C_ROPE), F32)], axis=0)
    sin = jnp.concatenate([sin, jnp.zeros((tm, C_ROPE), F32)], axis=0)
    rows = SEQ + tm
    ones = jnp.ones((rows, C_NOPE), F32)
    z = lambda w: jnp.zeros((rows, w), F32)
    cq = jnp.concatenate([ones, cos, z(HEAD_PAD - C_NOPE - C_ROPE)], axis=1)
    sq = jnp.concatenate([z(C_NOPE), sin, z(HEAD_PAD - C_NOPE - C_ROPE)], axis=1)
    ck = jnp.concatenate([cos, z(LANE - C_ROPE)], axis=1)
    sk = jnp.concatenate([sin, z(LANE - C_ROPE)], axis=1)
    return cq, sq, ck, sk


def kernel(x, c, ctx, c_ctx, mod_w, mod_b, norm_mix_g, norm_ffn_g, final_g, ev_w_in, a_ln_g, a_ln_b, a_ws, a_bs, b_gate_w, b_gate_b, b_norm_g, ev_w_out, od_w_in, c_q_norm_g, c_w_uq, c_kv_norm_g, c_w_ukv, od_w_out, ff_w_gate, ff_w_up, ff_w_down, moe_router, moe_w_gate, moe_w_up, moe_w_down):
    h = jnp.concatenate([ctx.reshape(NC_ROWS, D_MODEL), x.reshape(NL_ROWS, D_MODEL)], axis=0)
    cond = jnp.concatenate([c, c_ctx[None, :], jnp.zeros((MOD_PAD - MOD_ROWS, D_MODEL), F32)], axis=0)
    mod = _modulation(cond, mod_w, mod_b)
    rope = _rope_tables(TM)

    for layer in range(DEPTH):
        i = layer // 2
        need_ctx = layer < DEPTH - 1
        ng_mix = _row_vec(norm_mix_g[layer])
        ng_ffn = _row_vec(norm_ffn_g[layer])
        if layer % 2 == 0:
            wts = _even_weights(i, ev_w_in, a_ln_g, a_ln_b, a_ws, a_bs, b_gate_w, b_gate_b, b_norm_g,
                                ev_w_out, ff_w_gate, ff_w_up, ff_w_down)
            ya, qkv, r, g = _even_in(h, mod[layer], ng_mix, wts)
            o_f, o_b = _gla(qkv, g)
            h = _even_out(h, mod[layer], o_f, o_b, r, ya, wts)
            h = _ffn(h, mod[layer], ng_ffn, wts)
        else:
            wts = _odd_weights(i, od_w_in, c_q_norm_g, c_w_uq, c_kv_norm_g, c_w_ukv, od_w_out,
                               moe_router, moe_w_gate, moe_w_up, moe_w_down)
            q, k, v = _odd_in(h, mod[layer], ng_mix, wts, rope)
            o = _attention(q, k, v, latent=True)
            if need_ctx:
                o = _attention(q, k, v, latent=False, prev=o)
            first = 0 if need_ctx else NC_ROWS
            h = _odd_out(h, mod[layer], o, wts["w_out"], first // TM, (N_ROWS - first) // TM)
            h = _moe(h, mod[layer], ng_ffn, wts, first, None if need_ctx else _row_vec(final_g))
    return h.reshape(BATCH, SEQ, D_MODEL)
```

```python
import functools

import numpy as np
import jax
import jax.numpy as jnp
from jax import lax
from jax.experimental import pallas as pl
from jax.experimental.pallas import tpu as pltpu

D_MODEL = 1024
BATCH = 8
SEQ = 2048
DEPTH = 4
GRID_W = 64
CTX_LEN = 256
EPS = 1e-6
N_MOD = 6

A_HEADS = 4
A_HEAD_DIM = 128
A_CHUNK = 128
A_WIDTH = A_HEADS * A_HEAD_DIM

B_HEADS = 4
B_DK = 64
B_DV = 128
B_QK = B_HEADS * B_DK
B_V = B_HEADS * B_DV
B_GATE_RANK = 16
B_GATE_TEMP = 16.0
B_CHUNK = 64

C_HEADS = 16
C_NOPE = 64
C_ROPE = 32
C_VDIM = 64
C_Q_RANK = 256
C_KV_RANK = 256
C_SCALE = (C_NOPE + C_ROPE) ** -0.5
LOG2_E = 1.4426950408889634
ROPE_BASE = 10000.0

FF_HIDDEN = 2816
N_EXPERTS = 8
MOE_HIDDEN = 3584

NC_ROWS = BATCH * CTX_LEN
NL_ROWS = BATCH * SEQ
N_ROWS = NC_ROWS + NL_ROWS
MOD_ROWS = BATCH + 1
MOD_PAD = 16

LANE = 128
HEAD_PAD = 128
C_QK_PAD = C_HEADS * HEAD_PAD

TM = 512
GLA_BLK = 256
GLA_NB = 1
ATT_TQ = 1024
ATT_HG = 4
FF_TH = 256
MOE_TC = 512
MOE_BLK = 256
MOE_XT = 1024
MOE_TH = 512
VMEM_LIMIT = 56 * 1024 * 1024

F32 = jnp.float32
BF16 = jnp.bfloat16


def _mm(a, b):
    return jnp.dot(a.astype(BF16), b.astype(BF16), preferred_element_type=F32)


def _mm_nt(a, b):
    return lax.dot_general(a.astype(BF16), b.astype(BF16), (((1,), (1,)), ((), ())),
                           preferred_element_type=F32)


def _mm_tn(a, b):
    return lax.dot_general(a.astype(BF16), b.astype(BF16), (((0,), (0,)), ((), ())),
                           preferred_element_type=F32)


def _sigmoid(x):
    return 1.0 / (1.0 + jnp.exp(-x))


def _silu(x):
    return x * _sigmoid(x)


def _gelu(x):
    return 0.5 * x * (1.0 + jnp.tanh(np.sqrt(2.0 / np.pi) * (x + 0.044715 * (x * x * x))))


def _rms(x, g):
    return x * lax.rsqrt(jnp.mean(x * x, axis=-1, keepdims=True) + EPS) * g


def _split_bf16(x):
    hi = x.astype(BF16)
    lo = (x - hi.astype(F32)).astype(BF16)
    return hi, lo


def _rows(tm, width, tile0=0, col=0):
    return pl.BlockSpec((tm, width), lambda i: (i + tile0, col))


def _const(shape):
    zeros = (0,) * len(shape)
    return pl.BlockSpec(shape, lambda *_: zeros, pipeline_mode=pl.Buffered(1))


def _mod_row(tile, tm):
    return jnp.where(tile < NC_ROWS // tm, BATCH, (tile * tm - NC_ROWS) // SEQ)


def _mod_spec(tm, tile0=0):
    return pl.BlockSpec((N_MOD, None, 1, D_MODEL), lambda i: (0, _mod_row(i + tile0, tm), 0, 0))


def _params(*sem):
    return pltpu.CompilerParams(dimension_semantics=sem, vmem_limit_bytes=VMEM_LIMIT)


def _mod_kernel(c_ref, w_ref, b_ref, o_ref):
    o_ref[...] = _mm(_silu(c_ref[...]), w_ref[...]) + b_ref[...]


def _modulation(cond, mod_w, mod_b):
    tn = 1536
    out = pl.pallas_call(
        _mod_kernel,
        out_shape=jax.ShapeDtypeStruct((DEPTH, MOD_PAD, N_MOD * D_MODEL), F32),
        grid=(DEPTH, N_MOD * D_MODEL // tn),
        in_specs=[
            pl.BlockSpec((MOD_PAD, D_MODEL), lambda l, j: (0, 0)),
            pl.BlockSpec((None, D_MODEL, tn), lambda l, j: (l, 0, j)),
            pl.BlockSpec((None, 1, tn), lambda l, j: (l, 0, j)),
        ],
        out_specs=pl.BlockSpec((None, MOD_PAD, tn), lambda l, j: (l, 0, j)),
        compiler_params=_params("arbitrary", "arbitrary"),
        name="modulation",
    )(cond, mod_w, mod_b.reshape(DEPTH, 1, N_MOD * D_MODEL))
    out = out[:, :MOD_ROWS].reshape(DEPTH, MOD_ROWS, N_MOD, 1, D_MODEL)
    return jnp.transpose(out, (0, 2, 1, 3, 4))


def _even_in_kernel(h_ref, mod_ref, ng_ref, wa_ref, wq_ref, wgf_ref, gw_ref, gb_ref,
                    lng_ref, lnb_ref, ws_ref, bs_ref, ya_ref, qkv_ref, r_ref, g_ref):
    tm = h_ref.shape[0]
    n = _rms(h_ref[...], ng_ref[...]) * (1.0 + mod_ref[1]) + mod_ref[0]
    nb = n.astype(BF16)

    ua_va = jnp.dot(nb, wa_ref[...], preferred_element_type=F32)
    u = _gelu(ua_va[:, :A_WIDTH])
    v = _gelu(ua_va[:, A_WIDTH:])
    mu = jnp.mean(v, axis=-1, keepdims=True)
    vc = v - mu
    var = jnp.mean(vc * vc, axis=-1, keepdims=True)
    v = (vc * lax.rsqrt(var + EPS) * lng_ref[...] + lnb_ref[...]).astype(BF16)
    for c in range(tm // A_CHUNK):
        rows = slice(c * A_CHUNK, (c + 1) * A_CHUNK)
        for h in range(A_HEADS):
            cols = slice(h * A_HEAD_DIM, (h + 1) * A_HEAD_DIM)
            mixed = jnp.dot(ws_ref[h], v[rows, cols], preferred_element_type=F32) + bs_ref[h]
            ya_ref[rows, cols] = (u[rows, cols] * mixed).astype(ya_ref.dtype)

    p = jnp.dot(nb, wq_ref[...], preferred_element_type=F32)
    qkv_ref[...] = p[:, :2 * B_QK + B_V].astype(qkv_ref.dtype)
    r_ref[...] = p[:, 2 * B_QK + B_V:].astype(r_ref.dtype)
    gf = jnp.dot(nb, wgf_ref[...], preferred_element_type=F32)
    logit = _mm(gf, gw_ref[...]) + gb_ref[...]
    log_sig = jnp.minimum(logit, 0.0) - jnp.log(1.0 + jnp.exp(-jnp.abs(logit)))
    g_ref[...] = log_sig / B_GATE_TEMP


def _even_in(h, mod, ng, wts):
    tm = TM
    n_tiles = N_ROWS // tm
    qkv_w = 2 * B_QK + B_V
    consts = [ng, wts["w_a"], wts["w_q"], wts["w_gf"], wts["gate_w"], wts["gate_b"],
              wts["ln_g"], wts["ln_b"], wts["ws"], wts["bs"]]
    return pl.pallas_call(
        _even_in_kernel,
        out_shape=(jax.ShapeDtypeStruct((N_ROWS, A_WIDTH), BF16),
                   jax.ShapeDtypeStruct((N_ROWS, qkv_w), BF16),
                   jax.ShapeDtypeStruct((N_ROWS, B_V), BF16),
                   jax.ShapeDtypeStruct((N_ROWS, 2 * B_QK), F32)),
        grid=(n_tiles,),
        in_specs=[_rows(tm, D_MODEL), _mod_spec(tm)] + [_const(a.shape) for a in consts],
        out_specs=(_rows(tm, A_WIDTH), _rows(tm, qkv_w), _rows(tm, B_V), _rows(tm, 2 * B_QK)),
        compiler_params=_params("arbitrary"),
        name="even_in",
    )(h, mod, *consts)


def _gla_direction(q, k, v, g, st_ref, reverse):
    n = GLA_BLK
    nch = n // B_CHUNK
    ri = lax.broadcasted_iota(jnp.int32, (n, n), 0)
    ci = lax.broadcasted_iota(jnp.int32, (n, n), 1)
    same = (ri // B_CHUNK) == (ci // B_CHUNK)
    tri = same & ((ci >= ri) if reverse else (ci <= ri))
    tri_b = jnp.where(tri, 1.0, 0.0).astype(BF16)
    same_b = jnp.where(same, 1.0, 0.0).astype(BF16)

    g_hi, g_lo = _split_bf16(g)
    dot = functools.partial(jnp.dot, preferred_element_type=F32)
    b = dot(tri_b, g_hi) + dot(tri_b, g_lo)
    b_tot = dot(same_b, g_hi) + dot(same_b, g_lo)
    q_dec = (q * (B_DK ** -0.5)) * jnp.exp(b)
    k_inv = k * jnp.exp(-b)
    k_dec = k * jnp.exp(b_tot - b)
    decay = jnp.exp(b_tot)

    lane = lax.broadcasted_iota(jnp.int32, (1, LANE), 1)
    even_lane = lane < B_DK

    outs = [None] * B_HEADS
    order = list(range(nch))[::-1] if reverse else list(range(nch))
    for p in range(B_HEADS // 2):
        lanes = slice(p * LANE, (p + 1) * LANE)
        qp = q_dec[:, lanes]
        q_heads = (jnp.where(even_lane, qp, 0.0).astype(BF16), jnp.where(even_lane, 0.0, qp).astype(BF16))
        k_inv_p = k_inv[:, lanes].astype(BF16)
        k_dec_p = k_dec[:, lanes].astype(BF16)
        scores_pair = _mm_nt(jnp.concatenate(q_heads, axis=0), k_inv_p)
        intra = []
        for hh in range(2):
            h = 2 * p + hh
            scores = jnp.where(tri, scores_pair[hh * n:(hh + 1) * n], 0.0)
            intra.append(_mm(scores, v[:, h * B_DV:(h + 1) * B_DV]))
        st = st_ref[p]
        inter = [[None] * nch for _ in range(2)]
        for c in order:
            rows = slice(c * B_CHUNK, (c + 1) * B_CHUNK)
            q_rows = jnp.concatenate([q_heads[0][rows], q_heads[1][rows]], axis=0)
            inter_pair = _mm_nt(q_rows, st.astype(BF16))
            inter[0][c] = inter_pair[:B_CHUNK]
            inter[1][c] = inter_pair[B_CHUNK:]
            upd = _mm_tn(v[rows, (2 * p) * B_DV:(2 * p + 2) * B_DV], k_dec_p[rows])
            st = st * decay[c * B_CHUNK:c * B_CHUNK + 1, lanes] + jnp.where(even_lane, upd[:B_DV], upd[B_DV:])
        st_ref[p] = st
        for hh in range(2):
            outs[2 * p + hh] = intra[hh] + jnp.concatenate(inter[hh], axis=0)
    return outs


def _gla_kernel(*refs):
    n_scan = 2 * GLA_NB
    in_refs = refs[:4 * n_scan]
    of_ref, ob_ref = refs[4 * n_scan:4 * n_scan + 2]
    st_refs = refs[4 * n_scan + 2:]

    @pl.when(pl.program_id(1) == 0)
    def _():
        for st in st_refs:
            st[...] = jnp.zeros_like(st)

    for s in range(n_scan):
        q_ref, k_ref, v_ref, g_ref = in_refs[4 * s:4 * s + 4]
        nb, reverse = s // 2, s % 2 == 1
        outs = _gla_direction(q_ref[...].astype(F32), k_ref[...].astype(F32), v_ref[...], g_ref[...],
                              st_refs[s], reverse)
        o_ref = ob_ref if reverse else of_ref
        for h in range(B_HEADS):
            o_ref[nb, :, h * B_DV:(h + 1) * B_DV] = outs[h]


def _gla(qkv, g):
    nb_ctx = CTX_LEN // GLA_BLK
    nb_lat = SEQ // GLA_BLK
    nblk = nb_ctx + nb_lat
    ctx_blocks = NC_ROWS // GLA_BLK

    def fwd_blk(j):
        return j

    def bwd_blk(j):
        return jnp.where(j < nb_ctx, nb_ctx - 1 - j, nb_ctx + nblk - 1 - j)

    def flat_row(b, blk):
        return jnp.where(blk < nb_ctx, b * nb_ctx + blk, ctx_blocks + b * nb_lat + (blk - nb_ctx))

    def in_spec(width, col, nb, blk):
        return pl.BlockSpec((GLA_BLK, width), lambda p, j: (flat_row(GLA_NB * p + nb, blk(j)), col))

    def out_spec(blk):
        return pl.BlockSpec((GLA_NB, None, GLA_BLK, B_V), lambda p, j: (p, blk(j), 0, 0))

    ins, in_specs = [], []
    for nb in range(GLA_NB):
        for d, blk in enumerate((fwd_blk, bwd_blk)):
            ins += [qkv, qkv, qkv, g]
            in_specs += [in_spec(B_QK, 0, nb, blk), in_spec(B_QK, 1, nb, blk), in_spec(B_V, 1, nb, blk),
                         in_spec(B_QK, d, nb, blk)]
    return pl.pallas_call(
        _gla_kernel,
        out_shape=(jax.ShapeDtypeStruct((BATCH, nblk, GLA_BLK, B_V), F32),) * 2,
        grid=(BATCH // GLA_NB, nblk),
        in_specs=in_specs,
        out_specs=(out_spec(fwd_blk), out_spec(bwd_blk)),
        scratch_shapes=[pltpu.VMEM((B_HEADS // 2, B_DV, LANE), F32)] * (2 * GLA_NB),
        compiler_params=_params("arbitrary", "arbitrary"),
        name="gla",
    )(*ins)


def _even_out_kernel(h_ref, mod_ref, of_ref, ob_ref, r_ref, ya_ref, bng_ref, woa_ref, wob_ref, out_ref):
    o = of_ref[...] + ob_ref[...]
    parts = [_rms(o[:, h * B_DV:(h + 1) * B_DV], bng_ref[...]) for h in range(B_HEADS)]
    yb = _silu(r_ref[...].astype(F32)) * jnp.concatenate(parts, axis=-1)
    y = jnp.dot(ya_ref[...], woa_ref[...], preferred_element_type=F32) + _mm(yb, wob_ref[...])
    out_ref[...] = h_ref[...] + mod_ref[2] * y


def _even_out(h, mod, o_f, o_b, r, ya, wts):
    tm = GLA_BLK
    consts = [wts["b_norm_g"], wts["w_out_a"], wts["w_out_b"]]
    nb_ctx = CTX_LEN // GLA_BLK
    nb_lat = SEQ // GLA_BLK
    ctx_blocks = NC_ROWS // GLA_BLK

    def gla_blk(i):
        t = i - ctx_blocks
        return (jnp.where(i < ctx_blocks, i // nb_ctx, t // nb_lat),
                jnp.where(i < ctx_blocks, i % nb_ctx, nb_ctx + t % nb_lat), 0, 0)

    o_spec = pl.BlockSpec((None, None, GLA_BLK, B_V), gla_blk)
    return pl.pallas_call(
        _even_out_kernel,
        out_shape=jax.ShapeDtypeStruct((N_ROWS, D_MODEL), F32),
        grid=(N_ROWS // tm,),
        in_specs=[_rows(tm, D_MODEL), _mod_spec(tm), o_spec, o_spec, _rows(tm, B_V),
                  _rows(tm, A_WIDTH)] + [_const(a.shape) for a in consts],
        out_specs=_rows(tm, D_MODEL),
        input_output_aliases={0: 0},
        compiler_params=_params("arbitrary"),
        name="even_out",
    )(h, mod, o_f, o_b, r, ya, *consts)


def _ffn_kernel(h_ref, mod_ref, ng_ref, wg_ref, wu_ref, wd_ref, out_ref):
    x = h_ref[...]
    m = (_rms(x, ng_ref[...]) * (1.0 + mod_ref[4]) + mod_ref[3]).astype(BF16)
    acc = jnp.zeros(x.shape, F32)
    for j in range(wg_ref.shape[0]):
        gate = jnp.dot(m, wg_ref[j], preferred_element_type=F32)
        up = jnp.dot(m, wu_ref[j], preferred_element_type=F32)
        act = (_silu(gate) * up).astype(BF16)
        acc = acc + jnp.dot(act, wd_ref[j], preferred_element_type=F32)
    out_ref[...] = x + mod_ref[5] * acc


def _ffn(h, mod, ng, wts):
    tm = TM
    consts = [ng, wts["ff_g"], wts["ff_u"], wts["ff_d"]]
    return pl.pallas_call(
        _ffn_kernel,
        out_shape=jax.ShapeDtypeStruct((N_ROWS, D_MODEL), F32),
        grid=(N_ROWS // tm,),
        in_specs=[_rows(tm, D_MODEL), _mod_spec(tm)] + [_const(a.shape) for a in consts],
        out_specs=_rows(tm, D_MODEL),
        input_output_aliases={0: 0},
        compiler_params=_params("arbitrary"),
        name="ffn",
    )(h, mod, *consts)


def _odd_in_kernel(h_ref, mod_ref, ng_ref, win_ref, qg_ref, kvg_ref, wq_ref, wqs_ref, wk_ref, wv_ref,
                   place_ref, vone_ref, cq_ref, sq_ref, ck_ref, sk_ref, q_ref, k_ref, v_ref):
    n = _rms(h_ref[...], ng_ref[...]) * (1.0 + mod_ref[1]) + mod_ref[0]
    p = jnp.dot(n.astype(BF16), win_ref[...], preferred_element_type=F32)
    cq = _rms(p[:, :C_Q_RANK], qg_ref[...]).astype(BF16)
    ckv = _rms(p[:, C_Q_RANK:C_Q_RANK + C_KV_RANK], kvg_ref[...]).astype(BF16)
    base = C_Q_RANK + C_KV_RANK
    kr = p[:, base:base + LANE] * ck_ref[...] + p[:, base + LANE:base + 2 * LANE] * sk_ref[...]

    q_plain = jnp.dot(cq, wq_ref[...], preferred_element_type=F32)
    q_swap = jnp.dot(cq, wqs_ref[...], preferred_element_type=F32)
    cq_t, sq_t = cq_ref[...], sq_ref[...]
    for h in range(C_HEADS):
        cols = slice(h * HEAD_PAD, (h + 1) * HEAD_PAD)
        q_ref[:, cols] = ((q_plain[:, cols] * cq_t + q_swap[:, cols] * sq_t) * (C_SCALE * LOG2_E)).astype(q_ref.dtype)
    k_cat = jnp.dot(ckv, wk_ref[...], preferred_element_type=F32) + _mm(kr, place_ref[...])
    k_ref[...] = k_cat.astype(k_ref.dtype)
    v_ref[...] = (jnp.dot(ckv, wv_ref[...], preferred_element_type=F32) + vone_ref[...]).astype(v_ref.dtype)


def _odd_in(h, mod, ng, wts, rope):
    tm = TM
    n_tiles = N_ROWS // tm
    consts = [ng, wts["w_in"], wts["q_norm_g"], wts["kv_norm_g"], wts["w_q"], wts["w_q_swap"],
              wts["w_k"], wts["w_v"], wts["place"], wts["v_one"]]
    ctx_tiles = NC_ROWS // tm
    lat_tiles = SEQ // tm

    def rope_spec():
        return pl.BlockSpec((tm, LANE), lambda i: (jnp.where(i < ctx_tiles, lat_tiles, (i - ctx_tiles) % lat_tiles), 0))

    return pl.pallas_call(
        _odd_in_kernel,
        out_shape=(jax.ShapeDtypeStruct((N_ROWS, C_QK_PAD), BF16),) * 3,
        grid=(n_tiles,),
        in_specs=[_rows(tm, D_MODEL), _mod_spec(tm)] + [_const(a.shape) for a in consts] + [rope_spec()] * 4,
        out_specs=(_rows(tm, C_QK_PAD),) * 3,
        compiler_params=_params("arbitrary"),
        name="odd_in",
    )(h, mod, *consts, *rope)


def _attn_kernel(*refs, n_pieces):
    q_ref = refs[0]
    k_refs = refs[1:1 + n_pieces]
    v_refs = refs[1 + n_pieces:1 + 2 * n_pieces]
    o_ref = refs[-1]
    for pair in range(ATT_HG // 2):
        acc = None
        for hh in range(2):
            cols = slice((2 * pair + hh) * HEAD_PAD, (2 * pair + hh + 1) * HEAD_PAD)
            q = q_ref[:, cols]
            s = [lax.dot_general(q, k[:, cols], (((1,), (1,)), ((), ())), preferred_element_type=F32)
                 for k in k_refs]
            m = functools.reduce(jnp.maximum, [jnp.max(x, axis=-1, keepdims=True) for x in s])
            e = [jnp.exp2(x - m) for x in s]
            o = functools.reduce(jnp.add, [jnp.dot(x.astype(BF16), v[:, cols], preferred_element_type=F32)
                                           for x, v in zip(e, v_refs)])
            ones_lane = C_VDIM if hh == 0 else 0
            l = o[:, ones_lane:ones_lane + 1]
            lane = lax.broadcasted_iota(jnp.int32, o.shape, 1)
            own = (lane < C_VDIM) if hh == 0 else (lane >= C_VDIM)
            o = jnp.where(own, o * (1.0 / l), 0.0)
            acc = o if acc is None else acc + o
        o_ref[:, pair * LANE:(pair + 1) * LANE] = acc.astype(o_ref.dtype)


def _attention(q, k, v, latent, prev=None):
    width = ATT_HG * HEAD_PAD
    n_hg = C_HEADS // ATT_HG
    if latent:
        tq = ATT_TQ
        nq = SEQ // tq
        q_spec = pl.BlockSpec((tq, width), lambda b, g, t: ((NC_ROWS + b * SEQ) // tq + t, g))
        kv_specs = [pl.BlockSpec((CTX_LEN, width), lambda b, g, t: (b, g)),
                    pl.BlockSpec((SEQ, width), lambda b, g, t: (NC_ROWS // SEQ + b, g))]
    else:
        tq = CTX_LEN
        nq = 1
        q_spec = pl.BlockSpec((tq, width), lambda b, g, t: (b, g))
        kv_specs = [pl.BlockSpec((CTX_LEN, width), lambda b, g, t: (b, g))]
    n_pieces = len(kv_specs)
    o_spec = pl.BlockSpec((tq, ATT_HG * C_VDIM), q_spec.index_map)
    ins = [q] + [k] * n_pieces + [v] * n_pieces
    in_specs = [q_spec] + kv_specs + kv_specs
    aliases = {}
    if prev is not None:
        aliases = {len(ins): 0}
        ins.append(prev)
        in_specs.append(pl.BlockSpec(memory_space=pl.ANY))
    return pl.pallas_call(
        functools.partial(_attn_kernel, n_pieces=n_pieces),
        out_shape=jax.ShapeDtypeStruct((N_ROWS, C_HEADS * C_VDIM), BF16),
        grid=(BATCH, n_hg, nq),
        in_specs=in_specs,
        out_specs=o_spec,
        input_output_aliases=aliases,
        compiler_params=_params("arbitrary", "arbitrary", "arbitrary"),
        name="attn_latent" if latent else "attn_context",
    )(*ins)


def _odd_out_kernel(h_ref, mod_ref, o_ref, w_ref, out_ref):
    y = jnp.dot(o_ref[...], w_ref[...], preferred_element_type=F32)
    out_ref[...] = h_ref[...] + mod_ref[2] * y


def _odd_out(h, mod, o, w_out, tile0, n_tiles):
    tm = TM
    return pl.pallas_call(
        _odd_out_kernel,
        out_shape=jax.ShapeDtypeStruct((N_ROWS, D_MODEL), F32),
        grid=(n_tiles,),
        in_specs=[_rows(tm, D_MODEL, tile0), _mod_spec(tm, tile0), _rows(tm, C_HEADS * C_VDIM, tile0),
                  _const(w_out.shape)],
        out_specs=_rows(tm, D_MODEL, tile0),
        input_output_aliases={0: 0},
        compiler_params=_params("arbitrary"),
        name="odd_out",
    )(h, mod, o, w_out)


def _route(m, router):
    m_hi, m_lo = _split_bf16(m)
    r_hi, r_lo = _split_bf16(router)
    dot = functools.partial(jnp.dot, preferred_element_type=F32)
    logits = dot(m_hi, r_hi) + (dot(m_hi, r_lo) + dot(m_lo, r_hi))
    lane = lax.broadcasted_iota(jnp.int32, logits.shape, 1)
    neg = -jnp.inf
    l1 = jnp.where(lane < N_EXPERTS, logits, neg)
    m1 = jnp.max(l1, axis=-1, keepdims=True)
    i1 = jnp.min(jnp.where(l1 == m1, lane, LANE), axis=-1, keepdims=True)
    l2 = jnp.where(lane == i1, neg, l1)
    m2 = jnp.max(l2, axis=-1, keepdims=True)
    i2 = jnp.min(jnp.where(l2 == m2, lane, LANE), axis=-1, keepdims=True)
    e2 = jnp.exp(m2 - m1)
    w1 = 1.0 / (1.0 + e2)
    w2 = e2 / (1.0 + e2)
    return i1, i2, w1, w2


R_E0, R_E1, R_RANK0, R_RANK1, R_W0, R_W1 = range(6)
R_POS0, R_POS1 = R_RANK0, R_RANK1
R_ROWS = 8
F_VALID, F_FIRST, F_LAST = 1, 2, 4


def _moe_route_kernel(h_ref, mod_ref, ng_ref, rt_ref, m_ref, rcol_ref, rrow_ref, cnt_ref, carry_s):
    @pl.when(pl.program_id(0) == 0)
    def _():
        carry_s[...] = jnp.zeros_like(carry_s)

    m = _rms(h_ref[...], ng_ref[...]) * (1.0 + mod_ref[4]) + mod_ref[3]
    m_ref[...] = m.astype(m_ref.dtype)
    i1, i2, w1, w2 = _route(m, rt_ref[...])
    tm = m.shape[0]
    lane = lax.broadcasted_iota(jnp.int32, (tm, LANE), 1)
    picked = jnp.where(lane == i1, 1.0, jnp.where(lane == i2, 1.0, 0.0))
    ri = lax.broadcasted_iota(jnp.int32, (tm, tm), 0)
    ci = lax.broadcasted_iota(jnp.int32, (tm, tm), 1)
    earlier = jnp.where(ci < ri, 1.0, 0.0).astype(BF16)
    before = carry_s[0:1, :]
    rank_all = jnp.dot(earlier, picked.astype(BF16), preferred_element_type=F32) + before
    rank0 = jnp.sum(jnp.where(lane == i1, rank_all, 0.0), axis=-1, keepdims=True)
    rank1 = jnp.sum(jnp.where(lane == i2, rank_all, 0.0), axis=-1, keepdims=True)
    after = before + jnp.sum(picked, axis=0, keepdims=True)
    carry_s[...] = jnp.broadcast_to(after, carry_s.shape)
    sub = lax.broadcasted_iota(jnp.int32, cnt_ref.shape, 0)
    cnt_ref[...] = jnp.where(sub == 0, before, jnp.where(sub == 1, after, 0.0))

    fields = {R_E0: i1.astype(F32), R_E1: i2.astype(F32), R_RANK0: rank0, R_RANK1: rank1, R_W0: w1, R_W1: w2}
    rcol = jnp.zeros((tm, LANE), F32)
    for k, val in fields.items():
        rcol = jnp.where(lane == k, val, rcol)
    rcol_ref[...] = rcol
    rrow_ref[...] = jnp.transpose(rcol)[:R_ROWS, :]


def _moe_route(h, mod, ng, router, tile0, n_chunks):
    tm = MOE_TC
    rows = n_chunks * tm
    return pl.pallas_call(
        _moe_route_kernel,
        out_shape=(jax.ShapeDtypeStruct((rows, D_MODEL), BF16),
                   jax.ShapeDtypeStruct((rows, LANE), F32),
                   jax.ShapeDtypeStruct((R_ROWS, rows), F32),
                   jax.ShapeDtypeStruct((n_chunks, R_ROWS, LANE), F32)),
        grid=(n_chunks,),
        in_specs=[_rows(tm, D_MODEL, tile0), _mod_spec(tm, tile0), _const(ng.shape), _const(router.shape)],
        out_specs=(_rows(tm, D_MODEL), _rows(tm, LANE),
                   pl.BlockSpec((R_ROWS, tm), lambda i: (0, i)),
                   pl.BlockSpec((None, R_ROWS, LANE), lambda i: (i, 0, 0))),
        scratch_shapes=[pltpu.VMEM((R_ROWS, LANE), F32)],
        compiler_params=_params("arbitrary"),
        name="moe_route",
    )(h, mod, ng, router)


def _moe_plan(cnt, n_chunks, n_blk):
    i32 = jnp.int32
    before = cnt[:, 0, :N_EXPERTS].astype(i32)
    after = cnt[:, 1, :N_EXPERTS].astype(i32)
    padded = (after[-1] + MOE_XT - 1) // MOE_XT * MOE_XT
    ends = jnp.cumsum(padded).astype(i32)
    base = ends - padded
    used_rows = ends[-1]
    start = base[None, :] + before
    stop = base[None, :] + after
    lo = jnp.arange(n_blk, dtype=i32) * MOE_BLK
    overlap = (start[:, :, None] < lo + MOE_BLK) & (stop[:, :, None] > lo)
    inc = jnp.any(overlap, axis=1)
    fill = (lo < used_rows) & ~jnp.any(inc, axis=0)
    inc_fill = inc.at[0].set(inc[0] | fill)
    n_items = n_chunks * N_EXPERTS + n_blk

    def items(mat):
        idx = jnp.nonzero(mat.reshape(-1), size=n_items, fill_value=-1)[0].astype(i32)
        valid = idx >= 0
        n_valid = jnp.sum(valid).astype(i32)
        idx = jnp.where(valid, idx, idx[jnp.maximum(n_valid - 1, 0)])
        outer = idx // mat.shape[1]
        inner = idx % mat.shape[1]
        prev = jnp.concatenate([jnp.full((1,), -1, i32), outer[:-1]])
        nxt = jnp.concatenate([outer[1:], jnp.full((1,), -1, i32)])
        pos = jnp.arange(n_items, dtype=i32)
        first = valid & (outer != prev)
        last = valid & ((outer != nxt) | (pos == n_valid - 1))
        flags = valid.astype(i32) * F_VALID + first.astype(i32) * F_FIRST + last.astype(i32) * F_LAST
        return outer, inner, flags

    disp_blk, disp_chunk, disp_flags = items(inc_fill.T)
    comb_chunk, comb_blk, comb_flags = items(inc)
    n_xt = n_blk * MOE_BLK // MOE_XT
    t0 = jnp.arange(n_xt, dtype=i32) * MOE_XT
    tile_used = (t0 < used_rows).astype(i32)
    tile_src = jnp.minimum(jnp.arange(n_xt, dtype=i32), used_rows // MOE_XT - 1)
    tile_e = jnp.minimum(jnp.sum(t0[:, None] >= ends[None, :], axis=1), N_EXPERTS - 1).astype(i32)[tile_src]
    return {"base": base, "disp": (disp_chunk, disp_blk, disp_flags), "comb": (comb_chunk, comb_blk, comb_flags),
            "tiles": (tile_e, tile_src, tile_used)}


def _sorted_rows(rcol, rrow, base):
    base_f = base.astype(F32)
    b0 = base_f[rcol[:, R_E0].astype(jnp.int32)]
    b1 = base_f[rcol[:, R_E1].astype(jnp.int32)]
    lane = jnp.arange(LANE)
    rcol = rcol + jnp.where(lane == R_POS0, b0[:, None], 0.0) + jnp.where(lane == R_POS1, b1[:, None], 0.0)
    sub = jnp.arange(R_ROWS)[:, None]
    rrow = rrow + jnp.where(sub == R_POS0, b0[None, :], 0.0) + jnp.where(sub == R_POS1, b1[None, :], 0.0)
    return rcol, rrow


def _moe_dispatch_kernel(c_ref, d_ref, f_ref, m_ref, rrow_ref, x_ref):
    n = pl.program_id(0)
    flag = f_ref[n]

    @pl.when((flag & F_VALID) != 0)
    def _():
        rr = rrow_ref[...]
        row0 = (d_ref[n] * MOE_BLK).astype(F32)
        rel0 = rr[R_POS0:R_POS0 + 1] - row0
        rel1 = rr[R_POS1:R_POS1 + 1] - row0
        ri = lax.broadcasted_iota(jnp.int32, (MOE_BLK, MOE_TC), 0).astype(F32)
        onehot = jnp.where(rel0 == ri, 1.0, jnp.where(rel1 == ri, 1.0, 0.0)).astype(BF16)
        rows = jnp.dot(onehot, m_ref[...], preferred_element_type=F32).astype(x_ref.dtype)

        @pl.when((flag & F_FIRST) != 0)
        def _():
            x_ref[...] = rows

        @pl.when((flag & F_FIRST) == 0)
        def _():
            x_ref[...] = x_ref[...] + rows


def _moe_dispatch(m, rrow, plan, n_blk):
    chunk, blk, flags = plan["disp"]
    n_items = chunk.shape[0]
    grid_spec = pltpu.PrefetchScalarGridSpec(
        num_scalar_prefetch=3,
        grid=(n_items,),
        in_specs=[pl.BlockSpec((MOE_TC, D_MODEL), lambda n, c, d, f: (c[n], 0)),
                  pl.BlockSpec((R_ROWS, MOE_TC), lambda n, c, d, f: (0, c[n]))],
        out_specs=pl.BlockSpec((MOE_BLK, D_MODEL), lambda n, c, d, f: (d[n], 0)),
    )
    return pl.pallas_call(
        _moe_dispatch_kernel,
        out_shape=jax.ShapeDtypeStruct((n_blk * MOE_BLK, D_MODEL), BF16),
        grid_spec=grid_spec,
        compiler_params=_params("arbitrary"),
        name="moe_dispatch",
    )(chunk, blk, flags, m, rrow)


def _moe_experts_kernel(e_ref, src_ref, used_ref, x_ref, wg_ref, wu_ref, wd_ref, y_ref, acc_s):
    t = pl.program_id(0)
    j = pl.program_id(1)
    last = j == pl.num_programs(1) - 1
    used = used_ref[t] != 0

    @pl.when(used)
    def _():
        x = x_ref[...]
        gate = jnp.dot(x, wg_ref[...].astype(BF16), preferred_element_type=F32)
        up = jnp.dot(x, wu_ref[...].astype(BF16), preferred_element_type=F32)
        act = (_silu(gate) * up).astype(BF16)
        part = jnp.dot(act, wd_ref[...].astype(BF16), preferred_element_type=F32)

        @pl.when(j == 0)
        def _():
            acc_s[...] = part

        @pl.when(j != 0)
        def _():
            acc_s[...] += part

        @pl.when(last)
        def _():
            y_ref[...] = acc_s[...].astype(y_ref.dtype)

    @pl.when(jnp.logical_not(used) & last)
    def _():
        y_ref[...] = jnp.zeros_like(y_ref)


def _moe_experts(x, plan, wts):
    tile_e, tile_src, tile_used = plan["tiles"]
    n_xt = tile_e.shape[0]
    li = wts["moe_layer"]
    th = MOE_TH
    nj = MOE_HIDDEN // th

    def hidden(t, j, used):
        return jnp.where(used[t] != 0, j, nj - 1)

    grid_spec = pltpu.PrefetchScalarGridSpec(
        num_scalar_prefetch=3,
        grid=(n_xt, nj),
        in_specs=[pl.BlockSpec((MOE_XT, D_MODEL), lambda t, j, e, s, u: (s[t], 0)),
                  pl.BlockSpec((None, None, D_MODEL, th), lambda t, j, e, s, u: (li, e[t], 0, hidden(t, j, u))),
                  pl.BlockSpec((None, None, D_MODEL, th), lambda t, j, e, s, u: (li, e[t], 0, hidden(t, j, u))),
                  pl.BlockSpec((None, None, th, D_MODEL), lambda t, j, e, s, u: (li, e[t], hidden(t, j, u), 0))],
        out_specs=pl.BlockSpec((MOE_XT, D_MODEL), lambda t, j, e, s, u: (t, 0)),
        scratch_shapes=[pltpu.VMEM((MOE_XT, D_MODEL), F32)],
    )
    return pl.pallas_call(
        _moe_experts_kernel,
        out_shape=jax.ShapeDtypeStruct((n_xt * MOE_XT, D_MODEL), BF16),
        grid_spec=grid_spec,
        compiler_params=_params("arbitrary", "arbitrary"),
        name="moe_experts",
    )(tile_e, tile_src, tile_used, x, wts["moe_g"], wts["moe_u"], wts["moe_d"])


def _moe_combine_kernel(c_ref, d_ref, f_ref, rcol_ref, y_ref, h_ref, mod_ref, *rest, final):
    fg_ref = rest[0] if final else None
    out_ref, acc_s = rest[-2:]
    n = pl.program_id(0)
    flag = f_ref[n]

    @pl.when((flag & F_VALID) != 0)
    def _():
        rc = rcol_ref[...]
        row0 = (d_ref[n] * MOE_BLK).astype(F32)
        rel0 = rc[:, R_POS0:R_POS0 + 1] - row0
        rel1 = rc[:, R_POS1:R_POS1 + 1] - row0
        li = lax.broadcasted_iota(jnp.int32, (MOE_TC, MOE_BLK), 1).astype(F32)
        gates = jnp.where(rel0 == li, rc[:, R_W0:R_W0 + 1], jnp.where(rel1 == li, rc[:, R_W1:R_W1 + 1], 0.0))
        part = jnp.dot(gates.astype(BF16), y_ref[...], preferred_element_type=F32)

        @pl.when((flag & F_FIRST) != 0)
        def _():
            acc_s[...] = part

        @pl.when((flag & F_FIRST) == 0)
        def _():
            acc_s[...] += part

        @pl.when((flag & F_LAST) != 0)
        def _():
            new_h = h_ref[...] + mod_ref[5] * acc_s[...]
            out_ref[...] = _rms(new_h, fg_ref[...]) if final else new_h


def _moe_combine(h, mod, rcol, y, plan, tile0, final_g=None):
    chunk, blk, flags = plan["comb"]
    n_items = chunk.shape[0]
    final = final_g is not None
    extra_specs = [pl.BlockSpec(final_g.shape, lambda n, c, d, f: (0, 0))] if final else []
    extra_ins = [final_g] if final else []
    out_tile0 = 0 if final else tile0
    out_rows = N_ROWS - tile0 * MOE_TC if final else N_ROWS
    grid_spec = pltpu.PrefetchScalarGridSpec(
        num_scalar_prefetch=3,
        grid=(n_items,),
        in_specs=[pl.BlockSpec((MOE_TC, LANE), lambda n, c, d, f: (c[n], 0)),
                  pl.BlockSpec((MOE_BLK, D_MODEL), lambda n, c, d, f: (d[n], 0)),
                  pl.BlockSpec((MOE_TC, D_MODEL), lambda n, c, d, f: (c[n] + tile0, 0)),
                  pl.BlockSpec((N_MOD, None, 1, D_MODEL),
                               lambda n, c, d, f: (0, _mod_row(c[n] + tile0, MOE_TC), 0, 0))] + extra_specs,
        out_specs=pl.BlockSpec((MOE_TC, D_MODEL), lambda n, c, d, f: (c[n] + out_tile0, 0)),
        scratch_shapes=[pltpu.VMEM((MOE_TC, D_MODEL), F32)],
    )
    return pl.pallas_call(
        functools.partial(_moe_combine_kernel, final=final),
        out_shape=jax.ShapeDtypeStruct((out_rows, D_MODEL), F32),
        grid_spec=grid_spec,
        input_output_aliases={} if final else {5: 0},
        compiler_params=_params("arbitrary"),
        name="moe_combine_final" if final else "moe_combine",
    )(chunk, blk, flags, rcol, y, h, mod, *extra_ins)


def _moe(h, mod, ng, wts, first_row, final_g=None):
    tile0 = first_row // MOE_TC
    n_chunks = (N_ROWS - first_row) // MOE_TC
    n_blk = (2 * n_chunks * MOE_TC + N_EXPERTS * MOE_XT) // MOE_BLK
    m, rcol, rrow, cnt = _moe_route(h, mod, ng, wts["router"], tile0, n_chunks)
    plan = _moe_plan(cnt, n_chunks, n_blk)
    rcol, rrow = _sorted_rows(rcol, rrow, plan["base"])
    x = _moe_dispatch(m, rrow, plan, n_blk)
    y = _moe_experts(x, plan, wts)
    return _moe_combine(h, mod, rcol, y, plan, tile0, final_g)


assert DEPTH % 2 == 0, "the final norm is fused into the last (odd, MoE) layer's combine kernel"


def _row_vec(x):
    return x.reshape(1, -1).astype(F32)


def _even_weights(i, ev_w_in, a_ln_g, a_ln_b, a_ws, a_bs, b_gate_w, b_gate_b, b_norm_g, ev_w_out,
                  ff_w_gate, ff_w_up, ff_w_down):
    w_in = ev_w_in[i]
    qkvr = 2 * B_QK + 2 * B_V
    gate_w = jnp.zeros((2 * B_GATE_RANK, 2 * B_QK), F32)
    gate_w = gate_w.at[:B_GATE_RANK, :B_QK].set(b_gate_w[i, 0]).at[B_GATE_RANK:, B_QK:].set(b_gate_w[i, 1])
    nch = FF_HIDDEN // FF_TH
    return {
        "w_a": w_in[:, :2 * A_WIDTH].astype(BF16),
        "w_q": w_in[:, 2 * A_WIDTH:2 * A_WIDTH + qkvr].astype(BF16),
        "w_gf": w_in[:, 2 * A_WIDTH + qkvr:].astype(BF16),
        "gate_w": gate_w.astype(BF16),
        "gate_b": b_gate_b[i].reshape(1, 2 * B_QK).astype(F32),
        "ln_g": _row_vec(a_ln_g[i]),
        "ln_b": _row_vec(a_ln_b[i]),
        "ws": a_ws[i].astype(BF16),
        "bs": jnp.broadcast_to(a_bs[i][:, :, None], (A_HEADS, A_CHUNK, A_HEAD_DIM)).astype(F32),
        "b_norm_g": _row_vec(b_norm_g[i]),
        "w_out_a": ev_w_out[i, :A_WIDTH].astype(BF16),
        "w_out_b": ev_w_out[i, A_WIDTH:].astype(BF16),
        "ff_g": jnp.transpose(ff_w_gate[i].reshape(D_MODEL, nch, FF_TH), (1, 0, 2)).astype(BF16),
        "ff_u": jnp.transpose(ff_w_up[i].reshape(D_MODEL, nch, FF_TH), (1, 0, 2)).astype(BF16),
        "ff_d": ff_w_down[i].reshape(nch, FF_TH, D_MODEL).astype(BF16),
    }


def _rope_swap_index():
    quarter = C_ROPE // 4
    idx = np.arange(C_ROPE)
    return np.concatenate([idx[quarter:2 * quarter], idx[:quarter], idx[3 * quarter:], idx[2 * quarter:3 * quarter]])


def _odd_weights(i, od_w_in, c_q_norm_g, c_w_uq, c_kv_norm_g, c_w_ukv, od_w_out,
                 moe_router, moe_w_gate, moe_w_up, moe_w_down):
    swap = _rope_swap_index()
    w_in = od_w_in[i]
    low = C_Q_RANK + C_KV_RANK
    k_rope_w = w_in[:, low:]
    pad = jnp.zeros((D_MODEL, LANE - C_ROPE), F32)
    w_in_x = jnp.concatenate([w_in[:, :low], k_rope_w, pad, k_rope_w[:, swap], pad], axis=1)

    uq = c_w_uq[i].reshape(C_Q_RANK, C_HEADS, C_NOPE + C_ROPE)
    zq = jnp.zeros((C_Q_RANK, C_HEADS, HEAD_PAD - C_NOPE - C_ROPE), F32)
    w_q = jnp.concatenate([uq, zq], axis=-1)
    w_q_swap = jnp.concatenate([jnp.zeros((C_Q_RANK, C_HEADS, C_NOPE), F32), uq[:, :, C_NOPE:][:, :, swap], zq], axis=-1)
    ukv = c_w_ukv[i].reshape(C_KV_RANK, C_HEADS, C_NOPE + C_VDIM)
    w_k = jnp.concatenate([ukv[:, :, :C_NOPE], jnp.zeros((C_KV_RANK, C_HEADS, HEAD_PAD - C_NOPE), F32)], axis=-1)
    zv = jnp.zeros((C_KV_RANK, C_HEADS // 2, C_VDIM), F32)
    uv = ukv[:, :, C_NOPE:].reshape(C_KV_RANK, C_HEADS // 2, 2, C_VDIM)
    w_v = jnp.stack([jnp.concatenate([uv[:, :, 0], zv], axis=-1),
                     jnp.concatenate([zv, uv[:, :, 1]], axis=-1)], axis=2)
    v_one = np.zeros((C_HEADS // 2, 2, HEAD_PAD), np.float32)
    v_one[:, 0, C_VDIM] = 1.0
    v_one[:, 1, 0] = 1.0
    place = np.zeros((LANE, C_HEADS, HEAD_PAD), np.float32)
    for r in range(C_ROPE):
        place[r, :, C_NOPE + r] = 1.0
    router = jnp.concatenate([moe_router[i], jnp.zeros((D_MODEL, LANE - N_EXPERTS), F32)], axis=1)
    return {
        "w_in": w_in_x.astype(BF16),
        "q_norm_g": _row_vec(c_q_norm_g[i]),
        "kv_norm_g": _row_vec(c_kv_norm_g[i]),
        "w_q": w_q.reshape(C_Q_RANK, C_QK_PAD).astype(BF16),
        "w_q_swap": w_q_swap.reshape(C_Q_RANK, C_QK_PAD).astype(BF16),
        "w_k": w_k.reshape(C_KV_RANK, C_QK_PAD).astype(BF16),
        "w_v": w_v.reshape(C_KV_RANK, C_QK_PAD).astype(BF16),
        "place": jnp.asarray(place.reshape(LANE, C_QK_PAD), BF16),
        "v_one": jnp.asarray(v_one.reshape(1, C_QK_PAD), F32),
        "w_out": od_w_out[i].astype(BF16),
        "router": router,
        "moe_layer": i,
        "moe_g": moe_w_gate,
        "moe_u": moe_w_up,
        "moe_d": moe_w_down,
    }


def _rope_tables(tm):
    quarter = C_ROPE // 4
    half = C_ROPE // 2
    inv_freq = ROPE_BASE ** (-jnp.arange(0, half, 2, dtype=F32) / half)
    pos = jnp.arange(SEQ, dtype=jnp.int32)
    ang_r = (pos // GRID_W).astype(F32)[:, None] * inv_freq
    ang_c = (pos % GRID_W).astype(F32)[:, None] * inv_freq
    cos = jnp.concatenate([jnp.cos(ang_r)] * 2 + [jnp.cos(ang_c)] * 2, axis=1)
    sin = jnp.concatenate([-jnp.sin(ang_r), jnp.sin(ang_r), -jnp.sin(ang_c), jnp.sin(ang_c)], axis=1)
    cos = jnp.concatenate([cos, jnp.ones((tm, C_ROPE), F32)], axis=0)
    sin = jnp.concatenate([sin, jnp.zeros((tm, C_ROPE), F32)], axis=0)
    rows = SEQ + tm
    ones = jnp.ones((rows, C_NOPE), F32)
    z = lambda w: jnp.zeros((rows, w), F32)
    cq = jnp.concatenate([ones, cos, z(HEAD_PAD - C_NOPE - C_ROPE)], axis=1)
    sq = jnp.concatenate([z(C_NOPE), sin, z(HEAD_PAD - C_NOPE - C_ROPE)], axis=1)
    ck = jnp.concatenate([cos, z(LANE - C_ROPE)], axis=1)
    sk = jnp.concatenate([sin, z(LANE - C_ROPE)], axis=1)
    return cq, sq, ck, sk


def kernel(x, c, ctx, c_ctx, mod_w, mod_b, norm_mix_g, norm_ffn_g, final_g, ev_w_in, a_ln_g, a_ln_b, a_ws, a_bs, b_gate_w, b_gate_b, b_norm_g, ev_w_out, od_w_in, c_q_norm_g, c_w_uq, c_kv_norm_g, c_w_ukv, od_w_out, ff_w_gate, ff_w_up, ff_w_down, moe_router, moe_w_gate, moe_w_up, moe_w_down):
    h = jnp.concatenate([ctx.reshape(NC_ROWS, D_MODEL), x.reshape(NL_ROWS, D_MODEL)], axis=0)
    cond = jnp.concatenate([c, c_ctx[None, :], jnp.zeros((MOD_PAD - MOD_ROWS, D_MODEL), F32)], axis=0)
    mod = _modulation(cond, mod_w, mod_b)
    rope = _rope_tables(TM)

    for layer in range(DEPTH):
        i = layer // 2
        need_ctx = layer < DEPTH - 1
        ng_mix = _row_vec(norm_mix_g[layer])
        ng_ffn = _row_vec(norm_ffn_g[layer])
        if layer % 2 == 0:
            wts = _even_weights(i, ev_w_in, a_ln_g, a_ln_b, a_ws, a_bs, b_gate_w, b_gate_b, b_norm_g,
                                ev_w_out, ff_w_gate, ff_w_up, ff_w_down)
            ya, qkv, r, g = _even_in(h, mod[layer], ng_mix, wts)
            o_f, o_b = _gla(qkv, g)
            h = _even_out(h, mod[layer], o_f, o_b, r, ya, wts)
            h = _ffn(h, mod[layer], ng_ffn, wts)
        else:
            wts = _odd_weights(i, od_w_in, c_q_norm_g, c_w_uq, c_kv_norm_g, c_w_ukv, od_w_out,
                               moe_router, moe_w_gate, moe_w_up, moe_w_down)
            q, k, v = _odd_in(h, mod[layer], ng_mix, wts, rope)
            o = _attention(q, k, v, latent=True)
            if need_ctx:
                o = _attention(q, k, v, latent=False, prev=o)
            first = 0 if need_ctx else NC_ROWS
            h = _odd_out(h, mod[layer], o, wts["w_out"], first // TM, (N_ROWS - first) // TM)
            h = _moe(h, mod[layer], ng_ffn, wts, first, None if need_ctx else _row_vec(final_g))
    return h.reshape(BATCH, SEQ, D_MODEL)
```

```python
import functools

import numpy as np
import jax
import jax.numpy as jnp
from jax import lax
from jax.experimental import pallas as pl
from jax.experimental.pallas import tpu as pltpu

D_MODEL = 1024
BATCH = 8
SEQ = 2048
DEPTH = 4
GRID_W = 64
CTX_LEN = 256
EPS = 1e-6
N_MOD = 6

A_HEADS = 4
A_HEAD_DIM = 128
A_CHUNK = 128
A_WIDTH = A_HEADS * A_HEAD_DIM

B_HEADS = 4
B_DK = 64
B_DV = 128
B_QK = B_HEADS * B_DK
B_V = B_HEADS * B_DV
B_GATE_RANK = 16
B_GATE_TEMP = 16.0
B_CHUNK = 64

C_HEADS = 16
C_NOPE = 64
C_ROPE = 32
C_VDIM = 64
C_Q_RANK = 256
C_KV_RANK = 256
C_SCALE = (C_NOPE + C_ROPE) ** -0.5
LOG2_E = 1.4426950408889634
ROPE_BASE = 10000.0

FF_HIDDEN = 2816
N_EXPERTS = 8
MOE_HIDDEN = 3584

NC_ROWS = BATCH * CTX_LEN
NL_ROWS = BATCH * SEQ
N_ROWS = NC_ROWS + NL_ROWS
MOD_ROWS = BATCH + 1
MOD_PAD = 16

LANE = 128
HEAD_PAD = 128
C_QK_PAD = C_HEADS * HEAD_PAD

TM = 512
GLA_BLK = 256
GLA_NB = 1
ATT_TQ = 512
ATT_QS = 256
ATT_KC = 256
ATT_HG = 4
FF_TH = 256
MOE_TC = 512
MOE_BLK = 256
MOE_XT = 1024
MOE_TH = 512
VMEM_LIMIT = 56 * 1024 * 1024

F32 = jnp.float32
BF16 = jnp.bfloat16


def _mm(a, b):
    return jnp.dot(a.astype(BF16), b.astype(BF16), preferred_element_type=F32)


def _mm_nt(a, b):
    return lax.dot_general(a.astype(BF16), b.astype(BF16), (((1,), (1,)), ((), ())),
                           preferred_element_type=F32)


def _mm_tn(a, b):
    return lax.dot_general(a.astype(BF16), b.astype(BF16), (((0,), (0,)), ((), ())),
                           preferred_element_type=F32)


def _sigmoid(x):
    return 1.0 / (1.0 + jnp.exp(-x))


def _silu(x):
    return x * _sigmoid(x)


def _gelu(x):
    return 0.5 * x * (1.0 + jnp.tanh(np.sqrt(2.0 / np.pi) * (x + 0.044715 * (x * x * x))))


def _rms(x, g):
    return x * lax.rsqrt(jnp.mean(x * x, axis=-1, keepdims=True) + EPS) * g


def _split_bf16(x):
    hi = x.astype(BF16)
    lo = (x - hi.astype(F32)).astype(BF16)
    return hi, lo


def _rows(tm, width, tile0=0, col=0):
    return pl.BlockSpec((tm, width), lambda i: (i + tile0, col))


def _const(shape):
    zeros = (0,) * len(shape)
    return pl.BlockSpec(shape, lambda *_: zeros, pipeline_mode=pl.Buffered(1))


def _mod_row(tile, tm):
    return jnp.where(tile < NC_ROWS // tm, BATCH, (tile * tm - NC_ROWS) // SEQ)


def _mod_spec(tm, tile0=0):
    return pl.BlockSpec((N_MOD, None, 1, D_MODEL), lambda i: (0, _mod_row(i + tile0, tm), 0, 0))


def _params(*sem):
    return pltpu.CompilerParams(dimension_semantics=sem, vmem_limit_bytes=VMEM_LIMIT)


def _mod_kernel(c_ref, w_ref, b_ref, o_ref):
    o_ref[...] = _mm(_silu(c_ref[...]), w_ref[...]) + b_ref[...]


def _modulation(cond, mod_w, mod_b):
    tn = 1536
    out = pl.pallas_call(
        _mod_kernel,
        out_shape=jax.ShapeDtypeStruct((DEPTH, MOD_PAD, N_MOD * D_MODEL), F32),
        grid=(DEPTH, N_MOD * D_MODEL // tn),
        in_specs=[
            pl.BlockSpec((MOD_PAD, D_MODEL), lambda l, j: (0, 0)),
            pl.BlockSpec((None, D_MODEL, tn), lambda l, j: (l, 0, j)),
            pl.BlockSpec((None, 1, tn), lambda l, j: (l, 0, j)),
        ],
        out_specs=pl.BlockSpec((None, MOD_PAD, tn), lambda l, j: (l, 0, j)),
        compiler_params=_params("arbitrary", "arbitrary"),
        name="modulation",
    )(cond, mod_w, mod_b.reshape(DEPTH, 1, N_MOD * D_MODEL))
    out = out[:, :MOD_ROWS].reshape(DEPTH, MOD_ROWS, N_MOD, 1, D_MODEL)
    return jnp.transpose(out, (0, 2, 1, 3, 4))


def _even_in_kernel(h_ref, mod_ref, ng_ref, wa_ref, wq_ref, wgf_ref, gw_ref, gb_ref,
                    lng_ref, lnb_ref, ws_ref, bs_ref, ya_ref, qkv_ref, r_ref, g_ref):
    tm = h_ref.shape[0]
    n = _rms(h_ref[...], ng_ref[...]) * (1.0 + mod_ref[1]) + mod_ref[0]
    nb = n.astype(BF16)

    ua_va = jnp.dot(nb, wa_ref[...], preferred_element_type=F32)
    u = _gelu(ua_va[:, :A_WIDTH])
    v = _gelu(ua_va[:, A_WIDTH:])
    mu = jnp.mean(v, axis=-1, keepdims=True)
    vc = v - mu
    var = jnp.mean(vc * vc, axis=-1, keepdims=True)
    v = (vc * lax.rsqrt(var + EPS) * lng_ref[...] + lnb_ref[...]).astype(BF16)
    for c in range(tm // A_CHUNK):
        rows = slice(c * A_CHUNK, (c + 1) * A_CHUNK)
        for h in range(A_HEADS):
            cols = slice(h * A_HEAD_DIM, (h + 1) * A_HEAD_DIM)
            mixed = jnp.dot(ws_ref[h], v[rows, cols], preferred_element_type=F32) + bs_ref[h]
            ya_ref[rows, cols] = (u[rows, cols] * mixed).astype(ya_ref.dtype)

    p = jnp.dot(nb, wq_ref[...], preferred_element_type=F32)
    qkv_ref[...] = p[:, :2 * B_QK + B_V].astype(qkv_ref.dtype)
    r_ref[...] = p[:, 2 * B_QK + B_V:].astype(r_ref.dtype)
    gf = jnp.dot(nb, wgf_ref[...], preferred_element_type=F32)
    logit = _mm(gf, gw_ref[...]) + gb_ref[...]
    log_sig = jnp.minimum(logit, 0.0) - jnp.log(1.0 + jnp.exp(-jnp.abs(logit)))
    g_ref[...] = log_sig / B_GATE_TEMP


def _even_in(h, mod, ng, wts):
    tm = TM
    n_tiles = N_ROWS // tm
    qkv_w = 2 * B_QK + B_V
    consts = [ng, wts["w_a"], wts["w_q"], wts["w_gf"], wts["gate_w"], wts["gate_b"],
              wts["ln_g"], wts["ln_b"], wts["ws"], wts["bs"]]
    return pl.pallas_call(
        _even_in_kernel,
        out_shape=(jax.ShapeDtypeStruct((N_ROWS, A_WIDTH), BF16),
                   jax.ShapeDtypeStruct((N_ROWS, qkv_w), BF16),
                   jax.ShapeDtypeStruct((N_ROWS, B_V), BF16),
                   jax.ShapeDtypeStruct((N_ROWS, 2 * B_QK), F32)),
        grid=(n_tiles,),
        in_specs=[_rows(tm, D_MODEL), _mod_spec(tm)] + [_const(a.shape) for a in consts],
        out_specs=(_rows(tm, A_WIDTH), _rows(tm, qkv_w), _rows(tm, B_V), _rows(tm, 2 * B_QK)),
        compiler_params=_params("arbitrary"),
        name="even_in",
    )(h, mod, *consts)


def _gla_direction(q, k, v, g, st_ref, reverse):
    n = GLA_BLK
    nch = n // B_CHUNK
    ri = lax.broadcasted_iota(jnp.int32, (n, n), 0)
    ci = lax.broadcasted_iota(jnp.int32, (n, n), 1)
    same = (ri // B_CHUNK) == (ci // B_CHUNK)
    tri = same & ((ci >= ri) if reverse else (ci <= ri))
    tri_b = jnp.where(tri, 1.0, 0.0).astype(BF16)
    same_b = jnp.where(same, 1.0, 0.0).astype(BF16)

    g_hi, g_lo = _split_bf16(g)
    dot = functools.partial(jnp.dot, preferred_element_type=F32)
    b = dot(tri_b, g_hi) + dot(tri_b, g_lo)
    b_tot = dot(same_b, g_hi) + dot(same_b, g_lo)
    q_dec = (q * (B_DK ** -0.5)) * jnp.exp(b)
    k_inv = k * jnp.exp(-b)
    k_dec = k * jnp.exp(b_tot - b)
    decay = jnp.exp(b_tot)

    lane = lax.broadcasted_iota(jnp.int32, (1, LANE), 1)
    even_lane = lane < B_DK

    outs = [None] * B_HEADS
    order = list(range(nch))[::-1] if reverse else list(range(nch))
    for p in range(B_HEADS // 2):
        lanes = slice(p * LANE, (p + 1) * LANE)
        qp = q_dec[:, lanes]
        q_heads = (jnp.where(even_lane, qp, 0.0).astype(BF16), jnp.where(even_lane, 0.0, qp).astype(BF16))
        k_inv_p = k_inv[:, lanes].astype(BF16)
        k_dec_p = k_dec[:, lanes].astype(BF16)
        scores_pair = _mm_nt(jnp.concatenate(q_heads, axis=0), k_inv_p)
        intra = []
        for hh in range(2):
            h = 2 * p + hh
            scores = jnp.where(tri, scores_pair[hh * n:(hh + 1) * n], 0.0)
            intra.append(_mm(scores, v[:, h * B_DV:(h + 1) * B_DV]))
        st = st_ref[p]
        inter = [[None] * nch for _ in range(2)]
        for c in order:
            rows = slice(c * B_CHUNK, (c + 1) * B_CHUNK)
            q_rows = jnp.concatenate([q_heads[0][rows], q_heads[1][rows]], axis=0)
            inter_pair = _mm_nt(q_rows, st.astype(BF16))
            inter[0][c] = inter_pair[:B_CHUNK]
            inter[1][c] = inter_pair[B_CHUNK:]
            upd = _mm_tn(v[rows, (2 * p) * B_DV:(2 * p + 2) * B_DV], k_dec_p[rows])
            st = st * decay[c * B_CHUNK:c * B_CHUNK + 1, lanes] + jnp.where(even_lane, upd[:B_DV], upd[B_DV:])
        st_ref[p] = st
        for hh in range(2):
            outs[2 * p + hh] = intra[hh] + jnp.concatenate(inter[hh], axis=0)
    return outs


def _gla_kernel(*refs):
    n_scan = 2 * GLA_NB
    in_refs = refs[:4 * n_scan]
    of_ref, ob_ref = refs[4 * n_scan:4 * n_scan + 2]
    st_refs = refs[4 * n_scan + 2:]

    @pl.when(pl.program_id(1) == 0)
    def _():
        for st in st_refs:
            st[...] = jnp.zeros_like(st)

    for s in range(n_scan):
        q_ref, k_ref, v_ref, g_ref = in_refs[4 * s:4 * s + 4]
        nb, reverse = s // 2, s % 2 == 1
        outs = _gla_direction(q_ref[...].astype(F32), k_ref[...].astype(F32), v_ref[...], g_ref[...],
                              st_refs[s], reverse)
        o_ref = ob_ref if reverse else of_ref
        for h in range(B_HEADS):
            o_ref[nb, :, h * B_DV:(h + 1) * B_DV] = outs[h]


def _gla(qkv, g):
    nb_ctx = CTX_LEN // GLA_BLK
    nb_lat = SEQ // GLA_BLK
    nblk = nb_ctx + nb_lat
    ctx_blocks = NC_ROWS // GLA_BLK

    def fwd_blk(j):
        return j

    def bwd_blk(j):
        return jnp.where(j < nb_ctx, nb_ctx - 1 - j, nb_ctx + nblk - 1 - j)

    def flat_row(b, blk):
        return jnp.where(blk < nb_ctx, b * nb_ctx + blk, ctx_blocks + b * nb_lat + (blk - nb_ctx))

    def in_spec(width, col, nb, blk):
        return pl.BlockSpec((GLA_BLK, width), lambda p, j: (flat_row(GLA_NB * p + nb, blk(j)), col))

    def out_spec(blk):
        return pl.BlockSpec((GLA_NB, None, GLA_BLK, B_V), lambda p, j: (p, blk(j), 0, 0))

    ins, in_specs = [], []
    for nb in range(GLA_NB):
        for d, blk in enumerate((fwd_blk, bwd_blk)):
            ins += [qkv, qkv, qkv, g]
            in_specs += [in_spec(B_QK, 0, nb, blk), in_spec(B_QK, 1, nb, blk), in_spec(B_V, 1, nb, blk),
                         in_spec(B_QK, d, nb, blk)]
    return pl.pallas_call(
        _gla_kernel,
        out_shape=(jax.ShapeDtypeStruct((BATCH, nblk, GLA_BLK, B_V), F32),) * 2,
        grid=(BATCH // GLA_NB, nblk),
        in_specs=in_specs,
        out_specs=(out_spec(fwd_blk), out_spec(bwd_blk)),
        scratch_shapes=[pltpu.VMEM((B_HEADS // 2, B_DV, LANE), F32)] * (2 * GLA_NB),
        compiler_params=_params("arbitrary", "arbitrary"),
        name="gla",
    )(*ins)


def _even_out_kernel(h_ref, mod_ref, of_ref, ob_ref, r_ref, ya_ref, bng_ref, woa_ref, wob_ref, out_ref):
    o = of_ref[...] + ob_ref[...]
    parts = [_rms(o[:, h * B_DV:(h + 1) * B_DV], bng_ref[...]) for h in range(B_HEADS)]
    yb = _silu(r_ref[...].astype(F32)) * jnp.concatenate(parts, axis=-1)
    y = jnp.dot(ya_ref[...], woa_ref[...], preferred_element_type=F32) + _mm(yb, wob_ref[...])
    out_ref[...] = h_ref[...] + mod_ref[2] * y


def _even_out(h, mod, o_f, o_b, r, ya, wts):
    tm = GLA_BLK
    consts = [wts["b_norm_g"], wts["w_out_a"], wts["w_out_b"]]
    nb_ctx = CTX_LEN // GLA_BLK
    nb_lat = SEQ // GLA_BLK
    ctx_blocks = NC_ROWS // GLA_BLK

    def gla_blk(i):
        t = i - ctx_blocks
        return (jnp.where(i < ctx_blocks, i // nb_ctx, t // nb_lat),
                jnp.where(i < ctx_blocks, i % nb_ctx, nb_ctx + t % nb_lat), 0, 0)

    o_spec = pl.BlockSpec((None, None, GLA_BLK, B_V), gla_blk)
    return pl.pallas_call(
        _even_out_kernel,
        out_shape=jax.ShapeDtypeStruct((N_ROWS, D_MODEL), F32),
        grid=(N_ROWS // tm,),
        in_specs=[_rows(tm, D_MODEL), _mod_spec(tm), o_spec, o_spec, _rows(tm, B_V),
                  _rows(tm, A_WIDTH)] + [_const(a.shape) for a in consts],
        out_specs=_rows(tm, D_MODEL),
        input_output_aliases={0: 0},
        compiler_params=_params("arbitrary"),
        name="even_out",
    )(h, mod, o_f, o_b, r, ya, *consts)


def _ffn_kernel(h_ref, mod_ref, ng_ref, wg_ref, wu_ref, wd_ref, out_ref):
    x = h_ref[...]
    m = (_rms(x, ng_ref[...]) * (1.0 + mod_ref[4]) + mod_ref[3]).astype(BF16)
    acc = jnp.zeros(x.shape, F32)
    for j in range(wg_ref.shape[0]):
        gate = jnp.dot(m, wg_ref[j], preferred_element_type=F32)
        up = jnp.dot(m, wu_ref[j], preferred_element_type=F32)
        act = (_silu(gate) * up).astype(BF16)
        acc = acc + jnp.dot(act, wd_ref[j], preferred_element_type=F32)
    out_ref[...] = x + mod_ref[5] * acc


def _ffn(h, mod, ng, wts):
    tm = TM
    consts = [ng, wts["ff_g"], wts["ff_u"], wts["ff_d"]]
    return pl.pallas_call(
        _ffn_kernel,
        out_shape=jax.ShapeDtypeStruct((N_ROWS, D_MODEL), F32),
        grid=(N_ROWS // tm,),
        in_specs=[_rows(tm, D_MODEL), _mod_spec(tm)] + [_const(a.shape) for a in consts],
        out_specs=_rows(tm, D_MODEL),
        input_output_aliases={0: 0},
        compiler_params=_params("arbitrary"),
        name="ffn",
    )(h, mod, *consts)


def _odd_in_kernel(h_ref, mod_ref, ng_ref, win_ref, qg_ref, kvg_ref, wq_ref, wqs_ref, wk_ref, wv_ref,
                   place_ref, vone_ref, cq_ref, sq_ref, ck_ref, sk_ref, q_ref, k_ref, v_ref):
    n = _rms(h_ref[...], ng_ref[...]) * (1.0 + mod_ref[1]) + mod_ref[0]
    p = jnp.dot(n.astype(BF16), win_ref[...], preferred_element_type=F32)
    cq = _rms(p[:, :C_Q_RANK], qg_ref[...]).astype(BF16)
    ckv = _rms(p[:, C_Q_RANK:C_Q_RANK + C_KV_RANK], kvg_ref[...]).astype(BF16)
    base = C_Q_RANK + C_KV_RANK
    kr = p[:, base:base + LANE] * ck_ref[...] + p[:, base + LANE:base + 2 * LANE] * sk_ref[...]

    q_plain = jnp.dot(cq, wq_ref[...], preferred_element_type=F32)
    q_swap = jnp.dot(cq, wqs_ref[...], preferred_element_type=F32)
    cq_t, sq_t = cq_ref[...], sq_ref[...]
    for h in range(C_HEADS):
        cols = slice(h * HEAD_PAD, (h + 1) * HEAD_PAD)
        q_ref[:, cols] = ((q_plain[:, cols] * cq_t + q_swap[:, cols] * sq_t) * (C_SCALE * LOG2_E)).astype(q_ref.dtype)
    k_cat = jnp.dot(ckv, wk_ref[...], preferred_element_type=F32) + _mm(kr, place_ref[...])
    k_ref[...] = k_cat.astype(k_ref.dtype)
    v_ref[...] = (jnp.dot(ckv, wv_ref[...], preferred_element_type=F32) + vone_ref[...]).astype(v_ref.dtype)


def _odd_in(h, mod, ng, wts, rope):
    tm = TM
    n_tiles = N_ROWS // tm
    consts = [ng, wts["w_in"], wts["q_norm_g"], wts["kv_norm_g"], wts["w_q"], wts["w_q_swap"],
              wts["w_k"], wts["w_v"], wts["place"], wts["v_one"]]
    ctx_tiles = NC_ROWS // tm
    lat_tiles = SEQ // tm

    def rope_spec():
        return pl.BlockSpec((tm, LANE), lambda i: (jnp.where(i < ctx_tiles, lat_tiles, (i - ctx_tiles) % lat_tiles), 0))

    return pl.pallas_call(
        _odd_in_kernel,
        out_shape=(jax.ShapeDtypeStruct((N_ROWS, C_QK_PAD), BF16),) * 3,
        grid=(n_tiles,),
        in_specs=[_rows(tm, D_MODEL), _mod_spec(tm)] + [_const(a.shape) for a in consts] + [rope_spec()] * 4,
        out_specs=(_rows(tm, C_QK_PAD),) * 3,
        compiler_params=_params("arbitrary"),
        name="odd_in",
    )(h, mod, *consts, *rope)


def _attn_kernel(*refs, n_pieces):
    q_ref = refs[0]
    k_refs = refs[1:1 + n_pieces]
    v_refs = refs[1 + n_pieces:1 + 2 * n_pieces]
    o_ref = refs[-1]
    lane = lax.broadcasted_iota(jnp.int32, (ATT_QS, LANE), 1)
    for pair in range(ATT_HG // 2):
        for qs in range(q_ref.shape[0] // ATT_QS):
            rows = slice(qs * ATT_QS, (qs + 1) * ATT_QS)
            pair_out = None
            for hh in range(2):
                cols = slice((2 * pair + hh) * HEAD_PAD, (2 * pair + hh + 1) * HEAD_PAD)
                q = q_ref[rows, cols]
                m = jnp.full((ATT_QS, 1), -jnp.inf, F32)
                acc = jnp.zeros((ATT_QS, LANE), F32)
                for k_ref, v_ref in zip(k_refs, v_refs):
                    for c in range(k_ref.shape[0] // ATT_KC):
                        keys = slice(c * ATT_KC, (c + 1) * ATT_KC)
                        s = lax.dot_general(q, k_ref[keys, cols], (((1,), (1,)), ((), ())),
                                            preferred_element_type=F32)
                        m_new = jnp.maximum(m, jnp.max(s, axis=-1, keepdims=True))
                        p = jnp.exp2(s - m_new).astype(BF16)
                        acc = acc * jnp.exp2(m - m_new) + jnp.dot(p, v_ref[keys, cols], preferred_element_type=F32)
                        m = m_new
                ones_lane = C_VDIM if hh == 0 else 0
                own = (lane < C_VDIM) if hh == 0 else (lane >= C_VDIM)
                o = jnp.where(own, acc * (1.0 / acc[:, ones_lane:ones_lane + 1]), 0.0)
                pair_out = o if pair_out is None else pair_out + o
            o_ref[rows, pair * LANE:(pair + 1) * LANE] = pair_out.astype(o_ref.dtype)


def _attention(q, k, v, latent, prev=None):
    width = ATT_HG * HEAD_PAD
    n_hg = C_HEADS // ATT_HG
    if latent:
        tq = ATT_TQ
        nq = SEQ // tq
        q_spec = pl.BlockSpec((tq, width), lambda b, g, t: ((NC_ROWS + b * SEQ) // tq + t, g))
        kv_specs = [pl.BlockSpec((CTX_LEN, width), lambda b, g, t: (b, g)),
                    pl.BlockSpec((SEQ, width), lambda b, g, t: (NC_ROWS // SEQ + b, g))]
    else:
        tq = CTX_LEN
        nq = 1
        q_spec = pl.BlockSpec((tq, width), lambda b, g, t: (b, g))
        kv_specs = [pl.BlockSpec((CTX_LEN, width), lambda b, g, t: (b, g))]
    n_pieces = len(kv_specs)
    o_spec = pl.BlockSpec((tq, ATT_HG * C_VDIM), q_spec.index_map)
    ins = [q] + [k] * n_pieces + [v] * n_pieces
    in_specs = [q_spec] + kv_specs + kv_specs
    aliases = {}
    if prev is not None:
        aliases = {len(ins): 0}
        ins.append(prev)
        in_specs.append(pl.BlockSpec(memory_space=pl.ANY))
    return pl.pallas_call(
        functools.partial(_attn_kernel, n_pieces=n_pieces),
        out_shape=jax.ShapeDtypeStruct((N_ROWS, C_HEADS * C_VDIM), BF16),
        grid=(BATCH, n_hg, nq),
        in_specs=in_specs,
        out_specs=o_spec,
        input_output_aliases=aliases,
        compiler_params=_params("arbitrary", "arbitrary", "arbitrary"),
        name="attn_latent" if latent else "attn_context",
    )(*ins)


def _odd_out_kernel(h_ref, mod_ref, o_ref, w_ref, out_ref):
    y = jnp.dot(o_ref[...], w_ref[...], preferred_element_type=F32)
    out_ref[...] = h_ref[...] + mod_ref[2] * y


def _odd_out(h, mod, o, w_out, tile0, n_tiles):
    tm = TM
    return pl.pallas_call(
        _odd_out_kernel,
        out_shape=jax.ShapeDtypeStruct((N_ROWS, D_MODEL), F32),
        grid=(n_tiles,),
        in_specs=[_rows(tm, D_MODEL, tile0), _mod_spec(tm, tile0), _rows(tm, C_HEADS * C_VDIM, tile0),
                  _const(w_out.shape)],
        out_specs=_rows(tm, D_MODEL, tile0),
        input_output_aliases={0: 0},
        compiler_params=_params("arbitrary"),
        name="odd_out",
    )(h, mod, o, w_out)


def _route(m, router):
    m_hi, m_lo = _split_bf16(m)
    r_hi, r_lo = _split_bf16(router)
    dot = functools.partial(jnp.dot, preferred_element_type=F32)
    logits = dot(m_hi, r_hi) + (dot(m_hi, r_lo) + dot(m_lo, r_hi))
    lane = lax.broadcasted_iota(jnp.int32, logits.shape, 1)
    neg = -jnp.inf
    l1 = jnp.where(lane < N_EXPERTS, logits, neg)
    m1 = jnp.max(l1, axis=-1, keepdims=True)
    i1 = jnp.min(jnp.where(l1 == m1, lane, LANE), axis=-1, keepdims=True)
    l2 = jnp.where(lane == i1, neg, l1)
    m2 = jnp.max(l2, axis=-1, keepdims=True)
    i2 = jnp.min(jnp.where(l2 == m2, lane, LANE), axis=-1, keepdims=True)
    e2 = jnp.exp(m2 - m1)
    w1 = 1.0 / (1.0 + e2)
    w2 = e2 / (1.0 + e2)
    return i1, i2, w1, w2


R_E0, R_E1, R_RANK0, R_RANK1, R_W0, R_W1 = range(6)
R_POS0, R_POS1 = R_RANK0, R_RANK1
R_ROWS = 8
F_VALID, F_FIRST, F_LAST = 1, 2, 4


def _moe_route_kernel(h_ref, mod_ref, ng_ref, rt_ref, m_ref, rcol_ref, rrow_ref, cnt_ref, carry_s):
    @pl.when(pl.program_id(0) == 0)
    def _():
        carry_s[...] = jnp.zeros_like(carry_s)

    m = _rms(h_ref[...], ng_ref[...]) * (1.0 + mod_ref[4]) + mod_ref[3]
    m_ref[...] = m.astype(m_ref.dtype)
    i1, i2, w1, w2 = _route(m, rt_ref[...])
    tm = m.shape[0]
    lane = lax.broadcasted_iota(jnp.int32, (tm, LANE), 1)
    picked = jnp.where(lane == i1, 1.0, jnp.where(lane == i2, 1.0, 0.0))
    ri = lax.broadcasted_iota(jnp.int32, (tm, tm), 0)
    ci = lax.broadcasted_iota(jnp.int32, (tm, tm), 1)
    earlier = jnp.where(ci < ri, 1.0, 0.0).astype(BF16)
    before = carry_s[0:1, :]
    rank_all = jnp.dot(earlier, picked.astype(BF16), preferred_element_type=F32) + before
    rank0 = jnp.sum(jnp.where(lane == i1, rank_all, 0.0), axis=-1, keepdims=True)
    rank1 = jnp.sum(jnp.where(lane == i2, rank_all, 0.0), axis=-1, keepdims=True)
    after = before + jnp.sum(picked, axis=0, keepdims=True)
    carry_s[...] = jnp.broadcast_to(after, carry_s.shape)
    sub = lax.broadcasted_iota(jnp.int32, cnt_ref.shape, 0)
    cnt_ref[...] = jnp.where(sub == 0, before, jnp.where(sub == 1, after, 0.0))

    fields = {R_E0: i1.astype(F32), R_E1: i2.astype(F32), R_RANK0: rank0, R_RANK1: rank1, R_W0: w1, R_W1: w2}
    rcol = jnp.zeros((tm, LANE), F32)
    for k, val in fields.items():
        rcol = jnp.where(lane == k, val, rcol)
    rcol_ref[...] = rcol
    rrow_ref[...] = jnp.transpose(rcol)[:R_ROWS, :]


def _moe_route(h, mod, ng, router, tile0, n_chunks):
    tm = MOE_TC
    rows = n_chunks * tm
    return pl.pallas_call(
        _moe_route_kernel,
        out_shape=(jax.ShapeDtypeStruct((rows, D_MODEL), BF16),
                   jax.ShapeDtypeStruct((rows, LANE), F32),
                   jax.ShapeDtypeStruct((R_ROWS, rows), F32),
                   jax.ShapeDtypeStruct((n_chunks, R_ROWS, LANE), F32)),
        grid=(n_chunks,),
        in_specs=[_rows(tm, D_MODEL, tile0), _mod_spec(tm, tile0), _const(ng.shape), _const(router.shape)],
        out_specs=(_rows(tm, D_MODEL), _rows(tm, LANE),
                   pl.BlockSpec((R_ROWS, tm), lambda i: (0, i)),
                   pl.BlockSpec((None, R_ROWS, LANE), lambda i: (i, 0, 0))),
        scratch_shapes=[pltpu.VMEM((R_ROWS, LANE), F32)],
        compiler_params=_params("arbitrary"),
        name="moe_route",
    )(h, mod, ng, router)


def _moe_plan(cnt, n_chunks, n_blk):
    i32 = jnp.int32
    before = cnt[:, 0, :N_EXPERTS].astype(i32)
    after = cnt[:, 1, :N_EXPERTS].astype(i32)
    padded = (after[-1] + MOE_XT - 1) // MOE_XT * MOE_XT
    ends = jnp.cumsum(padded).astype(i32)
    base = ends - padded
    used_rows = ends[-1]
    start = base[None, :] + before
    stop = base[None, :] + after
    lo = jnp.arange(n_blk, dtype=i32) * MOE_BLK
    overlap = (start[:, :, None] < lo + MOE_BLK) & (stop[:, :, None] > lo)
    inc = jnp.any(overlap, axis=1)
    fill = (lo < used_rows) & ~jnp.any(inc, axis=0)
    inc_fill = inc.at[0].set(inc[0] | fill)
    n_items = n_chunks * N_EXPERTS + n_blk

    def items(mat):
        idx = jnp.nonzero(mat.reshape(-1), size=n_items, fill_value=-1)[0].astype(i32)
        valid = idx >= 0
        n_valid = jnp.sum(valid).astype(i32)
        idx = jnp.where(valid, idx, idx[jnp.maximum(n_valid - 1, 0)])
        outer = idx // mat.shape[1]
        inner = idx % mat.shape[1]
        prev = jnp.concatenate([jnp.full((1,), -1, i32), outer[:-1]])
        nxt = jnp.concatenate([outer[1:], jnp.full((1,), -1, i32)])
        pos = jnp.arange(n_items, dtype=i32)
        first = valid & (outer != prev)
        last = valid & ((outer != nxt) | (pos == n_valid - 1))
        flags = valid.astype(i32) * F_VALID + first.astype(i32) * F_FIRST + last.astype(i32) * F_LAST
        return outer, inner, flags

    disp_blk, disp_chunk, disp_flags = items(inc_fill.T)
    comb_chunk, comb_blk, comb_flags = items(inc)
    n_xt = n_blk * MOE_BLK // MOE_XT
    t0 = jnp.arange(n_xt, dtype=i32) * MOE_XT
    tile_used = (t0 < used_rows).astype(i32)
    tile_src = jnp.minimum(jnp.arange(n_xt, dtype=i32), used_rows // MOE_XT - 1)
    tile_e = jnp.minimum(jnp.sum(t0[:, None] >= ends[None, :], axis=1), N_EXPERTS - 1).astype(i32)[tile_src]
    return {"base": base, "disp": (disp_chunk, disp_blk, disp_flags), "comb": (comb_chunk, comb_blk, comb_flags),
            "tiles": (tile_e, tile_src, tile_used)}


def _sorted_rows(rcol, rrow, base):
    base_f = base.astype(F32)
    b0 = base_f[rcol[:, R_E0].astype(jnp.int32)]
    b1 = base_f[rcol[:, R_E1].astype(jnp.int32)]
    lane = jnp.arange(LANE)
    rcol = rcol + jnp.where(lane == R_POS0, b0[:, None], 0.0) + jnp.where(lane == R_POS1, b1[:, None], 0.0)
    sub = jnp.arange(R_ROWS)[:, None]
    rrow = rrow + jnp.where(sub == R_POS0, b0[None, :], 0.0) + jnp.where(sub == R_POS1, b1[None, :], 0.0)
    return rcol, rrow


def _moe_dispatch_kernel(c_ref, d_ref, f_ref, m_ref, rrow_ref, x_ref):
    n = pl.program_id(0)
    flag = f_ref[n]

    @pl.when((flag & F_VALID) != 0)
    def _():
        rr = rrow_ref[...]
        row0 = (d_ref[n] * MOE_BLK).astype(F32)
        rel0 = rr[R_POS0:R_POS0 + 1] - row0
        rel1 = rr[R_POS1:R_POS1 + 1] - row0
        ri = lax.broadcasted_iota(jnp.int32, (MOE_BLK, MOE_TC), 0).astype(F32)
        onehot = jnp.where(rel0 == ri, 1.0, jnp.where(rel1 == ri, 1.0, 0.0)).astype(BF16)
        rows = jnp.dot(onehot, m_ref[...], preferred_element_type=F32).astype(x_ref.dtype)

        @pl.when((flag & F_FIRST) != 0)
        def _():
            x_ref[...] = rows

        @pl.when((flag & F_FIRST) == 0)
        def _():
            x_ref[...] = x_ref[...] + rows


def _moe_dispatch(m, rrow, plan, n_blk):
    chunk, blk, flags = plan["disp"]
    n_items = chunk.shape[0]
    grid_spec = pltpu.PrefetchScalarGridSpec(
        num_scalar_prefetch=3,
        grid=(n_items,),
        in_specs=[pl.BlockSpec((MOE_TC, D_MODEL), lambda n, c, d, f: (c[n], 0)),
                  pl.BlockSpec((R_ROWS, MOE_TC), lambda n, c, d, f: (0, c[n]))],
        out_specs=pl.BlockSpec((MOE_BLK, D_MODEL), lambda n, c, d, f: (d[n], 0)),
    )
    return pl.pallas_call(
        _moe_dispatch_kernel,
        out_shape=jax.ShapeDtypeStruct((n_blk * MOE_BLK, D_MODEL), BF16),
        grid_spec=grid_spec,
        compiler_params=_params("arbitrary"),
        name="moe_dispatch",
    )(chunk, blk, flags, m, rrow)


def _moe_experts_kernel(e_ref, src_ref, used_ref, x_ref, wg_ref, wu_ref, wd_ref, y_ref, acc_s):
    t = pl.program_id(0)
    j = pl.program_id(1)
    last = j == pl.num_programs(1) - 1
    used = used_ref[t] != 0

    @pl.when(used)
    def _():
        x = x_ref[...]
        gate = jnp.dot(x, wg_ref[...].astype(BF16), preferred_element_type=F32)
        up = jnp.dot(x, wu_ref[...].astype(BF16), preferred_element_type=F32)
        act = (_silu(gate) * up).astype(BF16)
        part = jnp.dot(act, wd_ref[...].astype(BF16), preferred_element_type=F32)

        @pl.when(j == 0)
        def _():
            acc_s[...] = part

        @pl.when(j != 0)
        def _():
            acc_s[...] += part

        @pl.when(last)
        def _():
            y_ref[...] = acc_s[...].astype(y_ref.dtype)

    @pl.when(jnp.logical_not(used) & last)
    def _():
        y_ref[...] = jnp.zeros_like(y_ref)


def _moe_experts(x, plan, wts):
    tile_e, tile_src, tile_used = plan["tiles"]
    n_xt = tile_e.shape[0]
    li = wts["moe_layer"]
    th = MOE_TH
    nj = MOE_HIDDEN // th

    def hidden(t, j, used):
        return jnp.where(used[t] != 0, j, nj - 1)

    grid_spec = pltpu.PrefetchScalarGridSpec(
        num_scalar_prefetch=3,
        grid=(n_xt, nj),
        in_specs=[pl.BlockSpec((MOE_XT, D_MODEL), lambda t, j, e, s, u: (s[t], 0)),
                  pl.BlockSpec((None, None, D_MODEL, th), lambda t, j, e, s, u: (li, e[t], 0, hidden(t, j, u))),
                  pl.BlockSpec((None, None, D_MODEL, th), lambda t, j, e, s, u: (li, e[t], 0, hidden(t, j, u))),
                  pl.BlockSpec((None, None, th, D_MODEL), lambda t, j, e, s, u: (li, e[t], hidden(t, j, u), 0))],
        out_specs=pl.BlockSpec((MOE_XT, D_MODEL), lambda t, j, e, s, u: (t, 0)),
        scratch_shapes=[pltpu.VMEM((MOE_XT, D_MODEL), F32)],
    )
    return pl.pallas_call(
        _moe_experts_kernel,
        out_shape=jax.ShapeDtypeStruct((n_xt * MOE_XT, D_MODEL), BF16),
        grid_spec=grid_spec,
        compiler_params=_params("arbitrary", "arbitrary"),
        name="moe_experts",
    )(tile_e, tile_src, tile_used, x, wts["moe_g"], wts["moe_u"], wts["moe_d"])


def _moe_combine_kernel(c_ref, d_ref, f_ref, rcol_ref, y_ref, h_ref, mod_ref, *rest, final):
    fg_ref = rest[0] if final else None
    out_ref, acc_s = rest[-2:]
    n = pl.program_id(0)
    flag = f_ref[n]

    @pl.when((flag & F_VALID) != 0)
    def _():
        rc = rcol_ref[...]
        row0 = (d_ref[n] * MOE_BLK).astype(F32)
        rel0 = rc[:, R_POS0:R_POS0 + 1] - row0
        rel1 = rc[:, R_POS1:R_POS1 + 1] - row0
        li = lax.broadcasted_iota(jnp.int32, (MOE_TC, MOE_BLK), 1).astype(F32)
        gates = jnp.where(rel0 == li, rc[:, R_W0:R_W0 + 1], jnp.where(rel1 == li, rc[:, R_W1:R_W1 + 1], 0.0))
        part = jnp.dot(gates.astype(BF16), y_ref[...], preferred_element_type=F32)

        @pl.when((flag & F_FIRST) != 0)
        def _():
            acc_s[...] = part

        @pl.when((flag & F_FIRST) == 0)
        def _():
            acc_s[...] += part

        @pl.when((flag & F_LAST) != 0)
        def _():
            new_h = h_ref[...] + mod_ref[5] * acc_s[...]
            out_ref[...] = _rms(new_h, fg_ref[...]) if final else new_h


def _moe_combine(h, mod, rcol, y, plan, tile0, final_g=None):
    chunk, blk, flags = plan["comb"]
    n_items = chunk.shape[0]
    final = final_g is not None
    extra_specs = [pl.BlockSpec(final_g.shape, lambda n, c, d, f: (0, 0))] if final else []
    extra_ins = [final_g] if final else []
    out_tile0 = 0 if final else tile0
    out_rows = N_ROWS - tile0 * MOE_TC if final else N_ROWS
    grid_spec = pltpu.PrefetchScalarGridSpec(
        num_scalar_prefetch=3,
        grid=(n_items,),
        in_specs=[pl.BlockSpec((MOE_TC, LANE), lambda n, c, d, f: (c[n], 0)),
                  pl.BlockSpec((MOE_BLK, D_MODEL), lambda n, c, d, f: (d[n], 0)),
                  pl.BlockSpec((MOE_TC, D_MODEL), lambda n, c, d, f: (c[n] + tile0, 0)),
                  pl.BlockSpec((N_MOD, None, 1, D_MODEL),
                               lambda n, c, d, f: (0, _mod_row(c[n] + tile0, MOE_TC), 0, 0))] + extra_specs,
        out_specs=pl.BlockSpec((MOE_TC, D_MODEL), lambda n, c, d, f: (c[n] + out_tile0, 0)),
        scratch_shapes=[pltpu.VMEM((MOE_TC, D_MODEL), F32)],
    )
    return pl.pallas_call(
        functools.partial(_moe_combine_kernel, final=final),
        out_shape=jax.ShapeDtypeStruct((out_rows, D_MODEL), F32),
        grid_spec=grid_spec,
        input_output_aliases={} if final else {5: 0},
        compiler_params=_params("arbitrary"),
        name="moe_combine_final" if final else "moe_combine",
    )(chunk, blk, flags, rcol, y, h, mod, *extra_ins)


def _moe(h, mod, ng, wts, first_row, final_g=None):
    tile0 = first_row // MOE_TC
    n_chunks = (N_ROWS - first_row) // MOE_TC
    n_blk = (2 * n_chunks * MOE_TC + N_EXPERTS * MOE_XT) // MOE_BLK
    m, rcol, rrow, cnt = _moe_route(h, mod, ng, wts["router"], tile0, n_chunks)
    plan = _moe_plan(cnt, n_chunks, n_blk)
    rcol, rrow = _sorted_rows(rcol, rrow, plan["base"])
    x = _moe_dispatch(m, rrow, plan, n_blk)
    y = _moe_experts(x, plan, wts)
    return _moe_combine(h, mod, rcol, y, plan, tile0, final_g)


assert DEPTH % 2 == 0, "the final norm is fused into the last (odd, MoE) layer's combine kernel"


def _row_vec(x):
    return x.reshape(1, -1).astype(F32)


def _even_weights(i, ev_w_in, a_ln_g, a_ln_b, a_ws, a_bs, b_gate_w, b_gate_b, b_norm_g, ev_w_out,
                  ff_w_gate, ff_w_up, ff_w_down):
    w_in = ev_w_in[i]
    qkvr = 2 * B_QK + 2 * B_V
    gate_w = jnp.zeros((2 * B_GATE_RANK, 2 * B_QK), F32)
    gate_w = gate_w.at[:B_GATE_RANK, :B_QK].set(b_gate_w[i, 0]).at[B_GATE_RANK:, B_QK:].set(b_gate_w[i, 1])
    nch = FF_HIDDEN // FF_TH
    return {
        "w_a": w_in[:, :2 * A_WIDTH].astype(BF16),
        "w_q": w_in[:, 2 * A_WIDTH:2 * A_WIDTH + qkvr].astype(BF16),
        "w_gf": w_in[:, 2 * A_WIDTH + qkvr:].astype(BF16),
        "gate_w": gate_w.astype(BF16),
        "gate_b": b_gate_b[i].reshape(1, 2 * B_QK).astype(F32),
        "ln_g": _row_vec(a_ln_g[i]),
        "ln_b": _row_vec(a_ln_b[i]),
        "ws": a_ws[i].astype(BF16),
        "bs": jnp.broadcast_to(a_bs[i][:, :, None], (A_HEADS, A_CHUNK, A_HEAD_DIM)).astype(F32),
        "b_norm_g": _row_vec(b_norm_g[i]),
        "w_out_a": ev_w_out[i, :A_WIDTH].astype(BF16),
        "w_out_b": ev_w_out[i, A_WIDTH:].astype(BF16),
        "ff_g": jnp.transpose(ff_w_gate[i].reshape(D_MODEL, nch, FF_TH), (1, 0, 2)).astype(BF16),
        "ff_u": jnp.transpose(ff_w_up[i].reshape(D_MODEL, nch, FF_TH), (1, 0, 2)).astype(BF16),
        "ff_d": ff_w_down[i].reshape(nch, FF_TH, D_MODEL).astype(BF16),
    }


def _rope_swap_index():
    quarter = C_ROPE // 4
    idx = np.arange(C_ROPE)
    return np.concatenate([idx[quarter:2 * quarter], idx[:quarter], idx[3 * quarter:], idx[2 * quarter:3 * quarter]])


def _odd_weights(i, od_w_in, c_q_norm_g, c_w_uq, c_kv_norm_g, c_w_ukv, od_w_out,
                 moe_router, moe_w_gate, moe_w_up, moe_w_down):
    swap = _rope_swap_index()
    w_in = od_w_in[i]
    low = C_Q_RANK + C_KV_RANK
    k_rope_w = w_in[:, low:]
    pad = jnp.zeros((D_MODEL, LANE - C_ROPE), F32)
    w_in_x = jnp.concatenate([w_in[:, :low], k_rope_w, pad, k_rope_w[:, swap], pad], axis=1)

    uq = c_w_uq[i].reshape(C_Q_RANK, C_HEADS, C_NOPE + C_ROPE)
    zq = jnp.zeros((C_Q_RANK, C_HEADS, HEAD_PAD - C_NOPE - C_ROPE), F32)
    w_q = jnp.concatenate([uq, zq], axis=-1)
    w_q_swap = jnp.concatenate([jnp.zeros((C_Q_RANK, C_HEADS, C_NOPE), F32), uq[:, :, C_NOPE:][:, :, swap], zq], axis=-1)
    ukv = c_w_ukv[i].reshape(C_KV_RANK, C_HEADS, C_NOPE + C_VDIM)
    w_k = jnp.concatenate([ukv[:, :, :C_NOPE], jnp.zeros((C_KV_RANK, C_HEADS, HEAD_PAD - C_NOPE), F32)], axis=-1)
    zv = jnp.zeros((C_KV_RANK, C_HEADS // 2, C_VDIM), F32)
    uv = ukv[:, :, C_NOPE:].reshape(C_KV_RANK, C_HEADS // 2, 2, C_VDIM)
    w_v = jnp.stack([jnp.concatenate([uv[:, :, 0], zv], axis=-1),
                     jnp.concatenate([zv, uv[:, :, 1]], axis=-1)], axis=2)
    v_one = np.zeros((C_HEADS // 2, 2, HEAD_PAD), np.float32)
    v_one[:, 0, C_VDIM] = 1.0
    v_one[:, 1, 0] = 1.0
    place = np.zeros((LANE, C_HEADS, HEAD_PAD), np.float32)
    for r in range(C_ROPE):
        place[r, :, C_NOPE + r] = 1.0
    router = jnp.concatenate([moe_router[i], jnp.zeros((D_MODEL, LANE - N_EXPERTS), F32)], axis=1)
    return {
        "w_in": w_in_x.astype(BF16),
        "q_norm_g": _row_vec(c_q_norm_g[i]),
        "kv_norm_g": _row_vec(c_kv_norm_g[i]),
        "w_q": w_q.reshape(C_Q_RANK, C_QK_PAD).astype(BF16),
        "w_q_swap": w_q_swap.reshape(C_Q_RANK, C_QK_PAD).astype(BF16),
        "w_k": w_k.reshape(C_KV_RANK, C_QK_PAD).astype(BF16),
        "w_v": w_v.reshape(C_KV_RANK, C_QK_PAD).astype(BF16),
        "place": jnp.asarray(place.reshape(LANE, C_QK_PAD), BF16),
        "v_one": jnp.asarray(v_one.reshape(1, C_QK_PAD), F32),
        "w_out": od_w_out[i].astype(BF16),
        "router": router,
        "moe_layer": i,
        "moe_g": moe_w_gate,
        "moe_u": moe_w_up,
        "moe_d": moe_w_down,
    }


def _rope_tables(tm):
    quarter = C_ROPE // 4
    half = C_ROPE // 2
    inv_freq = ROPE_BASE ** (-jnp.arange(0, half, 2, dtype=F32) / half)
    pos = jnp.arange(SEQ, dtype=jnp.int32)
    ang_r = (pos // GRID_W).astype(F32)[:, None] * inv_freq
    ang_c = (pos % GRID_W).astype(F32)[:, None] * inv_freq
    cos = jnp.concatenate([jnp.cos(ang_r)] * 2 + [jnp.cos(ang_c)] * 2, axis=1)
    sin = jnp.concatenate([-jnp.sin(ang_r), jnp.sin(ang_r), -jnp.sin(ang_c), jnp.sin(ang_c)], axis=1)
    cos = jnp.concatenate([cos, jnp.ones((tm, C_ROPE), F32)], axis=0)
    sin = jnp.concatenate([sin, jnp.zeros((tm, C_ROPE), F32)], axis=0)
    rows = SEQ + tm
    ones = jnp.ones((rows, C_NOPE), F32)
    z = lambda w: jnp.zeros((rows, w), F32)
    cq = jnp.concatenate([ones, cos, z(HEAD_PAD - C_NOPE - C_ROPE)], axis=1)
    sq = jnp.concatenate([z(C_NOPE), sin, z(HEAD_PAD - C_NOPE - C_ROPE)], axis=1)
    ck = jnp.concatenate([cos, z(LANE - C_ROPE)], axis=1)
    sk = jnp.concatenate([sin, z(LANE - C_ROPE)], axis=1)
    return cq, sq, ck, sk


def kernel(x, c, ctx, c_ctx, mod_w, mod_b, norm_mix_g, norm_ffn_g, final_g, ev_w_in, a_ln_g, a_ln_b, a_ws, a_bs, b_gate_w, b_gate_b, b_norm_g, ev_w_out, od_w_in, c_q_norm_g, c_w_uq, c_kv_norm_g, c_w_ukv, od_w_out, ff_w_gate, ff_w_up, ff_w_down, moe_router, moe_w_gate, moe_w_up, moe_w_down):
    h = jnp.concatenate([ctx.reshape(NC_ROWS, D_MODEL), x.reshape(NL_ROWS, D_MODEL)], axis=0)
    cond = jnp.concatenate([c, c_ctx[None, :], jnp.zeros((MOD_PAD - MOD_ROWS, D_MODEL), F32)], axis=0)
    mod = _modulation(cond, mod_w, mod_b)
    rope = _rope_tables(TM)

    for layer in range(DEPTH):
        i = layer // 2
        need_ctx = layer < DEPTH - 1
        ng_mix = _row_vec(norm_mix_g[layer])
        ng_ffn = _row_vec(norm_ffn_g[layer])
        if layer % 2 == 0:
            wts = _even_weights(i, ev_w_in, a_ln_g, a_ln_b, a_ws, a_bs, b_gate_w, b_gate_b, b_norm_g,
                                ev_w_out, ff_w_gate, ff_w_up, ff_w_down)
            ya, qkv, r, g = _even_in(h, mod[layer], ng_mix, wts)
            o_f, o_b = _gla(qkv, g)
            h = _even_out(h, mod[layer], o_f, o_b, r, ya, wts)
            h = _ffn(h, mod[layer], ng_ffn, wts)
        else:
            wts = _odd_weights(i, od_w_in, c_q_norm_g, c_w_uq, c_kv_norm_g, c_w_ukv, od_w_out,
                               moe_router, moe_w_gate, moe_w_up, moe_w_down)
            q, k, v = _odd_in(h, mod[layer], ng_mix, wts, rope)
            o = _attention(q, k, v, latent=True)
            if need_ctx:
                o = _attention(q, k, v, latent=False, prev=o)
            first = 0 if need_ctx else NC_ROWS
            h = _odd_out(h, mod[layer], o, wts["w_out"], first // TM, (N_ROWS - first) // TM)
            h = _moe(h, mod[layer], ng_ffn, wts, first, None if need_ctx else _row_vec(final_g))
    return h.reshape(BATCH, SEQ, D_MODEL)
```
